```python
import math
import jax
import jax.numpy as jnp
from jax import lax
import numpy as np

D_MODEL = 1024
BATCH = 4
SEQ = 8192
DEPTH = 1

GRID_W = 64
CTX_LEN = 256
EXPAND = 2
D_MIX = EXPAND * D_MODEL
D_A = D_MIX // 2
D_B = D_MIX - D_A
HA_HEADS = 8
HA_DK = 128
HA_DV = D_A // HA_HEADS
A_QK = HA_HEADS * HA_DK
HB_HEADS = 8
HB_DK = 128
HB_DV = D_B // HB_HEADS
B_QK = HB_HEADS * HB_DK
CONV_K = 3
CONV_CH = 2 * B_QK + D_B
CHUNK = 64
NORM_EPS = 1e-6
COL_SIZES = (A_QK, A_QK, A_QK, D_A, D_A, B_QK, B_QK, D_B, D_B, 2 * HB_HEADS, 2 * HB_HEADS)
N_IN = 3 * A_QK + 2 * D_A + 2 * B_QK + 2 * D_B + 4 * HB_HEADS

kernel_name = "hybrid_hgrn2_gdn_flow_block"


def rmsnorm(x, w):
    xf = x.astype(jnp.float32)
    y = xf * lax.rsqrt(jnp.mean(xf * xf, axis=-1, keepdims=True) + NORM_EPS)
    return (y * w.astype(jnp.float32)).astype(x.dtype)


def l2norm(t):
    return t * lax.rsqrt(jnp.sum(t * t, axis=-1, keepdims=True) + NORM_EPS)


def heads(t, n_heads):
    bsz, length, width = t.shape
    return t.reshape(bsz, length, n_heads, width // n_heads).transpose(0, 2, 1, 3)


def merge_heads(t):
    bsz, nh, length, d = t.shape
    return t.transpose(0, 2, 1, 3).reshape(bsz, length, nh * d)


def split_columns(y):
    out, start = [], 0
    for size in COL_SIZES:
        out.append(y[..., start:start + size])
        start += size
    return out


def adaln(cond, ada_w, ada_b):
    m = jax.nn.silu(cond) @ ada_w + ada_b
    return jnp.split(m, 3, axis=-1)


def grid_conv(t, conv_w, rows):
    bsz, length, ch = t.shape
    img = t.reshape(bsz, rows, length // rows, ch)
    out = lax.conv_general_dilated(img, conv_w.astype(t.dtype)[:, :, None, :], (1, 1), "SAME",
                                   dimension_numbers=("NHWC", "HWIO", "NHWC"),
                                   feature_group_count=ch)
    return out.reshape(bsz, length, ch)


def hgrn2_scan(q, k, v, g, s0, readout):
    bsz, nh, length, dk = k.shape
    dv = v.shape[-1]
    n = length // CHUNK
    blk = lambda t: t.reshape(bsz, nh, n, CHUNK, t.shape[-1])
    q, k, v, g = blk(q), blk(k), blk(v), blk(g)
    b = jnp.cumsum(g, axis=-2)
    b_last = b[..., -1:, :]
    xs = [k * jnp.exp(b_last - b), v, jnp.exp(b_last)]
    if readout:
        xs += [q * jnp.exp(b), q, k, b]
    xs = tuple(jnp.moveaxis(t, 2, 0) for t in xs)
    incl = jnp.tril(jnp.ones((CHUNK, CHUNK), dtype=bool))[:, :, None]

    def step(s, inp):
        k_dec, v_i, d_last = inp[:3]
        s_new = d_last[..., 0, :, None] * s + jnp.einsum("bhsk,bhsv->bhkv", k_dec, v_i)
        if not readout:
            return s_new, None
        q_dec, q_i, k_i, b_i = inp[3:]
        diff = b_i[..., :, None, :] - b_i[..., None, :, :]
        dec = jnp.exp(jnp.where(incl, diff, -jnp.inf))
        scores = jnp.einsum("bhtk,bhsk,bhtsk->bhts", q_i, k_i, dec)
        o = jnp.einsum("bhtk,bhkv->bhtv", q_dec, s) + jnp.einsum("bhts,bhsv->bhtv", scores, v_i)
        return s_new, o

    s_fin, o = lax.scan(step, s0, xs)
    if readout:
        o = jnp.moveaxis(o, 0, 2).reshape(bsz, nh, length, dv)
    return o, s_fin


def gdn_scan(q, k, v, g, beta, s0, readout):
    bsz, nh, length, dk = k.shape
    dv = v.shape[-1]
    n = length // CHUNK
    blk = lambda t: t.reshape(bsz, nh, n, CHUNK, t.shape[-1])
    q, k, v = blk(q), blk(k), blk(v)
    g, beta = g.reshape(bsz, nh, n, CHUNK), beta.reshape(bsz, nh, n, CHUNK)
    b = jnp.cumsum(g, axis=-1)
    diff = b[..., :, None] - b[..., None, :]
    strict = jnp.tril(jnp.ones((CHUNK, CHUNK), dtype=bool), -1)
    incl = jnp.tril(jnp.ones((CHUNK, CHUNK), dtype=bool))
    kk = jnp.einsum("bhntk,bhnsk->bhnts", k, k)
    a_mat = jnp.eye(CHUNK, dtype=k.dtype) + beta[..., :, None] * kk * jnp.exp(jnp.where(strict, diff, -jnp.inf))
    rhs = jnp.concatenate([beta[..., None] * v, (beta * jnp.exp(b))[..., None] * k], axis=-1)
    sol = lax.linalg.triangular_solve(a_mat, rhs, left_side=True, lower=True, unit_diagonal=True)
    u0, w = sol[..., :dv], sol[..., dv:]
    b_last = b[..., -1:]
    xs = [u0, w, k * jnp.exp(b_last - b)[..., None], jnp.exp(b_last[..., 0])]
    if readout:
        p = jnp.einsum("bhntk,bhnsk->bhnts", q, k) * jnp.exp(jnp.where(incl, diff, -jnp.inf))
        xs += [q * jnp.exp(b)[..., None], p]
    xs = tuple(jnp.moveaxis(t, 2, 0) for t in xs)

    def step(s, inp):
        u0_i, w_i, k_dec, d_last = inp[:4]
        v_new = u0_i - jnp.einsum("bhck,bhkv->bhcv", w_i, s)
        s_new = d_last[..., None, None] * s + jnp.einsum("bhck,bhcv->bhkv", k_dec, v_new)
        if not readout:
            return s_new, None
        q_dec, p_i = inp[4:]
        o = jnp.einsum("bhtk,bhkv->bhtv", q_dec, s) + jnp.einsum("bhts,bhsv->bhtv", p_i, v_new)
        return s_new, o

    s_fin, o = lax.scan(step, s0, xs)
    if readout:
        o = jnp.moveaxis(o, 0, 2).reshape(bsz, nh, length, dv)
    return o, s_fin


def bidirectional(scan_fn, ctx_args, lat_args, s0, readout_ctx):
    flip = lambda t: jnp.flip(t, axis=2)
    (ctx_f, ctx_b), (lat_f, lat_b) = ctx_args, lat_args
    oc_f, sc_f = scan_fn(*ctx_f, s0, readout_ctx)
    oc_b, sc_b = scan_fn(*(flip(t) for t in ctx_b), s0, readout_ctx)
    ol_f, _ = scan_fn(*lat_f, sc_f, True)
    ol_b, _ = scan_fn(*(flip(t) for t in lat_b), sc_b, True)
    o_ctx = oc_f + flip(oc_b) if readout_ctx else None
    return ol_f + flip(ol_b), o_ctx


def project(h, w_in, conv_w, lb, a_log, dt_bias, rows):
    f32 = jnp.float32
    y = jnp.einsum("bld,dn->bln", h, w_in)
    a_q, a_f_fwd, a_f_bwd, a_i, a_z, b_q, b_k, b_v, b_z, b_a, b_b = split_columns(y)
    bsz, length, _ = y.shape
    q_a = heads(jax.nn.silu(a_q.astype(f32)), HA_HEADS) * HA_DK ** -0.5
    v_a = heads(a_i.astype(f32), HA_HEADS)

    def forget(fz, lb_d):
        fz = fz.astype(f32)
        log_f = jnp.logaddexp(jnp.log(lb_d), jnp.log1p(-lb_d) + jax.nn.log_sigmoid(fz))
        k = (1.0 - lb_d) * jax.nn.sigmoid(-fz)
        return heads(k, HA_HEADS), heads(log_f, HA_HEADS)

    k_af, g_af = forget(a_f_fwd, lb[0])
    k_ab, g_ab = forget(a_f_bwd, lb[1])
    hg_args = ((q_a, k_af, v_a, g_af), (q_a, k_ab, v_a, g_ab))
    qkv = jax.nn.silu(grid_conv(jnp.concatenate([b_q, b_k, b_v], axis=-1), conv_w, rows).astype(f32))
    q_b = l2norm(heads(qkv[..., :B_QK], HB_HEADS)) * HB_DK ** -0.5
    k_b = l2norm(heads(qkv[..., B_QK:2 * B_QK], HB_HEADS))
    v_b = heads(qkv[..., 2 * B_QK:], HB_HEADS)
    alpha_logit = b_a.astype(f32).reshape(bsz, length, 2, HB_HEADS)
    beta = jax.nn.sigmoid(b_b.astype(f32).reshape(bsz, length, 2, HB_HEADS))
    log_alpha = -jnp.exp(a_log.astype(f32)) * jax.nn.softplus(alpha_logit + dt_bias.astype(f32))
    log_alpha = log_alpha.transpose(2, 0, 3, 1)
    beta = beta.transpose(2, 0, 3, 1)
    gdn_args = ((q_b, k_b, v_b, log_alpha[0], beta[0]), (q_b, k_b, v_b, log_alpha[1], beta[1]))
    return hg_args, gdn_args, a_z.astype(f32), b_z.astype(f32)


def group_out(o_a, o_b, z_a, z_b, na_w, nb_w, w_out, dtype):
    def head_norm(o, w):
        o = o * lax.rsqrt(jnp.mean(o * o, axis=-1, keepdims=True) + NORM_EPS)
        return merge_heads(o * w.astype(jnp.float32)[None, :, None, :])

    y_a = jax.nn.silu(z_a) * head_norm(o_a, na_w)
    y_b = jax.nn.silu(z_b) * head_norm(o_b, nb_w)
    y = jnp.concatenate([y_a, y_b], axis=-1).astype(dtype)
    return y @ w_out


def setup_inputs(seed: int = 0) -> dict:
    key = jax.random.key(seed)
    ks = jax.random.split(key, 16)
    f32 = jnp.float32
    nrm = lambda k, shape, s: s * jax.random.normal(k, shape, f32)
    x = nrm(ks[0], (BATCH, SEQ, D_MODEL), 1.0)
    c = nrm(ks[1], (BATCH, D_MODEL), 1.0)
    ctx = nrm(ks[2], (BATCH, CTX_LEN, D_MODEL), 1.0)
    c_ctx = nrm(ks[3], (D_MODEL,), 1.0)
    norm_w = 1.0 + nrm(ks[4], (DEPTH, D_MODEL), 0.02)
    ada_w = nrm(ks[5], (DEPTH, D_MODEL, 3 * D_MODEL), 0.5 * D_MODEL ** -0.5)
    ada_b = nrm(ks[6], (DEPTH, 3 * D_MODEL), 0.01)
    w_in = nrm(ks[7], (DEPTH, D_MODEL, N_IN), D_MODEL ** -0.5)
    conv_w = nrm(ks[8], (DEPTH, CONV_K, CONV_K, CONV_CH), 1.0 / CONV_K)
    hg_lb_logits = nrm(ks[9], (DEPTH + 1, 2, A_QK), 0.5)
    gdn_a_log = jnp.log(jax.random.uniform(ks[10], (DEPTH, 2, HB_HEADS), f32, 1.0, 16.0))
    dt = jnp.exp(jax.random.uniform(ks[11], (DEPTH, 2, HB_HEADS), f32, math.log(1e-3), math.log(1e-1)))
    gdn_dt_bias = dt + jnp.log(-jnp.expm1(-dt))
    ha_norm_w = 1.0 + nrm(ks[12], (DEPTH, HA_HEADS, HA_DV), 0.02)
    hb_norm_w = 1.0 + nrm(ks[13], (DEPTH, HB_HEADS, HB_DV), 0.02)
    w_out = nrm(ks[14], (DEPTH, D_MIX, D_MODEL), D_MIX ** -0.5)
    final_norm_w = 1.0 + nrm(ks[15], (D_MODEL,), 0.02)
    return {"x": x, "c": c, "ctx": ctx, "c_ctx": c_ctx, "norm_w": norm_w, "ada_w": ada_w,
            "ada_b": ada_b, "w_in": w_in, "conv_w": conv_w, "hg_lb_logits": hg_lb_logits,
            "gdn_a_log": gdn_a_log, "gdn_dt_bias": gdn_dt_bias, "ha_norm_w": ha_norm_w,
            "hb_norm_w": hb_norm_w, "w_out": w_out, "final_norm_w": final_norm_w}


def reference(x, c, ctx, c_ctx, norm_w, ada_w, ada_b, w_in, conv_w, hg_lb_logits, gdn_a_log,
              gdn_dt_bias, ha_norm_w, hb_norm_w, w_out, final_norm_w):
    bsz = x.shape[0]
    rows = x.shape[1] // GRID_W
    lb_all = jnp.cumsum(jax.nn.softmax(hg_lb_logits.astype(jnp.float32), axis=0), axis=0)
    s0_a = jnp.zeros((bsz, HA_HEADS, HA_DK, HA_DV), jnp.float32)
    s0_b = jnp.zeros((bsz, HB_HEADS, HB_DK, HB_DV), jnp.float32)
    for layer in range(DEPTH):
        readout_ctx = layer + 1 < DEPTH
        sh_l, sc_l, gt_l = adaln(c, ada_w[layer], ada_b[layer])
        sh_c, sc_c, gt_c = adaln(c_ctx, ada_w[layer], ada_b[layer])
        h_lat = rmsnorm(x, norm_w[layer]) * (1.0 + sc_l[:, None, :]) + sh_l[:, None, :]
        h_ctx = rmsnorm(ctx, norm_w[layer]) * (1.0 + sc_c) + sh_c
        lat_hg, lat_gdn, lz_a, lz_b = project(h_lat, w_in[layer], conv_w[layer], lb_all[layer],
                                              gdn_a_log[layer], gdn_dt_bias[layer], rows)
        ctx_hg, ctx_gdn, cz_a, cz_b = project(h_ctx, w_in[layer], conv_w[layer], lb_all[layer],
                                              gdn_a_log[layer], gdn_dt_bias[layer], 1)
        oa_lat, oa_ctx = bidirectional(hgrn2_scan, ctx_hg, lat_hg, s0_a, readout_ctx)
        ob_lat, ob_ctx = bidirectional(gdn_scan, ctx_gdn, lat_gdn, s0_b, readout_ctx)
        mix_lat = group_out(oa_lat, ob_lat, lz_a, lz_b, ha_norm_w[layer], hb_norm_w[layer], w_out[layer], x.dtype)
        if readout_ctx:
            mix_ctx = group_out(oa_ctx, ob_ctx, cz_a, cz_b, ha_norm_w[layer], hb_norm_w[layer], w_out[layer], ctx.dtype)
            ctx = ctx + gt_c * mix_ctx
        x = x + gt_l[:, None, :] * mix_lat
    return rmsnorm(x, final_norm_w)
```

```python
import functools

import jax
import jax.numpy as jnp
from jax import lax
from jax.experimental import pallas as pl
from jax.experimental.pallas import tpu as pltpu

F32 = jnp.float32
BF16 = jnp.bfloat16

D_MODEL = 1024
N_HEADS = 8
HEAD_DIM = 128
CHUNK = 64
GRID_W = 64
NORM_EPS = 1e-6
N_MAIN = 72 * HEAD_DIM
N_GATE = 4 * N_HEADS
OFF_AQ, OFF_AFF, OFF_AFB, OFF_AI, OFF_AZ, OFF_BQ, OFF_BZ = 0, 8, 16, 24, 32, 40, 64
NEG_BIG = -1e30
VMEM_LIMIT = 56 * 1024 * 1024


def _dot(a, b):
    return jnp.dot(a, b, preferred_element_type=F32)


def _dot_nt(a, b):
    return lax.dot_general(a, b, (((1,), (1,)), ((), ())), preferred_element_type=F32)


def _sigmoid(x):
    return 1.0 / (1.0 + jnp.exp(-x))


def _silu(x):
    return x * _sigmoid(x)


def _adaln_body(c_ref, w_ref, b_ref, o_ref):
    o_ref[...] = _dot(_silu(c_ref[...]), w_ref[...]) + b_ref[...]


def _adaln(cond, ada_w, ada_b):
    rows, d = cond.shape
    n = ada_w.shape[1]
    tn = 1024
    return pl.pallas_call(
        _adaln_body,
        grid=(n // tn,),
        in_specs=[pl.BlockSpec((rows, d), lambda j: (0, 0)),
                  pl.BlockSpec((d, tn), lambda j: (0, j)),
                  pl.BlockSpec((1, tn), lambda j: (0, j))],
        out_specs=pl.BlockSpec((rows, tn), lambda j: (0, j)),
        out_shape=jax.ShapeDtypeStruct((rows, n), F32),
        compiler_params=pltpu.CompilerParams(dimension_semantics=("arbitrary",),
                                             vmem_limit_bytes=VMEM_LIMIT),
    )(cond, ada_w, ada_b.reshape(1, n))


def _inproj_body(x_ref, nw_ref, sc_ref, sh_ref, w_ref, wg_ref, y_ref, yg_ref, h_scr, *, tn):
    @pl.when(pl.program_id(2) == 0)
    def _():
        x = x_ref[0]
        ms = jnp.mean(x * x, axis=-1, keepdims=True)
        h = x * lax.rsqrt(ms + NORM_EPS) * nw_ref[...]
        h = (h * (1.0 + sc_ref[0]) + sh_ref[0]).astype(BF16)
        h_scr[...] = h
        yg_ref[0] = _dot_nt(wg_ref[...], h)

    acc = _dot(h_scr[...], w_ref[...])
    for j in range(tn // HEAD_DIM):
        y_ref[0, j] = acc[:, j * HEAD_DIM:(j + 1) * HEAD_DIM]


def _inproj(x, norm_w, scale, shift, w_main, w_gate_t, tm):
    bsz, length, d = x.shape
    tn = 1024
    grid = (bsz, length // tm, N_MAIN // tn)
    return pl.pallas_call(
        functools.partial(_inproj_body, tn=tn),
        grid=grid,
        in_specs=[pl.BlockSpec((1, tm, d), lambda b, m, n: (b, m, 0)),
                  pl.BlockSpec((1, d), lambda b, m, n: (0, 0)),
                  pl.BlockSpec((1, 1, d), lambda b, m, n: (b, 0, 0)),
                  pl.BlockSpec((1, 1, d), lambda b, m, n: (b, 0, 0)),
                  pl.BlockSpec((d, tn), lambda b, m, n: (0, n)),
                  pl.BlockSpec((N_GATE, d), lambda b, m, n: (0, 0))],
        out_specs=[pl.BlockSpec((1, tn // HEAD_DIM, tm, HEAD_DIM), lambda b, m, n: (b, n, m, 0)),
                   pl.BlockSpec((1, N_GATE, tm), lambda b, m, n: (b, 0, m))],
        out_shape=[jax.ShapeDtypeStruct((bsz, N_MAIN // HEAD_DIM, length, HEAD_DIM), F32),
                   jax.ShapeDtypeStruct((bsz, N_GATE, length), F32)],
        scratch_shapes=[pltpu.VMEM((tm, d), BF16)],
        compiler_params=pltpu.CompilerParams(
            dimension_semantics=("parallel", "parallel", "arbitrary"),
            vmem_limit_bytes=VMEM_LIMIT),
    )(x, norm_w, scale, shift, w_main, w_gate_t)


CONV_PAD = 72
CONV_HALO = 8


def _conv_body(x_ref, w_ref, o_ref, pad_scr, *, length, width, two_d, rt):
    blk = pl.program_id(1)
    zeros = jnp.zeros((CONV_PAD, HEAD_DIM), F32)
    pad_scr[0:CONV_PAD, :] = zeros
    pad_scr[CONV_PAD + length:CONV_PAD + length + CONV_PAD, :] = zeros
    pad_scr[CONV_PAD:CONV_PAD + length, :] = x_ref[0, 0]
    w = w_ref[0]
    win_rows = rt + 2 * CONV_HALO
    is_q = blk < N_HEADS
    is_qk = blk < 2 * N_HEADS

    def tile(i, carry):
        s = pl.multiple_of(i * rt, rt)
        col = (lax.broadcasted_iota(jnp.int32, (rt, HEAD_DIM), 0) + s) % width
        acc = jnp.zeros((rt, HEAD_DIM), F32)
        for dr in ((-1, 0, 1) if two_d else (0,)):
            base = pl.multiple_of(s + (CONV_PAD + dr * width - CONV_HALO), 8)
            win = pad_scr[pl.ds(base, win_rows), :]
            for dw in (-1, 0, 1):
                if dw == 0:
                    xs = win[CONV_HALO:CONV_HALO + rt]
                else:
                    xs = pltpu.roll(win, (-dw) % win_rows, 0)[CONV_HALO:CONV_HALO + rt]
                    ok = (col >= 1) if dw < 0 else (col <= width - 2)
                    xs = jnp.where(ok, xs, 0.0)
                tap = (dr + 1) * 3 + (dw + 1)
                acc = acc + xs * w[tap:tap + 1, :]
        a = _silu(acc)
        nrm = lax.rsqrt(jnp.sum(a * a, axis=-1, keepdims=True) + NORM_EPS)
        f = jnp.where(is_q, nrm * HEAD_DIM ** -0.5, jnp.where(is_qk, nrm, 1.0))
        o_ref[0, 0, pl.ds(s, rt), :] = a * f
        return carry

    lax.fori_loop(0, length // rt, tile, 0)


def _gdn_conv(yh, conv_w, two_d):
    bsz, _, length, _ = yh.shape
    width = GRID_W if two_d else length
    rt = min(512, length)
    nblk = 3 * N_HEADS
    return pl.pallas_call(
        functools.partial(_conv_body, length=length, width=width, two_d=two_d, rt=rt),
        grid=(bsz, nblk),
        in_specs=[pl.BlockSpec((1, 1, length, HEAD_DIM), lambda b, j: (b, OFF_BQ + j, 0, 0)),
                  pl.BlockSpec((1, 9, HEAD_DIM), lambda b, j: (j, 0, 0))],
        out_specs=pl.BlockSpec((1, 1, length, HEAD_DIM), lambda b, j: (b, j, 0, 0)),
        out_shape=jax.ShapeDtypeStruct((bsz, nblk, length, HEAD_DIM), F32),
        scratch_shapes=[pltpu.VMEM((length + 2 * CONV_PAD, HEAD_DIM), F32)],
        compiler_params=pltpu.CompilerParams(dimension_semantics=("parallel", "parallel"),
                                             vmem_limit_bytes=VMEM_LIMIT),
    )(yh, conv_w)


def _cumsum_rows(g, rev, row):
    x = g
    for sh in (1, 2, 4, 8, 16, 32):
        if rev:
            x = x + jnp.where(row < CHUNK - sh, pltpu.roll(x, CHUNK - sh, 0), 0.0)
        else:
            x = x + jnp.where(row >= sh, pltpu.roll(x, sh, 0), 0.0)
    return x


def _level_ref(b, m, row):
    if m >= 8:
        parts = []
        for blk in range(CHUNK // (2 * m)):
            r = blk * 2 * m + m
            parts.append(jnp.broadcast_to(b[r:r + 1, :], (2 * m, HEAD_DIM)))
        return parts[0] if len(parts) == 1 else jnp.concatenate(parts, axis=0)
    b3 = b.reshape(CHUNK // 8, 8, HEAD_DIM)
    pick = lambda r: jnp.broadcast_to(b3[:, r:r + 1, :], b3.shape).reshape(CHUNK, HEAD_DIM)
    sub = row % 8
    if m == 4:
        return pick(4)
    if m == 2:
        return jnp.where(sub < 4, pick(2), pick(6))
    return jnp.where(sub < 2, pick(1), jnp.where(sub < 4, pick(3), jnp.where(sub < 6, pick(5), pick(7))))


def _level_mask(t_idx, s_idx, m, rev):
    split = ((t_idx ^ s_idx) >> (m.bit_length() - 1)) == 1
    t_hi = (t_idx & m) != 0
    return split & (jnp.logical_not(t_hi) if rev else t_hi)


def _hg_gate(f_raw, lb):
    e = jnp.exp(-jnp.abs(f_raw))
    r = 1.0 / (1.0 + e)
    er = e * r
    pos = f_raw >= 0
    g = jnp.log(lb + (1.0 - lb) * jnp.where(pos, r, er))
    k = (1.0 - lb) * jnp.where(pos, er, r)
    return g, k


def _hg_chunk(q_raw, f_raw, v, lb, st, rev, readout):
    row = lax.broadcasted_iota(jnp.int32, (CHUNK, HEAD_DIM), 0)
    g, k = _hg_gate(f_raw, lb)
    b = _cumsum_rows(g, rev, row)
    btot = b[0:1, :] if rev else b[CHUNK - 1:CHUNK, :]
    k_dec = (k * jnp.exp(btot - b)).astype(BF16)
    vv_t = jnp.concatenate([v, v], axis=0).T
    st_new = st * jnp.exp(btot) + _dot(vv_t[:, :CHUNK].astype(BF16), k_dec)
    if not readout:
        return None, st_new

    q = _silu(q_raw) * HEAD_DIM ** -0.5
    col = lax.broadcasted_iota(jnp.int32, (CHUNK, HEAD_DIM), 1)
    s_idx = col % CHUNK
    scores = jnp.zeros((CHUNK, HEAD_DIM), F32)
    zero_blk = jnp.zeros((CHUNK, HEAD_DIM), BF16)
    for m_a, m_b in ((32, 16), (8, 4), (2, 1)):
        a_mats = []
        for m in (m_a, m_b):
            x = jnp.exp(-jnp.abs(b - _level_ref(b, m, row)))
            t_is_q = ((row & m) == 0) if rev else ((row & m) != 0)
            a_mats.append((jnp.where(t_is_q, q, k) * x).astype(BF16))
        lhs = jnp.concatenate(a_mats, axis=1)
        rhs = jnp.concatenate([jnp.concatenate([a_mats[0], zero_blk], axis=1),
                               jnp.concatenate([zero_blk, a_mats[1]], axis=1)], axis=0)
        gp = _dot_nt(lhs, rhs)
        keep = ((col < CHUNK) & _level_mask(row, s_idx, m_a, rev)) | \
               ((col >= CHUNK) & _level_mask(row, s_idx, m_b, rev))
        scores = scores + jnp.where(keep, gp, 0.0)
    q_dec = q * jnp.exp(b)
    lhs = jnp.concatenate([scores.astype(BF16), q_dec.astype(BF16)], axis=1)
    rhs = jnp.concatenate([vv_t.astype(BF16), st.astype(BF16)], axis=1)
    o = _dot_nt(lhs, rhs) + jnp.sum(q * k, axis=-1, keepdims=True) * v
    return o, st_new


def _hg_body(qf_ref, ff_ref, vf_ref, qb_ref, fb_ref, vb_ref,
             cq_ref, cff_ref, cfb_ref, cv_ref, lb_ref, of_ref, ob_ref, sf_scr, sb_scr,
             *, n_chunks, n_ctx_chunks):
    lb_f = lb_ref[0, 0:1, :]
    lb_b = lb_ref[0, 1:2, :]

    @pl.when(pl.program_id(2) == 0)
    def _():
        sf_scr[...] = jnp.zeros_like(sf_scr)
        sb_scr[...] = jnp.zeros_like(sb_scr)

        def ctx_step(c, carry):
            rf = pl.ds(pl.multiple_of(c * CHUNK, CHUNK), CHUNK)
            rb = pl.ds(pl.multiple_of((n_ctx_chunks - 1 - c) * CHUNK, CHUNK), CHUNK)
            _, sf = _hg_chunk(None, cff_ref[0, 0, rf, :], cv_ref[0, 0, rf, :], lb_f, sf_scr[...], False, False)
            _, sb = _hg_chunk(None, cfb_ref[0, 0, rb, :], cv_ref[0, 0, rb, :], lb_b, sb_scr[...], True, False)
            sf_scr[...] = sf
            sb_scr[...] = sb
            return carry

        lax.fori_loop(0, n_ctx_chunks, ctx_step, 0)

    def step(c, carry):
        rf = pl.ds(pl.multiple_of(c * CHUNK, CHUNK), CHUNK)
        rb = pl.ds(pl.multiple_of((n_chunks - 1 - c) * CHUNK, CHUNK), CHUNK)
        o_f, sf = _hg_chunk(qf_ref[0, 0, rf, :], ff_ref[0, 0, rf, :], vf_ref[0, 0, rf, :],
                            lb_f, sf_scr[...], False, True)
        o_b, sb = _hg_chunk(qb_ref[0, 0, rb, :], fb_ref[0, 0, rb, :], vb_ref[0, 0, rb, :],
                            lb_b, sb_scr[...], True, True)
        of_ref[0, 0, rf, :] = o_f
        ob_ref[0, 0, rb, :] = o_b
        sf_scr[...] = sf
        sb_scr[...] = sb
        return carry

    lax.fori_loop(0, n_chunks, step, 0)


def _hgrn2_scan(yh, yh_ctx, lb, tb):
    bsz, _, length, _ = yh.shape
    lc = yh_ctx.shape[2]
    nt = length // tb
    blk = (1, 1, tb, HEAD_DIM)
    cblk = (1, 1, lc, HEAD_DIM)
    fwd = lambda off: pl.BlockSpec(blk, lambda b, h, t: (b, off + h, t, 0))
    bwd = lambda off: pl.BlockSpec(blk, lambda b, h, t: (b, off + h, nt - 1 - t, 0))
    ctx = lambda off: pl.BlockSpec(cblk, lambda b, h, t: (b, off + h, 0, 0))
    out_shape = jax.ShapeDtypeStruct((bsz, N_HEADS, length, HEAD_DIM), F32)
    return pl.pallas_call(
        functools.partial(_hg_body, n_chunks=tb // CHUNK, n_ctx_chunks=lc // CHUNK),
        grid=(bsz, N_HEADS, nt),
        in_specs=[fwd(OFF_AQ), fwd(OFF_AFF), fwd(OFF_AI), bwd(OFF_AQ), bwd(OFF_AFB), bwd(OFF_AI),
                  ctx(OFF_AQ), ctx(OFF_AFF), ctx(OFF_AFB), ctx(OFF_AI),
                  pl.BlockSpec((1, 2, HEAD_DIM), lambda b, h, t: (h, 0, 0))],
        out_specs=[pl.BlockSpec(blk, lambda b, h, t: (b, h, t, 0)),
                   pl.BlockSpec(blk, lambda b, h, t: (b, h, nt - 1 - t, 0))],
        out_shape=[out_shape, out_shape],
        scratch_shapes=[pltpu.VMEM((HEAD_DIM, HEAD_DIM), F32), pltpu.VMEM((HEAD_DIM, HEAD_DIM), F32)],
        compiler_params=pltpu.CompilerParams(
            dimension_semantics=("parallel", "parallel", "arbitrary"),
            vmem_limit_bytes=VMEM_LIMIT),
    )(yh, yh, yh, yh, yh, yh, yh_ctx, yh_ctx, yh_ctx, yh_ctx, lb)


def _seg_cumsum_lanes(x, rev, lane):
    total = x.shape[1]
    seg = lane % CHUNK
    for sh in (1, 2, 4, 8, 16, 32):
        if rev:
            x = x + jnp.where(seg < CHUNK - sh, pltpu.roll(x, total - sh, 1), 0.0)
        else:
            x = x + jnp.where(seg >= sh, pltpu.roll(x, sh, 1), 0.0)
    return x


def _gdn_gate_prep(g_ref, d, head, a_log, dt_bias, row_scr, col_scr, n_chunks):
    total = g_ref.shape[2]
    lane = lax.broadcasted_iota(jnp.int32, (1, total), 1)
    a_raw = g_ref[0, pl.ds(d * N_HEADS + head, 1), :]
    b_raw = g_ref[0, pl.ds(2 * N_HEADS + d * N_HEADS + head, 1), :]
    z = a_raw + dt_bias
    softplus = jnp.maximum(z, 0.0) + jnp.log1p(jnp.exp(-jnp.abs(z)))
    g = -jnp.exp(jnp.full((1, total), a_log, F32)) * softplus
    beta = _sigmoid(b_raw)
    rev = d == 1
    b = _seg_cumsum_lanes(g, rev, lane)
    tot = b + _seg_cumsum_lanes(g, not rev, lane) - g
    rows = jnp.concatenate([b, beta, tot, jnp.zeros((5, total), F32)], axis=0)
    for c in range(n_chunks):
        row_scr[c] = rows[:, c * CHUNK:(c + 1) * CHUNK]
    pad = jnp.concatenate([rows, jnp.zeros((HEAD_DIM - 8, total), F32)], axis=0)
    col_scr[...] = pad.T


def _gdn_chunk(q, k, v, rows, cols, s, rev, readout):
    t_i = lax.broadcasted_iota(jnp.int32, (CHUNK, CHUNK), 0)
    s_i = lax.broadcasted_iota(jnp.int32, (CHUNK, CHUNK), 1)
    incl = (s_i >= t_i) if rev else (s_i <= t_i)
    strict = (s_i > t_i) if rev else (s_i < t_i)
    b_row, beta_row, tot_row = rows[0:1, :], rows[1:2, :], rows[2:3, :]
    b_col, beta_col = cols[:, 0:1], cols[:, 1:2]
    dmat = jnp.exp(jnp.where(incl, b_col - b_row, NEG_BIG))
    kb = k.astype(BF16)
    if readout:
        qb = q.astype(BF16)
        qkk = _dot_nt(jnp.concatenate([qb, kb], axis=0), kb)
        qk, kk = qkk[:CHUNK], qkk[CHUNK:]
    else:
        kk = _dot_nt(kb, kb)
    n_mat = jnp.where(strict, kk * dmat, 0.0) * beta_col
    eye = (t_i == s_i).astype(F32)
    x = eye - n_mat
    y = _dot(n_mat.astype(BF16), n_mat.astype(BF16))
    for i in range(5):
        x = x + _dot(x.astype(BF16), y.astype(BF16))
        if i < 4:
            y = _dot(y.astype(BF16), y.astype(BF16))
    tb_mat = (x * beta_row).astype(BF16)
    e_col = jnp.exp(b_col)
    sb = s.astype(BF16)
    if readout:
        qks = _dot(jnp.concatenate([qb, kb], axis=0), sb)
        q_s, k_s = qks[:CHUNK], qks[CHUNK:]
    else:
        k_s = _dot(kb, sb)
    v_new = _dot(tb_mat, (v - e_col * k_s).astype(BF16))
    v_new_b = v_new.astype(BF16)
    k_t = jnp.concatenate([k, k], axis=0).T[:, :CHUNK]
    k_dec_t = (k_t * jnp.exp(tot_row - b_row)).astype(BF16)
    s_new = s * jnp.exp(tot_row[:, 0:1]) + _dot(k_dec_t, v_new_b)
    if not readout:
        return None, s_new
    o = e_col * q_s + _dot((qk * dmat).astype(BF16), v_new_b)
    return o, s_new


def _gdn_body(par_ref, qf_ref, kf_ref, vf_ref, gf_ref, qb_ref, kb_ref, vb_ref, gb_ref,
              ck_ref, cv_ref, cg_ref, of_ref, ob_ref,
              sf_scr, sb_scr, rowf_scr, rowb_scr, colf_scr, colb_scr, crow_scr, ccol_scr,
              *, n_chunks, n_ctx_chunks):
    head = pl.program_id(1)
    alog_f, alog_b = par_ref[head, 0], par_ref[head, 1]
    dtb_f, dtb_b = par_ref[head, 2], par_ref[head, 3]

    @pl.when(pl.program_id(2) == 0)
    def _():
        sf_scr[...] = jnp.zeros_like(sf_scr)
        sb_scr[...] = jnp.zeros_like(sb_scr)
        for d, alog, dtb, s_scr in ((0, alog_f, dtb_f, sf_scr), (1, alog_b, dtb_b, sb_scr)):
            _gdn_gate_prep(cg_ref, d, head, alog, dtb, crow_scr, ccol_scr, n_ctx_chunks)

            def ctx_step(c, carry, d=d, s_scr=s_scr):
                cc = (n_ctx_chunks - 1 - c) if d == 1 else c
                r = pl.ds(pl.multiple_of(cc * CHUNK, CHUNK), CHUNK)
                _, s_new = _gdn_chunk(None, ck_ref[0, 0, r, :], cv_ref[0, 0, r, :],
                                      crow_scr[cc], ccol_scr[r, :], s_scr[...], d == 1, False)
                s_scr[...] = s_new
                return carry

            lax.fori_loop(0, n_ctx_chunks, ctx_step, 0)

    _gdn_gate_prep(gf_ref, 0, head, alog_f, dtb_f, rowf_scr, colf_scr, n_chunks)
    _gdn_gate_prep(gb_ref, 1, head, alog_b, dtb_b, rowb_scr, colb_scr, n_chunks)

    def step(c, carry):
        cb = n_chunks - 1 - c
        rf = pl.ds(pl.multiple_of(c * CHUNK, CHUNK), CHUNK)
        rb = pl.ds(pl.multiple_of(cb * CHUNK, CHUNK), CHUNK)
        o_f, sf = _gdn_chunk(qf_ref[0, 0, rf, :], kf_ref[0, 0, rf, :], vf_ref[0, 0, rf, :],
                             rowf_scr[c], colf_scr[rf, :], sf_scr[...], False, True)
        o_b, sb = _gdn_chunk(qb_ref[0, 0, rb, :], kb_ref[0, 0, rb, :], vb_ref[0, 0, rb, :],
                             rowb_scr[cb], colb_scr[rb, :], sb_scr[...], True, True)
        of_ref[0, 0, rf, :] = o_f
        ob_ref[0, 0, rb, :] = o_b
        sf_scr[...] = sf
        sb_scr[...] = sb
        return carry

    lax.fori_loop(0, n_chunks, step, 0)


def _gdn_scan(qkv, gates, qkv_ctx, gates_ctx, params, tb):
    bsz, _, length, _ = qkv.shape
    lc = qkv_ctx.shape[2]
    nt = length // tb
    nc, ncc = tb // CHUNK, lc // CHUNK
    blk = (1, 1, tb, HEAD_DIM)
    cblk = (1, 1, lc, HEAD_DIM)
    fwd = lambda off: pl.BlockSpec(blk, lambda b, h, t: (b, off + h, t, 0))
    bwd = lambda off: pl.BlockSpec(blk, lambda b, h, t: (b, off + h, nt - 1 - t, 0))
    ctx = lambda off: pl.BlockSpec(cblk, lambda b, h, t: (b, off + h, 0, 0))
    out_shape = jax.ShapeDtypeStruct((bsz, N_HEADS, length, HEAD_DIM), F32)
    return pl.pallas_call(
        functools.partial(_gdn_body, n_chunks=nc, n_ctx_chunks=ncc),
        grid=(bsz, N_HEADS, nt),
        in_specs=[pl.BlockSpec(memory_space=pltpu.SMEM),
                  fwd(0), fwd(N_HEADS), fwd(2 * N_HEADS),
                  pl.BlockSpec((1, N_GATE, tb), lambda b, h, t: (b, 0, t)),
                  bwd(0), bwd(N_HEADS), bwd(2 * N_HEADS),
                  pl.BlockSpec((1, N_GATE, tb), lambda b, h, t: (b, 0, nt - 1 - t)),
                  ctx(N_HEADS), ctx(2 * N_HEADS),
                  pl.BlockSpec((1, N_GATE, lc), lambda b, h, t: (b, 0, 0))],
        out_specs=[pl.BlockSpec(blk, lambda b, h, t: (b, h, t, 0)),
                   pl.BlockSpec(blk, lambda b, h, t: (b, h, nt - 1 - t, 0))],
        out_shape=[out_shape, out_shape],
        scratch_shapes=[pltpu.VMEM((HEAD_DIM, HEAD_DIM), F32), pltpu.VMEM((HEAD_DIM, HEAD_DIM), F32),
                        pltpu.VMEM((nc, 8, CHUNK), F32), pltpu.VMEM((nc, 8, CHUNK), F32),
                        pltpu.VMEM((tb, HEAD_DIM), F32), pltpu.VMEM((tb, HEAD_DIM), F32),
                        pltpu.VMEM((ncc, 8, CHUNK), F32), pltpu.VMEM((lc, HEAD_DIM), F32)],
        compiler_params=pltpu.CompilerParams(
            dimension_semantics=("parallel", "parallel", "arbitrary"),
            vmem_limit_bytes=VMEM_LIMIT),
    )(params, qkv, qkv, qkv, gates, qkv, qkv, qkv, gates, qkv_ctx, qkv_ctx, gates_ctx)


def _out_body(oaf_ref, oab_ref, obf_ref, obb_ref, za_ref, zb_ref, naw_ref, nbw_ref,
              w_ref, x_ref, gate_ref, fw_ref, o_ref):
    parts = []
    for of_ref, ob_ref, z_ref, nw_ref in ((oaf_ref, oab_ref, za_ref, naw_ref),
                                          (obf_ref, obb_ref, zb_ref, nbw_ref)):
        for h in range(N_HEADS):
            o = of_ref[0, h] + ob_ref[0, h]
            o = o * lax.rsqrt(jnp.mean(o * o, axis=-1, keepdims=True) + NORM_EPS) * nw_ref[h:h + 1, :]
            parts.append((_silu(z_ref[0, h]) * o).astype(BF16))
    y = jnp.concatenate(parts, axis=1)
    xo = x_ref[0] + gate_ref[0] * _dot(y, w_ref[...])
    ms = jnp.mean(xo * xo, axis=-1, keepdims=True)
    o_ref[0] = xo * lax.rsqrt(ms + NORM_EPS) * fw_ref[...]


def _out_stage(oa_f, oa_b, ob_f, ob_b, yh, na_w, nb_w, w_out, x, gate, final_w, tm):
    bsz, length, d = x.shape
    hblk = (1, N_HEADS, tm, HEAD_DIM)
    ospec = pl.BlockSpec(hblk, lambda b, m: (b, 0, m, 0))
    full2 = lambda a: pl.BlockSpec(a.shape, lambda b, m: (0, 0))
    return pl.pallas_call(
        _out_body,
        grid=(bsz, length // tm),
        in_specs=[ospec, ospec, ospec, ospec,
                  pl.BlockSpec(hblk, lambda b, m: (b, OFF_AZ // N_HEADS, m, 0)),
                  pl.BlockSpec(hblk, lambda b, m: (b, OFF_BZ // N_HEADS, m, 0)),
                  full2(na_w), full2(nb_w), full2(w_out),
                  pl.BlockSpec((1, tm, d), lambda b, m: (b, m, 0)),
                  pl.BlockSpec((1, 1, d), lambda b, m: (b, 0, 0)),
                  full2(final_w)],
        out_specs=pl.BlockSpec((1, tm, d), lambda b, m: (b, m, 0)),
        out_shape=jax.ShapeDtypeStruct((bsz, length, d), F32),
        compiler_params=pltpu.CompilerParams(dimension_semantics=("parallel", "parallel"),
                                             vmem_limit_bytes=VMEM_LIMIT),
    )(oa_f, oa_b, ob_f, ob_b, yh, yh, na_w, nb_w, w_out, x, gate, final_w)


def kernel(x, c, ctx, c_ctx, norm_w, ada_w, ada_b, w_in, conv_w, hg_lb_logits, gdn_a_log,
           gdn_dt_bias, ha_norm_w, hb_norm_w, w_out, final_norm_w):
    bsz, length, d = x.shape
    lc = ctx.shape[1]
    assert d == D_MODEL and length % 512 == 0 and length % GRID_W == 0 and lc % CHUNK == 0
    assert w_in.shape[0] == 1, "single-layer block"

    n_cond = bsz + 1
    cond = jnp.concatenate([c, c_ctx[None, :], jnp.zeros((-n_cond % 8, d), F32)], axis=0)
    mod = _adaln(cond, ada_w[0], ada_b[0])
    shift, scale, gate = mod[:, :d], mod[:, d:2 * d], mod[:, 2 * d:]
    lat = lambda m: m[:bsz, None, :]
    rep_ctx = lambda m: jnp.broadcast_to(m[bsz:bsz + 1, None, :], (bsz, 1, d))

    w_main = w_in[0, :, :N_MAIN].astype(BF16)
    w_gate_t = w_in[0, :, N_MAIN:].T.astype(BF16)
    nw = norm_w[0].reshape(1, d)
    yh, gates = _inproj(x, nw, lat(scale), lat(shift), w_main, w_gate_t, tm=min(1024, length))
    yh_c, gates_c = _inproj(ctx, nw, rep_ctx(scale), rep_ctx(shift), w_main, w_gate_t, tm=lc)

    lb = jax.nn.softmax(hg_lb_logits.astype(F32), axis=0)[0]
    lb = lb.reshape(2, N_HEADS, HEAD_DIM).transpose(1, 0, 2)
    tb = min(512, length)
    oa_f, oa_b = _hgrn2_scan(yh, yh_c, lb, tb)

    cw = conv_w[0].reshape(9, 3 * N_HEADS, HEAD_DIM).transpose(1, 0, 2)
    qkv = _gdn_conv(yh, cw, two_d=True)
    qkv_c = _gdn_conv(yh_c, cw, two_d=False)
    params = jnp.concatenate([gdn_a_log[0].T, gdn_dt_bias[0].T], axis=1).astype(F32)
    ob_f, ob_b = _gdn_scan(qkv, gates, qkv_c, gates_c, params, tb)

    return _out_stage(oa_f, oa_b, ob_f, ob_b, yh, ha_norm_w[0], hb_norm_w[0],
                      w_out[0].astype(BF16), x, lat(gate), final_norm_w.reshape(1, d),
                      tm=min(256, length))
```

```python
import functools

import jax
import jax.numpy as jnp
from jax import lax
from jax.experimental import pallas as pl
from jax.experimental.pallas import tpu as pltpu

F32 = jnp.float32
BF16 = jnp.bfloat16

D_MODEL = 1024
N_HEADS = 8
HEAD_DIM = 128
CHUNK = 64
GRID_W = 64
NORM_EPS = 1e-6
N_MAIN = 72 * HEAD_DIM
N_GATE = 4 * N_HEADS
OFF_AQ, OFF_AFF, OFF_AFB, OFF_AI, OFF_AZ, OFF_BQ, OFF_BZ = 0, 8, 16, 24, 32, 40, 64
NEG_BIG = -1e30
VMEM_LIMIT = 56 * 1024 * 1024


def _dot(a, b):
    return jnp.dot(a, b, preferred_element_type=F32)


def _dot_nt(a, b):
    return lax.dot_general(a, b, (((1,), (1,)), ((), ())), preferred_element_type=F32)


def _sigmoid(x):
    return 1.0 / (1.0 + jnp.exp(-x))


def _silu(x):
    return x * _sigmoid(x)


def _adaln_body(c_ref, w_ref, b_ref, o_ref):
    o_ref[...] = _dot(_silu(c_ref[...]), w_ref[...]) + b_ref[...]


def _adaln(cond, ada_w, ada_b):
    rows, d = cond.shape
    n = ada_w.shape[1]
    tn = 1024
    return pl.pallas_call(
        _adaln_body,
        grid=(n // tn,),
        in_specs=[pl.BlockSpec((rows, d), lambda j: (0, 0)),
                  pl.BlockSpec((d, tn), lambda j: (0, j)),
                  pl.BlockSpec((1, tn), lambda j: (0, j))],
        out_specs=pl.BlockSpec((rows, tn), lambda j: (0, j)),
        out_shape=jax.ShapeDtypeStruct((rows, n), F32),
        compiler_params=pltpu.CompilerParams(dimension_semantics=("arbitrary",),
                                             vmem_limit_bytes=VMEM_LIMIT),
    )(cond, ada_w, ada_b.reshape(1, n))


def _inproj_body(x_ref, nw_ref, sc_ref, sh_ref, w_ref, wg_ref, y_ref, yg_ref, h_scr, *, tn):
    @pl.when(pl.program_id(2) == 0)
    def _():
        x = x_ref[0]
        ms = jnp.mean(x * x, axis=-1, keepdims=True)
        h = x * lax.rsqrt(ms + NORM_EPS) * nw_ref[...]
        h = (h * (1.0 + sc_ref[0]) + sh_ref[0]).astype(BF16)
        h_scr[...] = h
        yg_ref[0] = _dot_nt(wg_ref[...], h)

    acc = _dot(h_scr[...], w_ref[...])
    for j in range(tn // HEAD_DIM):
        y_ref[0, j] = acc[:, j * HEAD_DIM:(j + 1) * HEAD_DIM]


def _inproj(x, norm_w, scale, shift, w_main, w_gate_t, tm):
    bsz, length, d = x.shape
    tn = 1024
    grid = (bsz, length // tm, N_MAIN // tn)
    return pl.pallas_call(
        functools.partial(_inproj_body, tn=tn),
        grid=grid,
        in_specs=[pl.BlockSpec((1, tm, d), lambda b, m, n: (b, m, 0)),
                  pl.BlockSpec((1, d), lambda b, m, n: (0, 0)),
                  pl.BlockSpec((1, 1, d), lambda b, m, n: (b, 0, 0)),
                  pl.BlockSpec((1, 1, d), lambda b, m, n: (b, 0, 0)),
                  pl.BlockSpec((d, tn), lambda b, m, n: (0, n)),
                  pl.BlockSpec((N_GATE, d), lambda b, m, n: (0, 0))],
        out_specs=[pl.BlockSpec((1, tn // HEAD_DIM, tm, HEAD_DIM), lambda b, m, n: (b, n, m, 0)),
                   pl.BlockSpec((1, N_GATE, tm), lambda b, m, n: (b, 0, m))],
        out_shape=[jax.ShapeDtypeStruct((bsz, N_MAIN // HEAD_DIM, length, HEAD_DIM), F32),
                   jax.ShapeDtypeStruct((bsz, N_GATE, length), F32)],
        scratch_shapes=[pltpu.VMEM((tm, d), BF16)],
        compiler_params=pltpu.CompilerParams(
            dimension_semantics=("parallel", "parallel", "arbitrary"),
            vmem_limit_bytes=VMEM_LIMIT),
    )(x, norm_w, scale, shift, w_main, w_gate_t)


CONV_PAD = 72
CONV_HALO = 8


def _conv_body(x_ref, w_ref, o_ref, pad_scr, *, length, width, two_d, rt):
    blk = pl.program_id(1)
    zeros = jnp.zeros((CONV_PAD, HEAD_DIM), F32)
    pad_scr[0:CONV_PAD, :] = zeros
    pad_scr[CONV_PAD + length:CONV_PAD + length + CONV_PAD, :] = zeros
    pad_scr[CONV_PAD:CONV_PAD + length, :] = x_ref[0, 0]
    w = w_ref[0]
    win_rows = rt + 2 * CONV_HALO
    is_q = blk < N_HEADS
    is_qk = blk < 2 * N_HEADS

    def tile(i, carry):
        s = pl.multiple_of(i * rt, rt)
        col = (lax.broadcasted_iota(jnp.int32, (rt, HEAD_DIM), 0) + s) % width
        acc = jnp.zeros((rt, HEAD_DIM), F32)
        for dr in ((-1, 0, 1) if two_d else (0,)):
            base = pl.multiple_of(s + (CONV_PAD + dr * width - CONV_HALO), 8)
            win = pad_scr[pl.ds(base, win_rows), :]
            for dw in (-1, 0, 1):
                if dw == 0:
                    xs = win[CONV_HALO:CONV_HALO + rt]
                else:
                    xs = pltpu.roll(win, (-dw) % win_rows, 0)[CONV_HALO:CONV_HALO + rt]
                    ok = (col >= 1) if dw < 0 else (col <= width - 2)
                    xs = jnp.where(ok, xs, 0.0)
                tap = (dr + 1) * 3 + (dw + 1)
                acc = acc + xs * w[tap:tap + 1, :]
        a = _silu(acc)
        nrm = lax.rsqrt(jnp.sum(a * a, axis=-1, keepdims=True) + NORM_EPS)
        f = jnp.where(is_q, nrm * HEAD_DIM ** -0.5, jnp.where(is_qk, nrm, 1.0))
        o_ref[0, 0, pl.ds(s, rt), :] = a * f
        return carry

    lax.fori_loop(0, length // rt, tile, 0)


def _gdn_conv(yh, conv_w, two_d):
    bsz, _, length, _ = yh.shape
    width = GRID_W if two_d else length
    rt = min(512, length)
    nblk = 3 * N_HEADS
    return pl.pallas_call(
        functools.partial(_conv_body, length=length, width=width, two_d=two_d, rt=rt),
        grid=(bsz, nblk),
        in_specs=[pl.BlockSpec((1, 1, length, HEAD_DIM), lambda b, j: (b, OFF_BQ + j, 0, 0)),
                  pl.BlockSpec((1, 9, HEAD_DIM), lambda b, j: (j, 0, 0))],
        out_specs=pl.BlockSpec((1, 1, length, HEAD_DIM), lambda b, j: (b, j, 0, 0)),
        out_shape=jax.ShapeDtypeStruct((bsz, nblk, length, HEAD_DIM), F32),
        scratch_shapes=[pltpu.VMEM((length + 2 * CONV_PAD, HEAD_DIM), F32)],
        compiler_params=pltpu.CompilerParams(dimension_semantics=("parallel", "parallel"),
                                             vmem_limit_bytes=VMEM_LIMIT),
    )(yh, conv_w)


def _cumsum_rows(g, rev, row):
    x = g
    for sh in (1, 2, 4, 8, 16, 32):
        if rev:
            x = x + jnp.where(row < CHUNK - sh, pltpu.roll(x, CHUNK - sh, 0), 0.0)
        else:
            x = x + jnp.where(row >= sh, pltpu.roll(x, sh, 0), 0.0)
    return x


def _level_ref(b, m, row):
    if m >= 8:
        parts = []
        for blk in range(CHUNK // (2 * m)):
            r = blk * 2 * m + m
            parts.append(jnp.broadcast_to(b[r:r + 1, :], (2 * m, HEAD_DIM)))
        return parts[0] if len(parts) == 1 else jnp.concatenate(parts, axis=0)
    b3 = b.reshape(CHUNK // 8, 8, HEAD_DIM)
    pick = lambda r: jnp.broadcast_to(b3[:, r:r + 1, :], b3.shape).reshape(CHUNK, HEAD_DIM)
    sub = row % 8
    if m == 4:
        return pick(4)
    if m == 2:
        return jnp.where(sub < 4, pick(2), pick(6))
    return jnp.where(sub < 2, pick(1), jnp.where(sub < 4, pick(3), jnp.where(sub < 6, pick(5), pick(7))))


def _level_mask(t_idx, s_idx, m, rev):
    split = ((t_idx ^ s_idx) >> (m.bit_length() - 1)) == 1
    t_hi = (t_idx & m) != 0
    return split & (jnp.logical_not(t_hi) if rev else t_hi)


def _hg_gate(f_raw, lb):
    e = jnp.exp(-jnp.abs(f_raw))
    r = 1.0 / (1.0 + e)
    er = e * r
    pos = f_raw >= 0
    g = jnp.log(lb + (1.0 - lb) * jnp.where(pos, r, er))
    k = (1.0 - lb) * jnp.where(pos, er, r)
    return g, k


def _hg_chunk(q_raw, f_raw, v, lb, st, rev, readout):
    row = lax.broadcasted_iota(jnp.int32, (CHUNK, HEAD_DIM), 0)
    g, k = _hg_gate(f_raw, lb)
    b = _cumsum_rows(g, rev, row)
    btot = b[0:1, :] if rev else b[CHUNK - 1:CHUNK, :]
    k_dec = (k * jnp.exp(btot - b)).astype(BF16)
    vv_t = jnp.concatenate([v, v], axis=0).T
    st_new = st * jnp.exp(btot) + _dot(vv_t[:, :CHUNK].astype(BF16), k_dec)
    if not readout:
        return None, st_new

    q = _silu(q_raw) * HEAD_DIM ** -0.5
    col = lax.broadcasted_iota(jnp.int32, (CHUNK, HEAD_DIM), 1)
    s_idx = col % CHUNK
    scores = jnp.zeros((CHUNK, HEAD_DIM), F32)
    zero_blk = jnp.zeros((CHUNK, HEAD_DIM), BF16)
    for m_a, m_b in ((32, 16), (8, 4), (2, 1)):
        a_mats = []
        for m in (m_a, m_b):
            x = jnp.exp(-jnp.abs(b - _level_ref(b, m, row)))
            t_is_q = ((row & m) == 0) if rev else ((row & m) != 0)
            a_mats.append((jnp.where(t_is_q, q, k) * x).astype(BF16))
        lhs = jnp.concatenate(a_mats, axis=1)
        rhs = jnp.concatenate([jnp.concatenate([a_mats[0], zero_blk], axis=1),
                               jnp.concatenate([zero_blk, a_mats[1]], axis=1)], axis=0)
        gp = _dot_nt(lhs, rhs)
        keep = ((col < CHUNK) & _level_mask(row, s_idx, m_a, rev)) | \
               ((col >= CHUNK) & _level_mask(row, s_idx, m_b, rev))
        scores = scores + jnp.where(keep, gp, 0.0)
    q_dec = q * jnp.exp(b)
    lhs = jnp.concatenate([scores.astype(BF16), q_dec.astype(BF16)], axis=1)
    rhs = jnp.concatenate([vv_t.astype(BF16), st.astype(BF16)], axis=1)
    o = _dot_nt(lhs, rhs) + jnp.sum(q * k, axis=-1, keepdims=True) * v
    return o, st_new


def _hg_body(qf_ref, ff_ref, vf_ref, qb_ref, fb_ref, vb_ref,
             cq_ref, cff_ref, cfb_ref, cv_ref, lb_ref, of_ref, ob_ref, sf_scr, sb_scr,
             *, n_chunks, n_ctx_chunks):
    lb_f = lb_ref[0, 0:1, :]
    lb_b = lb_ref[0, 1:2, :]

    @pl.when(pl.program_id(2) == 0)
    def _():
        sf_scr[...] = jnp.zeros_like(sf_scr)
        sb_scr[...] = jnp.zeros_like(sb_scr)

        def ctx_step(c, carry):
            rf = pl.ds(pl.multiple_of(c * CHUNK, CHUNK), CHUNK)
            rb = pl.ds(pl.multiple_of((n_ctx_chunks - 1 - c) * CHUNK, CHUNK), CHUNK)
            _, sf = _hg_chunk(None, cff_ref[0, 0, rf, :], cv_ref[0, 0, rf, :], lb_f, sf_scr[...], False, False)
            _, sb = _hg_chunk(None, cfb_ref[0, 0, rb, :], cv_ref[0, 0, rb, :], lb_b, sb_scr[...], True, False)
            sf_scr[...] = sf
            sb_scr[...] = sb
            return carry

        lax.fori_loop(0, n_ctx_chunks, ctx_step, 0)

    def step(c, carry):
        rf = pl.ds(pl.multiple_of(c * CHUNK, CHUNK), CHUNK)
        rb = pl.ds(pl.multiple_of((n_chunks - 1 - c) * CHUNK, CHUNK), CHUNK)
        o_f, sf = _hg_chunk(qf_ref[0, 0, rf, :], ff_ref[0, 0, rf, :], vf_ref[0, 0, rf, :],
                            lb_f, sf_scr[...], False, True)
        o_b, sb = _hg_chunk(qb_ref[0, 0, rb, :], fb_ref[0, 0, rb, :], vb_ref[0, 0, rb, :],
                            lb_b, sb_scr[...], True, True)
        of_ref[0, 0, rf, :] = o_f
        ob_ref[0, 0, rb, :] = o_b
        sf_scr[...] = sf
        sb_scr[...] = sb
        return carry

    lax.fori_loop(0, n_chunks, step, 0)


def _hgrn2_scan(yh, yh_ctx, lb, tb):
    bsz, _, length, _ = yh.shape
    lc = yh_ctx.shape[2]
    nt = length // tb
    blk = (1, 1, tb, HEAD_DIM)
    cblk = (1, 1, lc, HEAD_DIM)
    fwd = lambda off: pl.BlockSpec(blk, lambda b, h, t: (b, off + h, t, 0))
    bwd = lambda off: pl.BlockSpec(blk, lambda b, h, t: (b, off + h, nt - 1 - t, 0))
    ctx = lambda off: pl.BlockSpec(cblk, lambda b, h, t: (b, off + h, 0, 0))
    out_shape = jax.ShapeDtypeStruct((bsz, N_HEADS, length, HEAD_DIM), F32)
    return pl.pallas_call(
        functools.partial(_hg_body, n_chunks=tb // CHUNK, n_ctx_chunks=lc // CHUNK),
        grid=(bsz, N_HEADS, nt),
        in_specs=[fwd(OFF_AQ), fwd(OFF_AFF), fwd(OFF_AI), bwd(OFF_AQ), bwd(OFF_AFB), bwd(OFF_AI),
                  ctx(OFF_AQ), ctx(OFF_AFF), ctx(OFF_AFB), ctx(OFF_AI),
                  pl.BlockSpec((1, 2, HEAD_DIM), lambda b, h, t: (h, 0, 0))],
        out_specs=[pl.BlockSpec(blk, lambda b, h, t: (b, h, t, 0)),
                   pl.BlockSpec(blk, lambda b, h, t: (b, h, nt - 1 - t, 0))],
        out_shape=[out_shape, out_shape],
        scratch_shapes=[pltpu.VMEM((HEAD_DIM, HEAD_DIM), F32), pltpu.VMEM((HEAD_DIM, HEAD_DIM), F32)],
        compiler_params=pltpu.CompilerParams(
            dimension_semantics=("parallel", "parallel", "arbitrary"),
            vmem_limit_bytes=VMEM_LIMIT),
    )(yh, yh, yh, yh, yh, yh, yh_ctx, yh_ctx, yh_ctx, yh_ctx, lb)


def _seg_cumsum_lanes(x, rev, lane):
    total = x.shape[1]
    seg = lane % CHUNK
    for sh in (1, 2, 4, 8, 16, 32):
        if rev:
            x = x + jnp.where(seg < CHUNK - sh, pltpu.roll(x, total - sh, 1), 0.0)
        else:
            x = x + jnp.where(seg >= sh, pltpu.roll(x, sh, 1), 0.0)
    return x


def _gdn_gates(g_ref, d, head, a_log, dt_bias):
    total = g_ref.shape[2]
    lane = lax.broadcasted_iota(jnp.int32, (1, total), 1)
    a_raw = g_ref[0, pl.ds(d * N_HEADS + head, 1), :]
    b_raw = g_ref[0, pl.ds(2 * N_HEADS + d * N_HEADS + head, 1), :]
    z = a_raw + dt_bias
    softplus = jnp.maximum(z, 0.0) + jnp.log1p(jnp.exp(-jnp.abs(z)))
    g = -jnp.exp(jnp.full((1, total), a_log, F32)) * softplus
    beta = _sigmoid(b_raw)
    rev = d == 1
    b = _seg_cumsum_lanes(g, rev, lane)
    tot = b + _seg_cumsum_lanes(g, not rev, lane) - g
    rows = jnp.concatenate([b, beta, tot, jnp.zeros((5, total), F32)], axis=0)
    pad = jnp.concatenate([rows, jnp.zeros((HEAD_DIM - 8, total), F32)], axis=0)
    return rows, pad.T


def _gdn_prepare(chunks, readout, qkg_scr, tb_scr, pm_scr, kdt_scr, dec_scr):
    t_i = lax.broadcasted_iota(jnp.int32, (CHUNK, CHUNK), 0)
    s_i = lax.broadcasted_iota(jnp.int32, (CHUNK, CHUNK), 1)
    eye = (t_i == s_i).astype(F32)
    tri = {False: (s_i <= t_i, s_i < t_i), True: (s_i >= t_i, s_i > t_i)}

    dmats, qks, xs, ys = [], [], [], []
    for _, load_q, load_k, rows, cols, rev in chunks:
        incl, strict = tri[rev]
        dmat = jnp.exp(jnp.where(incl, cols[:, 0:1] - rows[0:1, :], NEG_BIG))
        kb = load_k().astype(BF16)
        if readout:
            qkk = _dot_nt(jnp.concatenate([load_q().astype(BF16), kb], axis=0), kb)
            qks.append(qkk[:CHUNK])
            kk = qkk[CHUNK:]
        else:
            kk = _dot_nt(kb, kb)
        n_mat = jnp.where(strict, kk * dmat, 0.0) * cols[:, 1:2]
        dmats.append(dmat)
        xs.append(eye - n_mat)
        ys.append(n_mat.astype(BF16))
    ys = [_dot(nb, nb) for nb in ys]
    for _ in range(4):
        xy = [_dot(jnp.concatenate([x, y], axis=0).astype(BF16), y.astype(BF16)) for x, y in zip(xs, ys)]
        xs = [x + p[:CHUNK] for x, p in zip(xs, xy)]
        ys = [p[CHUNK:] for p in xy]
    xs = [x + _dot(x.astype(BF16), y.astype(BF16)) for x, y in zip(xs, ys)]

    for i, (slot, load_q, load_k, rows, cols, rev) in enumerate(chunks):
        b_row, beta_row, tot_row = rows[0:1, :], rows[1:2, :], rows[2:3, :]
        e_col = jnp.exp(cols[:, 0:1])
        k = load_k()
        tb_scr[slot] = (xs[i] * beta_row).astype(BF16)
        qkg_scr[slot, CHUNK:, :] = (k * e_col).astype(BF16)
        if readout:
            qkg_scr[slot, :CHUNK, :] = (load_q() * e_col).astype(BF16)
            pm_scr[slot] = (qks[i] * dmats[i]).astype(BF16)
        k_t = jnp.concatenate([k, k], axis=0).T[:, :CHUNK]
        kdt_scr[slot] = (k_t * jnp.exp(tot_row - b_row)).astype(BF16)
        dec_scr[slot] = jnp.exp(jnp.concatenate([tot_row, tot_row], axis=1))


def _gdn_recur(n, dirs, readout, qkg_scr, tb_scr, pm_scr, kdt_scr, dec_scr):
    def step(c, carry):
        slots = [slot_of(c) for _, slot_of, _, _ in dirs]
        states = [s_scr[...] for s_scr, _, _, _ in dirs]
        if readout:
            qks = [_dot(qkg_scr[slot], s.astype(BF16)) for slot, s in zip(slots, states)]
            kss = [p[CHUNK:] for p in qks]
        else:
            kss = [_dot(qkg_scr[slot, CHUNK:, :], s.astype(BF16)) for slot, s in zip(slots, states)]
        v_new = [_dot(tb_scr[slot], (load_v(c) - ks).astype(BF16)).astype(BF16)
                 for slot, ks, (_, _, load_v, _) in zip(slots, kss, dirs)]
        for slot, s, vn, (s_scr, _, _, _) in zip(slots, states, v_new, dirs):
            s_scr[...] = s * dec_scr[slot] + _dot(kdt_scr[slot], vn)
        if readout:
            for slot, p, vn, (_, _, _, store_o) in zip(slots, qks, v_new, dirs):
                store_o(c, p[:CHUNK] + _dot(pm_scr[slot], vn))
        return carry

    lax.fori_loop(0, n, step, 0)


def _gdn_body(par_ref, qf_ref, kf_ref, vf_ref, gf_ref, qb_ref, kb_ref, vb_ref, gb_ref,
              ck_ref, cv_ref, cg_ref, of_ref, ob_ref,
              sf_scr, sb_scr, qkg_scr, tb_scr, pm_scr, kdt_scr, dec_scr,
              *, n_chunks, n_ctx_chunks):
    head = pl.program_id(1)
    alog = (par_ref[head, 0], par_ref[head, 1])
    dtb = (par_ref[head, 2], par_ref[head, 3])
    scr = (qkg_scr, tb_scr, pm_scr, kdt_scr, dec_scr)

    def chunk_list(n, q_refs, k_refs, g_refs):
        out = []
        for d in (0, 1):
            rows, cols = _gdn_gates(g_refs[d], d, head, alog[d], dtb[d])
            for c in range(n):
                sl = slice(c * CHUNK, (c + 1) * CHUNK)
                load_q = None if q_refs is None else (lambda r=q_refs[d], sl=sl: r[0, 0, sl, :])
                load_k = lambda r=k_refs[d], sl=sl: r[0, 0, sl, :]
                out.append((d * n + c, load_q, load_k, rows[:, sl], cols[sl, :], d == 1))
        return out

    @pl.when(pl.program_id(2) == 0)
    def _():
        sf_scr[...] = jnp.zeros_like(sf_scr)
        sb_scr[...] = jnp.zeros_like(sb_scr)
        n = n_ctx_chunks
        _gdn_prepare(chunk_list(n, None, (ck_ref, ck_ref), (cg_ref, cg_ref)), False, *scr)
        load_v = lambda c: cv_ref[0, 0, pl.ds(pl.multiple_of(c * CHUNK, CHUNK), CHUNK), :]
        _gdn_recur(n, [(sf_scr, lambda c: c, load_v, None),
                       (sb_scr, lambda c: 2 * n - 1 - c, lambda c: load_v(n - 1 - c), None)], False, *scr)

    n = n_chunks
    _gdn_prepare(chunk_list(n, (qf_ref, qb_ref), (kf_ref, kb_ref), (gf_ref, gb_ref)), True, *scr)
    ds = lambda c: pl.ds(pl.multiple_of(c * CHUNK, CHUNK), CHUNK)

    def store_f(c, o):
        of_ref[0, 0, ds(c), :] = o

    def store_b(c, o):
        ob_ref[0, 0, ds(n - 1 - c), :] = o

    _gdn_recur(n, [(sf_scr, lambda c: c, lambda c: vf_ref[0, 0, ds(c), :], store_f),
                   (sb_scr, lambda c: 2 * n - 1 - c, lambda c: vb_ref[0, 0, ds(n - 1 - c), :], store_b)],
               True, *scr)


def _gdn_scan(qkv, gates, qkv_ctx, gates_ctx, params, tb):
    bsz, _, length, _ = qkv.shape
    lc = qkv_ctx.shape[2]
    nt = length // tb
    nc, ncc = tb // CHUNK, lc // CHUNK
    slots = 2 * max(nc, ncc)
    blk = (1, 1, tb, HEAD_DIM)
    cblk = (1, 1, lc, HEAD_DIM)
    fwd = lambda off: pl.BlockSpec(blk, lambda b, h, t: (b, off + h, t, 0))
    bwd = lambda off: pl.BlockSpec(blk, lambda b, h, t: (b, off + h, nt - 1 - t, 0))
    ctx = lambda off: pl.BlockSpec(cblk, lambda b, h, t: (b, off + h, 0, 0))
    out_shape = jax.ShapeDtypeStruct((bsz, N_HEADS, length, HEAD_DIM), F32)
    return pl.pallas_call(
        functools.partial(_gdn_body, n_chunks=nc, n_ctx_chunks=ncc),
        grid=(bsz, N_HEADS, nt),
        in_specs=[pl.BlockSpec(memory_space=pltpu.SMEM),
                  fwd(0), fwd(N_HEADS), fwd(2 * N_HEADS),
                  pl.BlockSpec((1, N_GATE, tb), lambda b, h, t: (b, 0, t)),
                  bwd(0), bwd(N_HEADS), bwd(2 * N_HEADS),
                  pl.BlockSpec((1, N_GATE, tb), lambda b, h, t: (b, 0, nt - 1 - t)),
                  ctx(N_HEADS), ctx(2 * N_HEADS),
                  pl.BlockSpec((1, N_GATE, lc), lambda b, h, t: (b, 0, 0))],
        out_specs=[pl.BlockSpec(blk, lambda b, h, t: (b, h, t, 0)),
                   pl.BlockSpec(blk, lambda b, h, t: (b, h, nt - 1 - t, 0))],
        out_shape=[out_shape, out_shape],
        scratch_shapes=[pltpu.VMEM((HEAD_DIM, HEAD_DIM), F32), pltpu.VMEM((HEAD_DIM, HEAD_DIM), F32),
                        pltpu.VMEM((slots, 2 * CHUNK, HEAD_DIM), BF16),
                        pltpu.VMEM((slots, CHUNK, CHUNK), BF16),
                        pltpu.VMEM((slots, CHUNK, CHUNK), BF16),
                        pltpu.VMEM((slots, HEAD_DIM, CHUNK), BF16),
                        pltpu.VMEM((slots, 1, HEAD_DIM), F32)],
        compiler_params=pltpu.CompilerParams(
            dimension_semantics=("parallel", "parallel", "arbitrary"),
            vmem_limit_bytes=VMEM_LIMIT),
    )(params, qkv, qkv, qkv, gates, qkv, qkv, qkv, gates, qkv_ctx, qkv_ctx, gates_ctx)


def _out_body(oaf_ref, oab_ref, obf_ref, obb_ref, za_ref, zb_ref, naw_ref, nbw_ref,
              w_ref, x_ref, gate_ref, fw_ref, o_ref):
    parts = []
    for of_ref, ob_ref, z_ref, nw_ref in ((oaf_ref, oab_ref, za_ref, naw_ref),
                                          (obf_ref, obb_ref, zb_ref, nbw_ref)):
        for h in range(N_HEADS):
            o = of_ref[0, h] + ob_ref[0, h]
            o = o * lax.rsqrt(jnp.mean(o * o, axis=-1, keepdims=True) + NORM_EPS) * nw_ref[h:h + 1, :]
            parts.append((_silu(z_ref[0, h]) * o).astype(BF16))
    y = jnp.concatenate(parts, axis=1)
    xo = x_ref[0] + gate_ref[0] * _dot(y, w_ref[...])
    ms = jnp.mean(xo * xo, axis=-1, keepdims=True)
    o_ref[0] = xo * lax.rsqrt(ms + NORM_EPS) * fw_ref[...]


def _out_stage(oa_f, oa_b, ob_f, ob_b, yh, na_w, nb_w, w_out, x, gate, final_w, tm):
    bsz, length, d = x.shape
    hblk = (1, N_HEADS, tm, HEAD_DIM)
    ospec = pl.BlockSpec(hblk, lambda b, m: (b, 0, m, 0))
    full2 = lambda a: pl.BlockSpec(a.shape, lambda b, m: (0, 0))
    return pl.pallas_call(
        _out_body,
        grid=(bsz, length // tm),
        in_specs=[ospec, ospec, ospec, ospec,
                  pl.BlockSpec(hblk, lambda b, m: (b, OFF_AZ // N_HEADS, m, 0)),
                  pl.BlockSpec(hblk, lambda b, m: (b, OFF_BZ // N_HEADS, m, 0)),
                  full2(na_w), full2(nb_w), full2(w_out),
                  pl.BlockSpec((1, tm, d), lambda b, m: (b, m, 0)),
                  pl.BlockSpec((1, 1, d), lambda b, m: (b, 0, 0)),
                  full2(final_w)],
        out_specs=pl.BlockSpec((1, tm, d), lambda b, m: (b, m, 0)),
        out_shape=jax.ShapeDtypeStruct((bsz, length, d), F32),
        compiler_params=pltpu.CompilerParams(dimension_semantics=("parallel", "parallel"),
                                             vmem_limit_bytes=VMEM_LIMIT),
    )(oa_f, oa_b, ob_f, ob_b, yh, yh, na_w, nb_w, w_out, x, gate, final_w)


def kernel(x, c, ctx, c_ctx, norm_w, ada_w, ada_b, w_in, conv_w, hg_lb_logits, gdn_a_log,
           gdn_dt_bias, ha_norm_w, hb_norm_w, w_out, final_norm_w):
    bsz, length, d = x.shape
    lc = ctx.shape[1]
    assert d == D_MODEL and length % 512 == 0 and length % GRID_W == 0 and lc % CHUNK == 0
    assert w_in.shape[0] == 1, "single-layer block"

    n_cond = bsz + 1
    cond = jnp.concatenate([c, c_ctx[None, :], jnp.zeros((-n_cond % 8, d), F32)], axis=0)
    mod = _adaln(cond, ada_w[0], ada_b[0])
    shift, scale, gate = mod[:, :d], mod[:, d:2 * d], mod[:, 2 * d:]
    lat = lambda m: m[:bsz, None, :]
    rep_ctx = lambda m: jnp.broadcast_to(m[bsz:bsz + 1, None, :], (bsz, 1, d))

    w_main = w_in[0, :, :N_MAIN].astype(BF16)
    w_gate_t = w_in[0, :, N_MAIN:].T.astype(BF16)
    nw = norm_w[0].reshape(1, d)
    yh, gates = _inproj(x, nw, lat(scale), lat(shift), w_main, w_gate_t, tm=min(1024, length))
    yh_c, gates_c = _inproj(ctx, nw, rep_ctx(scale), rep_ctx(shift), w_main, w_gate_t, tm=lc)

    lb = jax.nn.softmax(hg_lb_logits.astype(F32), axis=0)[0]
    lb = lb.reshape(2, N_HEADS, HEAD_DIM).transpose(1, 0, 2)
    tb = min(512, length)
    oa_f, oa_b = _hgrn2_scan(yh, yh_c, lb, tb)

    cw = conv_w[0].reshape(9, 3 * N_HEADS, HEAD_DIM).transpose(1, 0, 2)
    qkv = _gdn_conv(yh, cw, two_d=True)
    qkv_c = _gdn_conv(yh_c, cw, two_d=False)
    params = jnp.concatenate([gdn_a_log[0].T, gdn_dt_bias[0].T], axis=1).astype(F32)
    ob_f, ob_b = _gdn_scan(qkv, gates, qkv_c, gates_c, params, tb)

    return _out_stage(oa_f, oa_b, ob_f, ob_b, yh, ha_norm_w[0], hb_norm_w[0],
                      w_out[0].astype(BF16), x, lat(gate), final_norm_w.reshape(1, d),
                      tm=min(256, length))
```

```python
import functools

import jax
import jax.numpy as jnp
import numpy as np
from jax import lax
from jax.experimental import pallas as pl
from jax.experimental.pallas import tpu as pltpu

F32 = jnp.float32
BF16 = jnp.bfloat16

D_MODEL = 1024
N_HEADS = 8
HEAD_DIM = 128
CHUNK = 64
GRID_W = 64
NORM_EPS = 1e-6
N_MAIN = 72 * HEAD_DIM
N_GATE = 4 * N_HEADS
OFF_AQ, OFF_AFF, OFF_AFB, OFF_AI, OFF_AZ, OFF_BQ, OFF_BZ = 0, 8, 16, 24, 32, 40, 64
NEG_BIG = -1e30
VMEM_LIMIT = 56 * 1024 * 1024


def _dot(a, b):
    return jnp.dot(a, b, preferred_element_type=F32)


def _dot_nt(a, b):
    return lax.dot_general(a, b, (((1,), (1,)), ((), ())), preferred_element_type=F32)


def _sigmoid(x):
    return 1.0 / (1.0 + jnp.exp(-x))


def _silu(x):
    return x * _sigmoid(x)


def _adaln_body(c_ref, w_ref, b_ref, o_ref):
    o_ref[...] = _dot(_silu(c_ref[...]), w_ref[...]) + b_ref[...]


def _adaln(cond, ada_w, ada_b):
    rows, d = cond.shape
    n = ada_w.shape[1]
    tn = 1024
    return pl.pallas_call(
        _adaln_body,
        grid=(n // tn,),
        in_specs=[pl.BlockSpec((rows, d), lambda j: (0, 0)),
                  pl.BlockSpec((d, tn), lambda j: (0, j)),
                  pl.BlockSpec((1, tn), lambda j: (0, j))],
        out_specs=pl.BlockSpec((rows, tn), lambda j: (0, j)),
        out_shape=jax.ShapeDtypeStruct((rows, n), F32),
        compiler_params=pltpu.CompilerParams(dimension_semantics=("arbitrary",),
                                             vmem_limit_bytes=VMEM_LIMIT),
    )(cond, ada_w, ada_b.reshape(1, n))


def _inproj_body(x_ref, nw_ref, sc_ref, sh_ref, w_ref, wg_ref, y_ref, yg_ref, h_scr, *, tn):
    @pl.when(pl.program_id(2) == 0)
    def _():
        x = x_ref[0]
        ms = jnp.mean(x * x, axis=-1, keepdims=True)
        h = x * lax.rsqrt(ms + NORM_EPS) * nw_ref[...]
        h = (h * (1.0 + sc_ref[0]) + sh_ref[0]).astype(BF16)
        h_scr[...] = h
        yg_ref[0] = _dot_nt(wg_ref[...], h)

    acc = _dot(h_scr[...], w_ref[...])
    for j in range(tn // HEAD_DIM):
        y_ref[0, j] = acc[:, j * HEAD_DIM:(j + 1) * HEAD_DIM]


def _inproj(x, norm_w, scale, shift, w_main, w_gate_t, tm):
    bsz, length, d = x.shape
    tn = 1024
    grid = (bsz, length // tm, N_MAIN // tn)
    return pl.pallas_call(
        functools.partial(_inproj_body, tn=tn),
        grid=grid,
        in_specs=[pl.BlockSpec((1, tm, d), lambda b, m, n: (b, m, 0)),
                  pl.BlockSpec((1, d), lambda b, m, n: (0, 0)),
                  pl.BlockSpec((1, 1, d), lambda b, m, n: (b, 0, 0)),
                  pl.BlockSpec((1, 1, d), lambda b, m, n: (b, 0, 0)),
                  pl.BlockSpec((d, tn), lambda b, m, n: (0, n)),
                  pl.BlockSpec((N_GATE, d), lambda b, m, n: (0, 0))],
        out_specs=[pl.BlockSpec((1, tn // HEAD_DIM, tm, HEAD_DIM), lambda b, m, n: (b, n, m, 0)),
                   pl.BlockSpec((1, N_GATE, tm), lambda b, m, n: (b, 0, m))],
        out_shape=[jax.ShapeDtypeStruct((bsz, N_MAIN // HEAD_DIM, length, HEAD_DIM), F32),
                   jax.ShapeDtypeStruct((bsz, N_GATE, length), F32)],
        scratch_shapes=[pltpu.VMEM((tm, d), BF16)],
        compiler_params=pltpu.CompilerParams(
            dimension_semantics=("parallel", "parallel", "arbitrary"),
            vmem_limit_bytes=VMEM_LIMIT),
    )(x, norm_w, scale, shift, w_main, w_gate_t)


CONV_PAD = 72
CONV_HALO = 8


def _conv_body(x_ref, w_ref, o_ref, pad_scr, *, length, width, two_d, rt):
    blk = pl.program_id(1)
    zeros = jnp.zeros((CONV_PAD, HEAD_DIM), F32)
    for j in range(3):
        pad_scr[j, 0:CONV_PAD, :] = zeros
        pad_scr[j, CONV_PAD + length:CONV_PAD + length + CONV_PAD, :] = zeros
    pad_scr[0, CONV_PAD:CONV_PAD + length, :] = x_ref[0, 0]
    w = w_ref[0]
    win_rows = rt + 2 * CONV_HALO
    is_q = blk < N_HEADS
    is_qk = blk < 2 * N_HEADS

    def neighbours(i, carry):
        s = pl.multiple_of(i * rt, rt)
        col = (lax.broadcasted_iota(jnp.int32, (rt, HEAD_DIM), 0) + s) % width
        win = pad_scr[0, pl.ds(pl.multiple_of(s + (CONV_PAD - CONV_HALO), 8), win_rows), :]
        left = pltpu.roll(win, 1, 0)[CONV_HALO:CONV_HALO + rt]
        right = pltpu.roll(win, win_rows - 1, 0)[CONV_HALO:CONV_HALO + rt]
        dst = pl.ds(pl.multiple_of(s + CONV_PAD, 8), rt)
        pad_scr[1, dst, :] = jnp.where(col >= 1, left, 0.0)
        pad_scr[2, dst, :] = jnp.where(col <= width - 2, right, 0.0)
        return carry

    lax.fori_loop(0, length // rt, neighbours, 0)

    def tile(i, carry):
        s = pl.multiple_of(i * rt, rt)
        acc = jnp.zeros((rt, HEAD_DIM), F32)
        for dr in ((-1, 0, 1) if two_d else (0,)):
            src = pl.ds(pl.multiple_of(s + (CONV_PAD + dr * width), 8), rt)
            for j, dw in ((1, -1), (0, 0), (2, 1)):
                tap = (dr + 1) * 3 + (dw + 1)
                acc = acc + pad_scr[j, src, :] * w[tap:tap + 1, :]
        hs = 0.5 * acc
        a = hs + hs * jnp.tanh(hs)
        nrm = lax.rsqrt(jnp.sum(a * a, axis=-1, keepdims=True) + NORM_EPS)
        f = jnp.where(is_q, nrm * HEAD_DIM ** -0.5, jnp.where(is_qk, nrm, 1.0))
        o_ref[0, 0, pl.ds(s, rt), :] = a * f
        return carry

    lax.fori_loop(0, length // rt, tile, 0)


def _gdn_conv(yh, conv_w, two_d):
    bsz, _, length, _ = yh.shape
    width = GRID_W if two_d else length
    rt = min(512, length)
    nblk = 3 * N_HEADS
    return pl.pallas_call(
        functools.partial(_conv_body, length=length, width=width, two_d=two_d, rt=rt),
        grid=(bsz, nblk),
        in_specs=[pl.BlockSpec((1, 1, length, HEAD_DIM), lambda b, j: (b, OFF_BQ + j, 0, 0)),
                  pl.BlockSpec((1, 9, HEAD_DIM), lambda b, j: (j, 0, 0))],
        out_specs=pl.BlockSpec((1, 1, length, HEAD_DIM), lambda b, j: (b, j, 0, 0)),
        out_shape=jax.ShapeDtypeStruct((bsz, nblk, length, HEAD_DIM), F32),
        scratch_shapes=[pltpu.VMEM((3, length + 2 * CONV_PAD, HEAD_DIM), F32)],
        compiler_params=pltpu.CompilerParams(dimension_semantics=("parallel", "parallel"),
                                             vmem_limit_bytes=VMEM_LIMIT),
    )(yh, conv_w)


HG_LEVELS = (32, 16, 8, 4, 2, 1)
HG_PAIRS = ((0, 1), (2, 3), (4, 5))
HG_GROUP = 2
LOG2E = 1.4426950408889634


def _hg_constants():
    t = np.arange(CHUNK)[:, None]
    col = np.arange(HEAD_DIM)[None, :]
    s = col % CHUNK
    out = [np.broadcast_to(np.where((t & m) != 0, LOG2E, -LOG2E), (CHUNK, HEAD_DIM)) for m in HG_LEVELS]
    for rev in (False, True):
        for pa, pb in HG_PAIRS:
            keep = np.zeros((CHUNK, HEAD_DIM), bool)
            for half, lv in ((col < CHUNK, pa), (col >= CHUNK, pb)):
                m = HG_LEVELS[lv]
                split = ((t ^ s) >> (m.bit_length() - 1)) == 1
                t_hi = (t & m) != 0
                keep |= half & split & (~t_hi if rev else t_hi)
            out.append(keep)
    out.append((col == t) & (col < CHUNK))
    return jnp.asarray(np.stack([np.asarray(o, np.float32) for o in out]))


def _cumsum_rows(g, rev, row):
    x = g
    for sh in (1, 2, 4, 8, 16, 32):
        if rev:
            x = x + jnp.where(row < CHUNK - sh, pltpu.roll(x, CHUNK - sh, 0), 0.0)
        else:
            x = x + jnp.where(row >= sh, pltpu.roll(x, sh, 0), 0.0)
    return x


def _level_ref(b_ref, m, sub):
    bc = lambda r, n: jnp.broadcast_to(b_ref[r:r + 1, :], (n, HEAD_DIM))
    if m >= 4:
        return jnp.concatenate([bc(blk * 2 * m + m, 2 * m) for blk in range(CHUNK // (2 * m))], axis=0)
    lo = jnp.concatenate([bc(v * 8 + 2, 8) for v in range(CHUNK // 8)], axis=0)
    hi = jnp.concatenate([bc(v * 8 + 6, 8) for v in range(CHUNK // 8)], axis=0)
    return jnp.where(sub < 4, lo, hi)


def _hg_prepare(items, readout, cst_ref, b_scr, sc_scr, vv_scr, u_scr, dec_scr):
    row = lax.broadcasted_iota(jnp.int32, (CHUNK, HEAD_DIM), 0)
    sub = row % 8
    even = (row & 1) == 0
    gs, ks, bs = [], [], []
    for i, (_, _, load_f, _, lb, rev) in enumerate(items):
        c1 = 0.5 - 0.5 * lb
        p = c1 * jnp.tanh(0.5 * load_f())
        g = jnp.log((0.5 + 0.5 * lb) + p)
        gs.append(g)
        ks.append(c1 - p)
        b = _cumsum_rows(g, rev, row)
        bs.append(b)
        b_scr[i] = b
    for i, (slot, _, _, load_v, _, rev) in enumerate(items):
        b = bs[i]
        btot = b[0:1, :] if rev else b[CHUNK - 1:CHUNK, :]
        v = load_v()
        vv_t = jnp.concatenate([v, v], axis=0).T.astype(BF16)
        k_dec = (ks[i] * jnp.exp(btot - b)).astype(BF16)
        u_scr[slot] = _dot(vv_t[:, :CHUNK], k_dec)
        dec_scr[slot] = jnp.exp(btot)
        vv_scr[slot] = vv_t
    if not readout:
        return

    qs = []
    for _, load_q, _, _, _, _ in items:
        q_raw = load_q()
        hs = (0.5 * HEAD_DIM ** -0.5) * q_raw
        qs.append(hs + hs * jnp.tanh(0.5 * q_raw))
    scores = [jnp.zeros((CHUNK, HEAD_DIM), F32) for _ in items]
    zero_blk = jnp.zeros((CHUNK, HEAD_DIM), BF16)
    for pi, pair in enumerate(HG_PAIRS):
        gps = []
        for i, (_, _, _, _, _, rev) in enumerate(items):
            a_mats = []
            for lv in pair:
                m = HG_LEVELS[lv]
                sign = cst_ref[lv]
                if m == 1:
                    x = jnp.exp(jnp.where(even, gs[i] if rev else pltpu.roll(gs[i], CHUNK - 1, 0), 0.0))
                else:
                    d = bs[i] - _level_ref(b_scr.at[i], m, sub)
                    x = jnp.exp2((-d if rev else d) * sign)
                t_is_q = (sign < 0.0) if rev else (sign > 0.0)
                a_mats.append((jnp.where(t_is_q, qs[i], ks[i]) * x).astype(BF16))
            lhs = jnp.concatenate(a_mats, axis=1)
            rhs = jnp.concatenate([jnp.concatenate([a_mats[0], zero_blk], axis=1),
                                   jnp.concatenate([zero_blk, a_mats[1]], axis=1)], axis=0)
            gps.append(_dot_nt(lhs, rhs))
        for i, (_, _, _, _, _, rev) in enumerate(items):
            scores[i] = scores[i] + gps[i] * cst_ref[6 + (3 if rev else 0) + pi]
    for i, (slot, _, _, _, _, _) in enumerate(items):
        diag = jnp.sum(qs[i] * ks[i], axis=-1, keepdims=True) * cst_ref[12]
        sc_scr[slot, :, :HEAD_DIM] = (scores[i] + diag).astype(BF16)
        sc_scr[slot, :, HEAD_DIM:] = (qs[i] * jnp.exp(bs[i])).astype(BF16)


def _hg_recur(n, dirs, readout, sc_scr, vv_scr, u_scr, dec_scr):
    states = [s_scr[...] for s_scr, _, _ in dirs]
    for c in range(n):
        for i, (_, slot_of, store_o) in enumerate(dirs):
            slot = slot_of(c)
            if readout:
                rhs = jnp.concatenate([vv_scr[slot], states[i].astype(BF16)], axis=1)
                store_o(c, _dot_nt(sc_scr[slot], rhs))
            states[i] = states[i] * dec_scr[slot] + u_scr[slot]
    for (s_scr, _, _), st in zip(dirs, states):
        s_scr[...] = st


def _hg_body(cst_ref, qf_ref, ff_ref, vf_ref, qb_ref, fb_ref, vb_ref,
             cff_ref, cfb_ref, cv_ref, lb_ref, of_ref, ob_ref,
             sf_scr, sb_scr, b_scr, sc_scr, vv_scr, u_scr, dec_scr, *, n_chunks, n_ctx_chunks):
    lb_f = lb_ref[0, 0:1, :]
    lb_b = lb_ref[0, 1:2, :]
    scr = (sc_scr, vv_scr, u_scr, dec_scr)

    @pl.when(pl.program_id(2) == 0)
    def _():
        sf_scr[...] = jnp.zeros_like(sf_scr)
        sb_scr[...] = jnp.zeros_like(sb_scr)
        n = n_ctx_chunks
        for c0 in range(0, n, HG_GROUP):
            items = []
            for c in range(c0, min(c0 + HG_GROUP, n)):
                sl = slice(c * CHUNK, (c + 1) * CHUNK)
                items.append((c, None, lambda sl=sl: cff_ref[0, 0, sl, :], lambda sl=sl: cv_ref[0, 0, sl, :], lb_f, False))
                items.append((n + c, None, lambda sl=sl: cfb_ref[0, 0, sl, :], lambda sl=sl: cv_ref[0, 0, sl, :], lb_b, True))
            _hg_prepare(items, False, cst_ref, b_scr, *scr)
        _hg_recur(n, [(sf_scr, lambda c: c, None), (sb_scr, lambda c: 2 * n - 1 - c, None)], False, *scr)

    n = n_chunks

    def group(gi, carry):
        items = []
        for j in range(HG_GROUP):
            c = gi * HG_GROUP + j
            r = pl.ds(pl.multiple_of(c * CHUNK, CHUNK), CHUNK)
            items.append((c, lambda r=r: qf_ref[0, 0, r, :], lambda r=r: ff_ref[0, 0, r, :],
                          lambda r=r: vf_ref[0, 0, r, :], lb_f, False))
            items.append((n + c, lambda r=r: qb_ref[0, 0, r, :], lambda r=r: fb_ref[0, 0, r, :],
                          lambda r=r: vb_ref[0, 0, r, :], lb_b, True))
        _hg_prepare(items, True, cst_ref, b_scr, *scr)
        return carry

    lax.fori_loop(0, n // HG_GROUP, group, 0)

    def store_f(c, o):
        of_ref[0, 0, c * CHUNK:(c + 1) * CHUNK, :] = o

    def store_b(c, o):
        ob_ref[0, 0, (n - 1 - c) * CHUNK:(n - c) * CHUNK, :] = o

    _hg_recur(n, [(sf_scr, lambda c: c, store_f), (sb_scr, lambda c: 2 * n - 1 - c, store_b)], True, *scr)


def _hgrn2_scan(yh, yh_ctx, lb, tb):
    bsz, _, length, _ = yh.shape
    lc = yh_ctx.shape[2]
    nt = length // tb
    nc, ncc = tb // CHUNK, lc // CHUNK
    assert nc % HG_GROUP == 0
    slots = 2 * max(nc, ncc)
    blk = (1, 1, tb, HEAD_DIM)
    cblk = (1, 1, lc, HEAD_DIM)
    fwd = lambda off: pl.BlockSpec(blk, lambda b, h, t: (b, off + h, t, 0))
    bwd = lambda off: pl.BlockSpec(blk, lambda b, h, t: (b, off + h, nt - 1 - t, 0))
    ctx = lambda off: pl.BlockSpec(cblk, lambda b, h, t: (b, off + h, 0, 0))
    consts = _hg_constants()
    out_shape = jax.ShapeDtypeStruct((bsz, N_HEADS, length, HEAD_DIM), F32)
    return pl.pallas_call(
        functools.partial(_hg_body, n_chunks=nc, n_ctx_chunks=ncc),
        grid=(bsz, N_HEADS, nt),
        in_specs=[pl.BlockSpec(consts.shape, lambda b, h, t: (0, 0, 0)),
                  fwd(OFF_AQ), fwd(OFF_AFF), fwd(OFF_AI), bwd(OFF_AQ), bwd(OFF_AFB), bwd(OFF_AI),
                  ctx(OFF_AFF), ctx(OFF_AFB), ctx(OFF_AI),
                  pl.BlockSpec((1, 2, HEAD_DIM), lambda b, h, t: (h, 0, 0))],
        out_specs=[pl.BlockSpec(blk, lambda b, h, t: (b, h, t, 0)),
                   pl.BlockSpec(blk, lambda b, h, t: (b, h, nt - 1 - t, 0))],
        out_shape=[out_shape, out_shape],
        scratch_shapes=[pltpu.VMEM((HEAD_DIM, HEAD_DIM), F32), pltpu.VMEM((HEAD_DIM, HEAD_DIM), F32),
                        pltpu.VMEM((2 * HG_GROUP, CHUNK, HEAD_DIM), F32),
                        pltpu.VMEM((slots, CHUNK, 2 * HEAD_DIM), BF16),
                        pltpu.VMEM((slots, HEAD_DIM, HEAD_DIM), BF16),
                        pltpu.VMEM((slots, HEAD_DIM, HEAD_DIM), F32),
                        pltpu.VMEM((slots, 1, HEAD_DIM), F32)],
        compiler_params=pltpu.CompilerParams(
            dimension_semantics=("parallel", "parallel", "arbitrary"),
            vmem_limit_bytes=VMEM_LIMIT),
    )(consts, yh, yh, yh, yh, yh, yh, yh_ctx, yh_ctx, yh_ctx, lb)


def _seg_cumsum_lanes(x, rev, lane):
    total = x.shape[1]
    seg = lane % CHUNK
    for sh in (1, 2, 4, 8, 16, 32):
        if rev:
            x = x + jnp.where(seg < CHUNK - sh, pltpu.roll(x, total - sh, 1), 0.0)
        else:
            x = x + jnp.where(seg >= sh, pltpu.roll(x, sh, 1), 0.0)
    return x


def _gdn_gates(g_ref, d, head, a_log, dt_bias):
    total = g_ref.shape[2]
    lane = lax.broadcasted_iota(jnp.int32, (1, total), 1)
    a_raw = g_ref[0, pl.ds(d * N_HEADS + head, 1), :]
    b_raw = g_ref[0, pl.ds(2 * N_HEADS + d * N_HEADS + head, 1), :]
    z = a_raw + dt_bias
    softplus = jnp.maximum(z, 0.0) + jnp.log1p(jnp.exp(-jnp.abs(z)))
    g = -jnp.exp(jnp.full((1, total), a_log, F32)) * softplus
    beta = _sigmoid(b_raw)
    rev = d == 1
    b = _seg_cumsum_lanes(g, rev, lane)
    tot = b + _seg_cumsum_lanes(g, not rev, lane) - g
    rows = jnp.concatenate([b, beta, tot, jnp.zeros((5, total), F32)], axis=0)
    return rows, jnp.broadcast_to(b, (HEAD_DIM, total)).T


def _gdn_prepare(chunks, readout, qkg_scr, tb_scr, pm_scr, kdt_scr, dec_scr):
    t_i = lax.broadcasted_iota(jnp.int32, (CHUNK, CHUNK), 0)
    s_i = lax.broadcasted_iota(jnp.int32, (CHUNK, CHUNK), 1)
    eye = (t_i == s_i).astype(F32)
    tri = {False: (s_i <= t_i, s_i < t_i), True: (s_i >= t_i, s_i > t_i)}

    dmats, qks, xs, ys = [], [], [], []
    for _, load_q, load_k, rows, bcol, rev in chunks:
        incl, strict = tri[rev]
        dmat = jnp.exp(jnp.where(incl, bcol[:, :CHUNK] - rows[0:1, :], NEG_BIG))
        kb = load_k().astype(BF16)
        if readout:
            qkk = _dot_nt(jnp.concatenate([load_q().astype(BF16), kb], axis=0), kb)
            qks.append(qkk[:CHUNK])
            kk = qkk[CHUNK:]
        else:
            kk = _dot_nt(kb, kb)
        n_mat = jnp.where(strict, kk * dmat, 0.0) * rows[1:2, :]
        dmats.append(dmat)
        xs.append(eye - n_mat)
        ys.append(n_mat.astype(BF16))
    ys = [_dot(nb, nb) for nb in ys]
    for _ in range(4):
        xy = [_dot(jnp.concatenate([x, y], axis=0).astype(BF16), y.astype(BF16)) for x, y in zip(xs, ys)]
        xs = [x + p[:CHUNK] for x, p in zip(xs, xy)]
        ys = [p[CHUNK:] for p in xy]
    xs = [x + _dot(x.astype(BF16), y.astype(BF16)) for x, y in zip(xs, ys)]

    for i, (slot, load_q, load_k, rows, bcol, rev) in enumerate(chunks):
        b_row, beta_row, tot_row = rows[0:1, :], rows[1:2, :], rows[2:3, :]
        e_col = jnp.exp(bcol)
        k = load_k()
        tb_scr[slot] = xs[i].astype(BF16)
        qkg_scr[slot, CHUNK:, :] = (k * e_col).astype(BF16)
        if readout:
            qkg_scr[slot, :CHUNK, :] = (load_q() * e_col).astype(BF16)
            pm_scr[slot] = (qks[i] * dmats[i] * beta_row).astype(BF16)
        k_t = jnp.concatenate([k, k], axis=0).T[:, :CHUNK]
        kdt_scr[slot] = (k_t * (jnp.exp(tot_row - b_row) * beta_row)).astype(BF16)
        dec_scr[slot] = jnp.exp(jnp.concatenate([tot_row, tot_row], axis=1))


def _gdn_recur(n, dirs, readout, qkg_scr, tb_scr, pm_scr, kdt_scr, dec_scr):
    def step(c, carry):
        slots = [slot_of(c) for _, slot_of, _, _ in dirs]
        states = [s_ref[...] for s_ref, _, _, _ in dirs]
        if readout:
            qks = [_dot(qkg_scr[slot], s.astype(BF16)) for slot, s in zip(slots, states)]
            kss = [p[CHUNK:] for p in qks]
        else:
            kss = [_dot(qkg_scr[slot, CHUNK:, :], s.astype(BF16)) for slot, s in zip(slots, states)]
        ws = [_dot(tb_scr[slot], (load_v(c) - ks).astype(BF16)).astype(BF16)
              for slot, ks, (_, _, load_v, _) in zip(slots, kss, dirs)]
        for slot, s, w, (s_ref, _, _, _) in zip(slots, states, ws, dirs):
            s_ref[...] = s * dec_scr[slot] + _dot(kdt_scr[slot], w)
        if readout:
            for slot, p, w, (_, _, _, store_o) in zip(slots, qks, ws, dirs):
                store_o(c, p[:CHUNK] + _dot(pm_scr[slot], w))
        return carry

    lax.fori_loop(0, n, step, 0)


def _gdn_body(par_ref, qf_ref, kf_ref, vf_ref, gf_ref, qb_ref, kb_ref, vb_ref, gb_ref,
              ck_ref, cv_ref, cg_ref, of_ref, ob_ref,
              s_scr, qkg_scr, tb_scr, pm_scr, kdt_scr, dec_scr,
              *, n_chunks, n_ctx_chunks, heads):
    h0 = pl.program_id(1) * heads
    scr = (qkg_scr, tb_scr, pm_scr, kdt_scr, dec_scr)
    ds = lambda c: pl.ds(pl.multiple_of(c * CHUNK, CHUNK), CHUNK)

    def chunk_list(hh, n, q_refs, k_refs, g_refs):
        head = h0 + hh
        out = []
        for d in (0, 1):
            rows, bcol = _gdn_gates(g_refs[d], d, head, par_ref[head, d], par_ref[head, 2 + d])
            for c in range(n):
                sl = slice(c * CHUNK, (c + 1) * CHUNK)
                load_q = None if q_refs is None else (lambda r=q_refs[d], sl=sl: r[0, hh, sl, :])
                load_k = lambda r=k_refs[d], sl=sl: r[0, hh, sl, :]
                out.append(((2 * hh + d) * n + c, load_q, load_k, rows[:, sl], bcol[sl, :], d == 1))
        return out

    def chains(n, vf, vb, of, ob):
        out = []
        for hh in range(heads):
            out.append((s_scr.at[2 * hh], lambda c, hh=hh: 2 * hh * n + c,
                        lambda c, hh=hh: vf[0, hh, ds(c), :],
                        None if of is None else (lambda c, o, hh=hh: of.__setitem__((0, hh, ds(c), slice(None)), o))))
            out.append((s_scr.at[2 * hh + 1], lambda c, hh=hh: (2 * hh + 1) * n + (n - 1 - c),
                        lambda c, hh=hh: vb[0, hh, ds(n - 1 - c), :],
                        None if ob is None else (lambda c, o, hh=hh: ob.__setitem__((0, hh, ds(n - 1 - c), slice(None)), o))))
        return out

    @pl.when(pl.program_id(2) == 0)
    def _():
        s_scr[...] = jnp.zeros_like(s_scr)
        n = n_ctx_chunks

        def ctx_head(hh, carry):
            _gdn_prepare(chunk_list(hh, n, None, (ck_ref, ck_ref), (cg_ref, cg_ref)), False, *scr)
            return carry

        lax.fori_loop(0, heads, ctx_head, 0)
        _gdn_recur(n, chains(n, cv_ref, cv_ref, None, None), False, *scr)

    n = n_chunks

    def main_head(hh, carry):
        _gdn_prepare(chunk_list(hh, n, (qf_ref, qb_ref), (kf_ref, kb_ref), (gf_ref, gb_ref)), True, *scr)
        return carry

    lax.fori_loop(0, heads, main_head, 0)
    _gdn_recur(n, chains(n, vf_ref, vb_ref, of_ref, ob_ref), True, *scr)


GDN_HEADS = 4


def _gdn_scan(qkv, gates, qkv_ctx, gates_ctx, params, tb):
    bsz, _, length, _ = qkv.shape
    lc = qkv_ctx.shape[2]
    nt = length // tb
    nc, ncc = tb // CHUNK, lc // CHUNK
    hb = GDN_HEADS
    slots = 2 * hb * max(nc, ncc)
    blk = (1, hb, tb, HEAD_DIM)
    cblk = (1, hb, lc, HEAD_DIM)
    fwd = lambda off: pl.BlockSpec(blk, lambda b, h, t: (b, off // hb + h, t, 0))
    bwd = lambda off: pl.BlockSpec(blk, lambda b, h, t: (b, off // hb + h, nt - 1 - t, 0))
    ctx = lambda off: pl.BlockSpec(cblk, lambda b, h, t: (b, off // hb + h, 0, 0))
    out_shape = jax.ShapeDtypeStruct((bsz, N_HEADS, length, HEAD_DIM), F32)
    return pl.pallas_call(
        functools.partial(_gdn_body, n_chunks=nc, n_ctx_chunks=ncc, heads=hb),
        grid=(bsz, N_HEADS // hb, nt),
        in_specs=[pl.BlockSpec(memory_space=pltpu.SMEM),
                  fwd(0), fwd(N_HEADS), fwd(2 * N_HEADS),
                  pl.BlockSpec((1, N_GATE, tb), lambda b, h, t: (b, 0, t)),
                  bwd(0), bwd(N_HEADS), bwd(2 * N_HEADS),
                  pl.BlockSpec((1, N_GATE, tb), lambda b, h, t: (b, 0, nt - 1 - t)),
                  ctx(N_HEADS), ctx(2 * N_HEADS),
                  pl.BlockSpec((1, N_GATE, lc), lambda b, h, t: (b, 0, 0))],
        out_specs=[pl.BlockSpec(blk, lambda b, h, t: (b, h, t, 0)),
                   pl.BlockSpec(blk, lambda b, h, t: (b, h, nt - 1 - t, 0))],
        out_shape=[out_shape, out_shape],
        scratch_shapes=[pltpu.VMEM((2 * hb, HEAD_DIM, HEAD_DIM), F32),
                        pltpu.VMEM((slots, 2 * CHUNK, HEAD_DIM), BF16),
                        pltpu.VMEM((slots, CHUNK, CHUNK), BF16),
                        pltpu.VMEM((slots, CHUNK, CHUNK), BF16),
                        pltpu.VMEM((slots, HEAD_DIM, CHUNK), BF16),
                        pltpu.VMEM((slots, 1, HEAD_DIM), F32)],
        compiler_params=pltpu.CompilerParams(
            dimension_semantics=("parallel", "parallel", "arbitrary"),
            vmem_limit_bytes=VMEM_LIMIT),
    )(params, qkv, qkv, qkv, gates, qkv, qkv, qkv, gates, qkv_ctx, qkv_ctx, gates_ctx)


def _out_body(oaf_ref, oab_ref, obf_ref, obb_ref, za_ref, zb_ref, naw_ref, nbw_ref,
              w_ref, x_ref, gate_ref, fw_ref, o_ref):
    parts = []
    for of_ref, ob_ref, z_ref, nw_ref in ((oaf_ref, oab_ref, za_ref, naw_ref),
                                          (obf_ref, obb_ref, zb_ref, nbw_ref)):
        for h in range(N_HEADS):
            o = of_ref[0, h] + ob_ref[0, h]
            o = o * lax.rsqrt(jnp.mean(o * o, axis=-1, keepdims=True) + NORM_EPS) * nw_ref[h:h + 1, :]
            parts.append((_silu(z_ref[0, h]) * o).astype(BF16))
    y = jnp.concatenate(parts, axis=1)
    xo = x_ref[0] + gate_ref[0] * _dot(y, w_ref[...])
    ms = jnp.mean(xo * xo, axis=-1, keepdims=True)
    o_ref[0] = xo * lax.rsqrt(ms + NORM_EPS) * fw_ref[...]


def _out_stage(oa_f, oa_b, ob_f, ob_b, yh, na_w, nb_w, w_out, x, gate, final_w, tm):
    bsz, length, d = x.shape
    hblk = (1, N_HEADS, tm, HEAD_DIM)
    ospec = pl.BlockSpec(hblk, lambda b, m: (b, 0, m, 0))
    full2 = lambda a: pl.BlockSpec(a.shape, lambda b, m: (0, 0))
    return pl.pallas_call(
        _out_body,
        grid=(bsz, length // tm),
        in_specs=[ospec, ospec, ospec, ospec,
                  pl.BlockSpec(hblk, lambda b, m: (b, OFF_AZ // N_HEADS, m, 0)),
                  pl.BlockSpec(hblk, lambda b, m: (b, OFF_BZ // N_HEADS, m, 0)),
                  full2(na_w), full2(nb_w), full2(w_out),
                  pl.BlockSpec((1, tm, d), lambda b, m: (b, m, 0)),
                  pl.BlockSpec((1, 1, d), lambda b, m: (b, 0, 0)),
                  full2(final_w)],
        out_specs=pl.BlockSpec((1, tm, d), lambda b, m: (b, m, 0)),
        out_shape=jax.ShapeDtypeStruct((bsz, length, d), F32),
        compiler_params=pltpu.CompilerParams(dimension_semantics=("parallel", "parallel"),
                                             vmem_limit_bytes=VMEM_LIMIT),
    )(oa_f, oa_b, ob_f, ob_b, yh, yh, na_w, nb_w, w_out, x, gate, final_w)


def kernel(x, c, ctx, c_ctx, norm_w, ada_w, ada_b, w_in, conv_w, hg_lb_logits, gdn_a_log,
           gdn_dt_bias, ha_norm_w, hb_norm_w, w_out, final_norm_w):
    bsz, length, d = x.shape
    lc = ctx.shape[1]
    assert d == D_MODEL and length % 512 == 0 and length % GRID_W == 0 and lc % CHUNK == 0
    assert w_in.shape[0] == 1, "single-layer block"

    n_cond = bsz + 1
    cond = jnp.concatenate([c, c_ctx[None, :], jnp.zeros((-n_cond % 8, d), F32)], axis=0)
    mod = _adaln(cond, ada_w[0], ada_b[0])
    shift, scale, gate = mod[:, :d], mod[:, d:2 * d], mod[:, 2 * d:]
    lat = lambda m: m[:bsz, None, :]
    rep_ctx = lambda m: jnp.broadcast_to(m[bsz:bsz + 1, None, :], (bsz, 1, d))

    w_main = w_in[0, :, :N_MAIN].astype(BF16)
    w_gate_t = w_in[0, :, N_MAIN:].T.astype(BF16)
    nw = norm_w[0].reshape(1, d)
    yh, gates = _inproj(x, nw, lat(scale), lat(shift), w_main, w_gate_t, tm=min(1024, length))
    yh_c, gates_c = _inproj(ctx, nw, rep_ctx(scale), rep_ctx(shift), w_main, w_gate_t, tm=lc)

    lb = jax.nn.softmax(hg_lb_logits.astype(F32), axis=0)[0]
    lb = lb.reshape(2, N_HEADS, HEAD_DIM).transpose(1, 0, 2)
    tb = min(512, length)
    oa_f, oa_b = _hgrn2_scan(yh, yh_c, lb, tb)

    cw = conv_w[0].reshape(9, 3 * N_HEADS, HEAD_DIM).transpose(1, 0, 2)
    qkv = _gdn_conv(yh, cw, two_d=True)
    qkv_c = _gdn_conv(yh_c, cw, two_d=False)
    params = jnp.concatenate([gdn_a_log[0].T, gdn_dt_bias[0].T], axis=1).astype(F32)
    ob_f, ob_b = _gdn_scan(qkv, gates, qkv_c, gates_c, params, tb)

    return _out_stage(oa_f, oa_b, ob_f, ob_b, yh, ha_norm_w[0], hb_norm_w[0],
                      w_out[0].astype(BF16), x, lat(gate), final_norm_w.reshape(1, d),
                      tm=min(256, length))
```

```python
import functools

import jax
import jax.numpy as jnp
import numpy as np
from jax import lax
from jax.experimental import pallas as pl
from jax.experimental.pallas import tpu as pltpu

F32 = jnp.float32
BF16 = jnp.bfloat16

D_MODEL = 1024
N_HEADS = 8
HEAD_DIM = 128
CHUNK = 64
GRID_W = 64
NORM_EPS = 1e-6
N_MAIN = 72 * HEAD_DIM
N_GATE = 4 * N_HEADS
OFF_AQ, OFF_AFF, OFF_AFB, OFF_AI, OFF_AZ, OFF_BQ, OFF_BZ = 0, 8, 16, 24, 32, 40, 64
NEG_BIG = -1e30
VMEM_LIMIT = 56 * 1024 * 1024


def _dot(a, b):
    return jnp.dot(a, b, preferred_element_type=F32)


def _dot_nt(a, b):
    return lax.dot_general(a, b, (((1,), (1,)), ((), ())), preferred_element_type=F32)


def _sigmoid(x):
    return 1.0 / (1.0 + jnp.exp(-x))


def _silu(x):
    return x * _sigmoid(x)


def _adaln_body(c_ref, w_ref, b_ref, o_ref):
    o_ref[...] = _dot(_silu(c_ref[...]), w_ref[...]) + b_ref[...]


def _adaln(cond, ada_w, ada_b):
    rows, d = cond.shape
    n = ada_w.shape[1]
    tn = 1024
    return pl.pallas_call(
        _adaln_body,
        grid=(n // tn,),
        in_specs=[pl.BlockSpec((rows, d), lambda j: (0, 0)),
                  pl.BlockSpec((d, tn), lambda j: (0, j)),
                  pl.BlockSpec((1, tn), lambda j: (0, j))],
        out_specs=pl.BlockSpec((rows, tn), lambda j: (0, j)),
        out_shape=jax.ShapeDtypeStruct((rows, n), F32),
        compiler_params=pltpu.CompilerParams(dimension_semantics=("arbitrary",),
                                             vmem_limit_bytes=VMEM_LIMIT),
    )(cond, ada_w, ada_b.reshape(1, n))


def _inproj_body(x_ref, nw_ref, sc_ref, sh_ref, w_ref, wg_ref, y_ref, yg_ref, h_scr, *, tn):
    @pl.when(pl.program_id(2) == 0)
    def _():
        x = x_ref[0]
        ms = jnp.mean(x * x, axis=-1, keepdims=True)
        h = x * lax.rsqrt(ms + NORM_EPS) * nw_ref[...]
        h = (h * (1.0 + sc_ref[0]) + sh_ref[0]).astype(BF16)
        h_scr[...] = h
        yg_ref[0] = _dot_nt(wg_ref[...], h)

    acc = _dot(h_scr[...], w_ref[...])
    for j in range(tn // HEAD_DIM):
        y_ref[0, j] = acc[:, j * HEAD_DIM:(j + 1) * HEAD_DIM]


def _inproj(x, norm_w, scale, shift, w_main, w_gate_t, tm):
    bsz, length, d = x.shape
    tn = 1024
    grid = (bsz, length // tm, N_MAIN // tn)
    return pl.pallas_call(
        functools.partial(_inproj_body, tn=tn),
        grid=grid,
        in_specs=[pl.BlockSpec((1, tm, d), lambda b, m, n: (b, m, 0)),
                  pl.BlockSpec((1, d), lambda b, m, n: (0, 0)),
                  pl.BlockSpec((1, 1, d), lambda b, m, n: (b, 0, 0)),
                  pl.BlockSpec((1, 1, d), lambda b, m, n: (b, 0, 0)),
                  pl.BlockSpec((d, tn), lambda b, m, n: (0, n)),
                  pl.BlockSpec((N_GATE, d), lambda b, m, n: (0, 0))],
        out_specs=[pl.BlockSpec((1, tn // HEAD_DIM, tm, HEAD_DIM), lambda b, m, n: (b, n, m, 0)),
                   pl.BlockSpec((1, N_GATE, tm), lambda b, m, n: (b, 0, m))],
        out_shape=[jax.ShapeDtypeStruct((bsz, N_MAIN // HEAD_DIM, length, HEAD_DIM), F32),
                   jax.ShapeDtypeStruct((bsz, N_GATE, length), F32)],
        scratch_shapes=[pltpu.VMEM((tm, d), BF16)],
        compiler_params=pltpu.CompilerParams(
            dimension_semantics=("parallel", "parallel", "arbitrary"),
            vmem_limit_bytes=VMEM_LIMIT),
    )(x, norm_w, scale, shift, w_main, w_gate_t)


CONV_PAD = 72
CONV_HALO = 8


def _conv_body(x_ref, w_ref, o_ref, pad_scr, *, length, width, two_d, rt):
    blk = pl.program_id(1)
    zeros = jnp.zeros((CONV_PAD, HEAD_DIM), F32)
    for j in range(3):
        pad_scr[j, 0:CONV_PAD, :] = zeros
        pad_scr[j, CONV_PAD + length:CONV_PAD + length + CONV_PAD, :] = zeros
    pad_scr[0, CONV_PAD:CONV_PAD + length, :] = x_ref[0, 0]
    w = w_ref[0]
    win_rows = rt + 2 * CONV_HALO
    is_q = blk < N_HEADS
    is_qk = blk < 2 * N_HEADS

    def neighbours(i, carry):
        s = pl.multiple_of(i * rt, rt)
        col = (lax.broadcasted_iota(jnp.int32, (rt, HEAD_DIM), 0) + s) % width
        win = pad_scr[0, pl.ds(pl.multiple_of(s + (CONV_PAD - CONV_HALO), 8), win_rows), :]
        left = pltpu.roll(win, 1, 0)[CONV_HALO:CONV_HALO + rt]
        right = pltpu.roll(win, win_rows - 1, 0)[CONV_HALO:CONV_HALO + rt]
        dst = pl.ds(pl.multiple_of(s + CONV_PAD, 8), rt)
        pad_scr[1, dst, :] = jnp.where(col >= 1, left, 0.0)
        pad_scr[2, dst, :] = jnp.where(col <= width - 2, right, 0.0)
        return carry

    lax.fori_loop(0, length // rt, neighbours, 0)

    def tile(i, carry):
        s = pl.multiple_of(i * rt, rt)
        acc = jnp.zeros((rt, HEAD_DIM), F32)
        for dr in ((-1, 0, 1) if two_d else (0,)):
            src = pl.ds(pl.multiple_of(s + (CONV_PAD + dr * width), 8), rt)
            for j, dw in ((1, -1), (0, 0), (2, 1)):
                tap = (dr + 1) * 3 + (dw + 1)
                acc = acc + pad_scr[j, src, :] * w[tap:tap + 1, :]
        hs = 0.5 * acc
        a = hs + hs * jnp.tanh(hs)
        nrm = lax.rsqrt(jnp.sum(a * a, axis=-1, keepdims=True) + NORM_EPS)
        f = jnp.where(is_q, nrm * HEAD_DIM ** -0.5, jnp.where(is_qk, nrm, 1.0))
        o_ref[0, 0, pl.ds(s, rt), :] = a * f
        return carry

    lax.fori_loop(0, length // rt, tile, 0)


def _gdn_conv(yh, conv_w, two_d):
    bsz, _, length, _ = yh.shape
    width = GRID_W if two_d else length
    rt = min(512, length)
    nblk = 3 * N_HEADS
    return pl.pallas_call(
        functools.partial(_conv_body, length=length, width=width, two_d=two_d, rt=rt),
        grid=(bsz, nblk),
        in_specs=[pl.BlockSpec((1, 1, length, HEAD_DIM), lambda b, j: (b, OFF_BQ + j, 0, 0)),
                  pl.BlockSpec((1, 9, HEAD_DIM), lambda b, j: (j, 0, 0))],
        out_specs=pl.BlockSpec((1, 1, length, HEAD_DIM), lambda b, j: (b, j, 0, 0)),
        out_shape=jax.ShapeDtypeStruct((bsz, nblk, length, HEAD_DIM), F32),
        scratch_shapes=[pltpu.VMEM((3, length + 2 * CONV_PAD, HEAD_DIM), F32)],
        compiler_params=pltpu.CompilerParams(dimension_semantics=("parallel", "parallel"),
                                             vmem_limit_bytes=VMEM_LIMIT),
    )(yh, conv_w)


HG_LEVELS = (32, 16, 8, 4, 2, 1)
HG_PAIRS = ((0, 1), (2, 3), (4, 5))
HG_GROUP = 4
LOG2E = 1.4426950408889634


def _hg_constants():
    t = np.arange(CHUNK)[:, None]
    col = np.arange(HEAD_DIM)[None, :]
    s = col % CHUNK
    out = [np.broadcast_to(np.where((t & m) != 0, LOG2E, -LOG2E), (CHUNK, HEAD_DIM)) for m in HG_LEVELS]
    for rev in (False, True):
        for pa, pb in HG_PAIRS:
            keep = np.zeros((CHUNK, HEAD_DIM), bool)
            for half, lv in ((col < CHUNK, pa), (col >= CHUNK, pb)):
                m = HG_LEVELS[lv]
                split = ((t ^ s) >> (m.bit_length() - 1)) == 1
                t_hi = (t & m) != 0
                keep |= half & split & (~t_hi if rev else t_hi)
            out.append(keep)
    out.append((col == t) & (col < CHUNK))
    return jnp.asarray(np.stack([np.asarray(o, np.float32) for o in out]))


def _hg_triangles():
    t = np.arange(CHUNK)
    lower = (t[None, :] <= t[:, None]).astype(np.float32)
    return jnp.asarray(np.stack([np.tile(lower, (1, 3)), np.tile(lower.T, (1, 3))]), BF16)


def _cumsum_rows(g, tri3):
    hi = g.astype(BF16)
    r1 = g - hi.astype(F32)
    mid = r1.astype(BF16)
    lo = (r1 - mid.astype(F32)).astype(BF16)
    return _dot(tri3, jnp.concatenate([hi, mid, lo], axis=0))


def _level_operand(b, b_ref, q, k, m, rev):
    parts = []
    for blk in range(CHUNK // m):
        rows = slice(blk * m, (blk + 1) * m)
        r = (blk // 2) * 2 * m + m
        ref = jnp.broadcast_to(b_ref[r:r + 1, :], (m, HEAD_DIM))
        q_side = (blk % 2 == 1) != rev
        e = (b[rows] - ref) if q_side else (ref - b[rows])
        parts.append(((q if q_side else k)[rows] * jnp.exp(e)).astype(BF16))
    return jnp.concatenate(parts, axis=0)


def _level_ref(b_ref, m, sub):
    bc = lambda r, n: jnp.broadcast_to(b_ref[r:r + 1, :], (n, HEAD_DIM))
    if m >= 4:
        return jnp.concatenate([bc(blk * 2 * m + m, 2 * m) for blk in range(CHUNK // (2 * m))], axis=0)
    lo = jnp.concatenate([bc(v * 8 + 2, 8) for v in range(CHUNK // 8)], axis=0)
    hi = jnp.concatenate([bc(v * 8 + 6, 8) for v in range(CHUNK // 8)], axis=0)
    return jnp.where(sub < 4, lo, hi)


def _hg_prepare(items, readout, cst_ref, tri_ref, b_scr, sc_scr, vv_scr, u_scr, dec_scr):
    row = lax.broadcasted_iota(jnp.int32, (CHUNK, HEAD_DIM), 0)
    sub = row % 8
    even = (row & 1) == 0
    gs, ks, bs = [], [], []
    for i, (_, _, load_f, _, lb, rev) in enumerate(items):
        c1 = 0.5 - 0.5 * lb
        p = c1 * jnp.tanh(0.5 * load_f())
        g = jnp.log((0.5 + 0.5 * lb) + p)
        gs.append(g)
        ks.append(c1 - p)
        b = _cumsum_rows(g, tri_ref[1 if rev else 0])
        bs.append(b)
        b_scr[i] = b
    for i, (slot, _, _, load_v, _, rev) in enumerate(items):
        b = bs[i]
        btot = b[0:1, :] if rev else b[CHUNK - 1:CHUNK, :]
        v = load_v()
        vv_t = jnp.concatenate([v, v], axis=0).T.astype(BF16)
        k_dec = (ks[i] * jnp.exp(btot - b)).astype(BF16)
        u_scr[slot] = _dot(vv_t[:, :CHUNK], k_dec)
        dec_scr[slot] = jnp.exp(btot)
        vv_scr[slot] = vv_t
    if not readout:
        return

    qs = []
    for _, load_q, _, _, _, _ in items:
        q_raw = load_q()
        hs = (0.5 * HEAD_DIM ** -0.5) * q_raw
        qs.append(hs + hs * jnp.tanh(0.5 * q_raw))
    scores = [jnp.zeros((CHUNK, HEAD_DIM), F32) for _ in items]
    zero_blk = jnp.zeros((CHUNK, HEAD_DIM), BF16)
    for pi, pair in enumerate(HG_PAIRS):
        gps = []
        for i, (_, _, _, _, _, rev) in enumerate(items):
            a_mats = []
            for lv in pair:
                m = HG_LEVELS[lv]
                if m >= 8:
                    a_mats.append(_level_operand(bs[i], b_scr.at[i], qs[i], ks[i], m, rev))
                    continue
                sign = cst_ref[lv]
                if m == 1:
                    x = jnp.exp(jnp.where(even, gs[i] if rev else pltpu.roll(gs[i], CHUNK - 1, 0), 0.0))
                else:
                    d = bs[i] - _level_ref(b_scr.at[i], m, sub)
                    x = jnp.exp2((-d if rev else d) * sign)
                t_is_q = (sign < 0.0) if rev else (sign > 0.0)
                a_mats.append((jnp.where(t_is_q, qs[i], ks[i]) * x).astype(BF16))
            lhs = jnp.concatenate(a_mats, axis=1)
            rhs = jnp.concatenate([jnp.concatenate([a_mats[0], zero_blk], axis=1),
                                   jnp.concatenate([zero_blk, a_mats[1]], axis=1)], axis=0)
            gps.append(_dot_nt(lhs, rhs))
        for i, (_, _, _, _, _, rev) in enumerate(items):
            scores[i] = scores[i] + gps[i] * cst_ref[6 + (3 if rev else 0) + pi]
    for i, (slot, _, _, _, _, _) in enumerate(items):
        diag = jnp.sum(qs[i] * ks[i], axis=-1, keepdims=True) * cst_ref[12]
        sc_scr[slot, :, :HEAD_DIM] = (scores[i] + diag).astype(BF16)
        sc_scr[slot, :, HEAD_DIM:] = (qs[i] * jnp.exp(bs[i])).astype(BF16)


def _hg_recur(n, dirs, readout, sc_scr, vv_scr, u_scr, dec_scr):
    states = [s_scr[...] for s_scr, _, _ in dirs]
    for c in range(n):
        for i, (_, slot_of, store_o) in enumerate(dirs):
            slot = slot_of(c)
            if readout:
                rhs = jnp.concatenate([vv_scr[slot], states[i].astype(BF16)], axis=1)
                store_o(c, _dot_nt(sc_scr[slot], rhs))
            states[i] = states[i] * dec_scr[slot] + u_scr[slot]
    for (s_scr, _, _), st in zip(dirs, states):
        s_scr[...] = st


def _hg_body(cst_ref, tri_ref, qf_ref, ff_ref, vf_ref, qb_ref, fb_ref, vb_ref,
             cff_ref, cfb_ref, cv_ref, lb_ref, of_ref, ob_ref,
             sf_scr, sb_scr, b_scr, sc_scr, vv_scr, u_scr, dec_scr, *, n_chunks, n_ctx_chunks):
    lb_f = lb_ref[0, 0:1, :]
    lb_b = lb_ref[0, 1:2, :]
    scr = (sc_scr, vv_scr, u_scr, dec_scr)

    @pl.when(pl.program_id(2) == 0)
    def _():
        sf_scr[...] = jnp.zeros_like(sf_scr)
        sb_scr[...] = jnp.zeros_like(sb_scr)
        n = n_ctx_chunks
        for c0 in range(0, n, HG_GROUP):
            items = []
            for c in range(c0, min(c0 + HG_GROUP, n)):
                sl = slice(c * CHUNK, (c + 1) * CHUNK)
                items.append((c, None, lambda sl=sl: cff_ref[0, 0, sl, :], lambda sl=sl: cv_ref[0, 0, sl, :], lb_f, False))
                items.append((n + c, None, lambda sl=sl: cfb_ref[0, 0, sl, :], lambda sl=sl: cv_ref[0, 0, sl, :], lb_b, True))
            _hg_prepare(items, False, cst_ref, tri_ref, b_scr, *scr)
        _hg_recur(n, [(sf_scr, lambda c: c, None), (sb_scr, lambda c: 2 * n - 1 - c, None)], False, *scr)

    n = n_chunks

    def group(gi, carry):
        items = []
        for j in range(HG_GROUP):
            c = gi * HG_GROUP + j
            r = pl.ds(pl.multiple_of(c * CHUNK, CHUNK), CHUNK)
            items.append((c, lambda r=r: qf_ref[0, 0, r, :], lambda r=r: ff_ref[0, 0, r, :],
                          lambda r=r: vf_ref[0, 0, r, :], lb_f, False))
            items.append((n + c, lambda r=r: qb_ref[0, 0, r, :], lambda r=r: fb_ref[0, 0, r, :],
                          lambda r=r: vb_ref[0, 0, r, :], lb_b, True))
        _hg_prepare(items, True, cst_ref, tri_ref, b_scr, *scr)
        return carry

    lax.fori_loop(0, n // HG_GROUP, group, 0)

    def store_f(c, o):
        of_ref[0, 0, c * CHUNK:(c + 1) * CHUNK, :] = o

    def store_b(c, o):
        ob_ref[0, 0, (n - 1 - c) * CHUNK:(n - c) * CHUNK, :] = o

    _hg_recur(n, [(sf_scr, lambda c: c, store_f), (sb_scr, lambda c: 2 * n - 1 - c, store_b)], True, *scr)


def _hgrn2_scan(yh, yh_ctx, lb, tb):
    bsz, _, length, _ = yh.shape
    lc = yh_ctx.shape[2]
    nt = length // tb
    nc, ncc = tb // CHUNK, lc // CHUNK
    assert nc % HG_GROUP == 0
    slots = 2 * max(nc, ncc)
    blk = (1, 1, tb, HEAD_DIM)
    cblk = (1, 1, lc, HEAD_DIM)
    fwd = lambda off: pl.BlockSpec(blk, lambda b, h, t: (b, off + h, t, 0))
    bwd = lambda off: pl.BlockSpec(blk, lambda b, h, t: (b, off + h, nt - 1 - t, 0))
    ctx = lambda off: pl.BlockSpec(cblk, lambda b, h, t: (b, off + h, 0, 0))
    consts = _hg_constants()
    tri3 = _hg_triangles()
    out_shape = jax.ShapeDtypeStruct((bsz, N_HEADS, length, HEAD_DIM), F32)
    return pl.pallas_call(
        functools.partial(_hg_body, n_chunks=nc, n_ctx_chunks=ncc),
        grid=(bsz, N_HEADS, nt),
        in_specs=[pl.BlockSpec(consts.shape, lambda b, h, t: (0, 0, 0)),
                  pl.BlockSpec(tri3.shape, lambda b, h, t: (0, 0, 0)),
                  fwd(OFF_AQ), fwd(OFF_AFF), fwd(OFF_AI), bwd(OFF_AQ), bwd(OFF_AFB), bwd(OFF_AI),
                  ctx(OFF_AFF), ctx(OFF_AFB), ctx(OFF_AI),
                  pl.BlockSpec((1, 2, HEAD_DIM), lambda b, h, t: (h, 0, 0))],
        out_specs=[pl.BlockSpec(blk, lambda b, h, t: (b, h, t, 0)),
                   pl.BlockSpec(blk, lambda b, h, t: (b, h, nt - 1 - t, 0))],
        out_shape=[out_shape, out_shape],
        scratch_shapes=[pltpu.VMEM((HEAD_DIM, HEAD_DIM), F32), pltpu.VMEM((HEAD_DIM, HEAD_DIM), F32),
                        pltpu.VMEM((2 * HG_GROUP, CHUNK, HEAD_DIM), F32),
                        pltpu.VMEM((slots, CHUNK, 2 * HEAD_DIM), BF16),
                        pltpu.VMEM((slots, HEAD_DIM, HEAD_DIM), BF16),
                        pltpu.VMEM((slots, HEAD_DIM, HEAD_DIM), F32),
                        pltpu.VMEM((slots, 1, HEAD_DIM), F32)],
        compiler_params=pltpu.CompilerParams(
            dimension_semantics=("parallel", "parallel", "arbitrary"),
            vmem_limit_bytes=VMEM_LIMIT),
    )(consts, tri3, yh, yh, yh, yh, yh, yh, yh_ctx, yh_ctx, yh_ctx, lb)


def _seg_cumsum_lanes(x, rev, lane):
    total = x.shape[1]
    seg = lane % CHUNK
    for sh in (1, 2, 4, 8, 16, 32):
        if rev:
            x = x + jnp.where(seg < CHUNK - sh, pltpu.roll(x, total - sh, 1), 0.0)
        else:
            x = x + jnp.where(seg >= sh, pltpu.roll(x, sh, 1), 0.0)
    return x


def _gate_rows_body(g_ref, p_ref, o_ref):
    nrow = 2 * N_HEADS
    total = g_ref.shape[2]
    lane = lax.broadcasted_iota(jnp.int32, (nrow, total), 1)
    row = lax.broadcasted_iota(jnp.int32, (nrow, total), 0)
    z = g_ref[0, :nrow, :] + p_ref[:, 1:2]
    softplus = jnp.maximum(z, 0.0) + jnp.log1p(jnp.exp(-jnp.abs(z)))
    g = -jnp.exp(p_ref[:, 0:1]) * softplus
    prefix = _seg_cumsum_lanes(g, False, lane)
    suffix = _seg_cumsum_lanes(g, True, lane)
    o_ref[0, 0] = jnp.where(row < N_HEADS, prefix, suffix)
    o_ref[0, 1] = _sigmoid(g_ref[0, nrow:, :])
    o_ref[0, 2] = prefix + suffix - g


def _gate_rows(gates, params):
    bsz, _, length = gates.shape
    tg = min(2048, length)
    nrow = 2 * N_HEADS
    return pl.pallas_call(
        _gate_rows_body,
        grid=(bsz, length // tg),
        in_specs=[pl.BlockSpec((1, N_GATE, tg), lambda b, t: (b, 0, t)),
                  pl.BlockSpec(params.shape, lambda b, t: (0, 0))],
        out_specs=pl.BlockSpec((1, 3, nrow, tg), lambda b, t: (b, 0, 0, t)),
        out_shape=jax.ShapeDtypeStruct((bsz, 3, nrow, length), F32),
        compiler_params=pltpu.CompilerParams(dimension_semantics=("parallel", "parallel"),
                                             vmem_limit_bytes=VMEM_LIMIT),
    )(gates, params)


def _gdn_gates(g_ref, d, head):
    r = pl.ds(d * N_HEADS + head, 1)
    b = g_ref[0, 0, r, :]
    rows = jnp.concatenate([b, g_ref[0, 1, r, :], g_ref[0, 2, r, :]], axis=0)
    return rows, jnp.broadcast_to(b, (HEAD_DIM, b.shape[1])).T


def _gdn_prepare(chunks, readout, qkg_scr, tb_scr, pm_scr, kdt_scr, dec_scr):
    t_i = lax.broadcasted_iota(jnp.int32, (CHUNK, CHUNK), 0)
    s_i = lax.broadcasted_iota(jnp.int32, (CHUNK, CHUNK), 1)
    eye = (t_i == s_i).astype(F32)
    tri = {False: (s_i <= t_i, s_i < t_i), True: (s_i >= t_i, s_i > t_i)}

    dmats, qks, xs, ys = [], [], [], []
    for _, load_q, load_k, rows, bcol, rev in chunks:
        incl, strict = tri[rev]
        dmat = jnp.exp(jnp.where(incl, bcol[:, :CHUNK] - rows[0:1, :], NEG_BIG))
        kb = load_k().astype(BF16)
        if readout:
            qkk = _dot_nt(jnp.concatenate([load_q().astype(BF16), kb], axis=0), kb)
            qks.append(qkk[:CHUNK])
            kk = qkk[CHUNK:]
        else:
            kk = _dot_nt(kb, kb)
        n_mat = jnp.where(strict, kk * dmat, 0.0) * rows[1:2, :]
        dmats.append(dmat)
        xs.append(eye - n_mat)
        ys.append(n_mat.astype(BF16))
    ys = [_dot(nb, nb) for nb in ys]
    for _ in range(4):
        xy = [_dot(jnp.concatenate([x, y], axis=0).astype(BF16), y.astype(BF16)) for x, y in zip(xs, ys)]
        xs = [x + p[:CHUNK] for x, p in zip(xs, xy)]
        ys = [p[CHUNK:] for p in xy]
    xs = [x + _dot(x.astype(BF16), y.astype(BF16)) for x, y in zip(xs, ys)]

    for i, (slot, load_q, load_k, rows, bcol, rev) in enumerate(chunks):
        b_row, beta_row, tot_row = rows[0:1, :], rows[1:2, :], rows[2:3, :]
        e_col = jnp.exp(bcol)
        k = load_k()
        tb_scr[slot] = xs[i].astype(BF16)
        qkg_scr[slot, CHUNK:, :] = (k * e_col).astype(BF16)
        if readout:
            qkg_scr[slot, :CHUNK, :] = (load_q() * e_col).astype(BF16)
            pm_scr[slot] = (qks[i] * dmats[i] * beta_row).astype(BF16)
        k_t = jnp.concatenate([k, k], axis=0).T[:, :CHUNK]
        kdt_scr[slot] = (k_t * (jnp.exp(tot_row - b_row) * beta_row)).astype(BF16)
        dec_scr[slot] = jnp.exp(jnp.concatenate([tot_row, tot_row], axis=1))


def _gdn_recur(n, dirs, readout, qkg_scr, tb_scr, pm_scr, kdt_scr, dec_scr):
    def step(c, carry):
        slots = [slot_of(c) for _, slot_of, _, _ in dirs]
        states = [s_ref[...] for s_ref, _, _, _ in dirs]
        if readout:
            qks = [_dot(qkg_scr[slot], s.astype(BF16)) for slot, s in zip(slots, states)]
            kss = [p[CHUNK:] for p in qks]
        else:
            kss = [_dot(qkg_scr[slot, CHUNK:, :], s.astype(BF16)) for slot, s in zip(slots, states)]
        ws = [_dot(tb_scr[slot], (load_v(c) - ks).astype(BF16)).astype(BF16)
              for slot, ks, (_, _, load_v, _) in zip(slots, kss, dirs)]
        for slot, s, w, (s_ref, _, _, _) in zip(slots, states, ws, dirs):
            s_ref[...] = s * dec_scr[slot] + _dot(kdt_scr[slot], w)
        if readout:
            for slot, p, w, (_, _, _, store_o) in zip(slots, qks, ws, dirs):
                store_o(c, p[:CHUNK] + _dot(pm_scr[slot], w))
        return carry

    lax.fori_loop(0, n, step, 0)


def _gdn_body(qf_ref, kf_ref, vf_ref, gf_ref, qb_ref, kb_ref, vb_ref, gb_ref,
              ck_ref, cv_ref, cg_ref, of_ref, ob_ref,
              s_scr, qkg_scr, tb_scr, pm_scr, kdt_scr, dec_scr,
              *, n_chunks, n_ctx_chunks, heads):
    h0 = pl.program_id(1) * heads
    scr = (qkg_scr, tb_scr, pm_scr, kdt_scr, dec_scr)
    ds = lambda c: pl.ds(pl.multiple_of(c * CHUNK, CHUNK), CHUNK)

    def chunk_list(hh, n, q_refs, k_refs, g_refs):
        head = h0 + hh
        out = []
        for d in (0, 1):
            rows, bcol = _gdn_gates(g_refs[d], d, head)
            for c in range(n):
                sl = slice(c * CHUNK, (c + 1) * CHUNK)
                load_q = None if q_refs is None else (lambda r=q_refs[d], sl=sl: r[0, hh, sl, :])
                load_k = lambda r=k_refs[d], sl=sl: r[0, hh, sl, :]
                out.append(((2 * hh + d) * n + c, load_q, load_k, rows[:, sl], bcol[sl, :], d == 1))
        return out

    def chains(n, vf, vb, of, ob):
        out = []
        for hh in range(heads):
            out.append((s_scr.at[2 * hh], lambda c, hh=hh: 2 * hh * n + c,
                        lambda c, hh=hh: vf[0, hh, ds(c), :],
                        None if of is None else (lambda c, o, hh=hh: of.__setitem__((0, hh, ds(c), slice(None)), o))))
            out.append((s_scr.at[2 * hh + 1], lambda c, hh=hh: (2 * hh + 1) * n + (n - 1 - c),
                        lambda c, hh=hh: vb[0, hh, ds(n - 1 - c), :],
                        None if ob is None else (lambda c, o, hh=hh: ob.__setitem__((0, hh, ds(n - 1 - c), slice(None)), o))))
        return out

    @pl.when(pl.program_id(2) == 0)
    def _():
        s_scr[...] = jnp.zeros_like(s_scr)
        n = n_ctx_chunks

        def ctx_head(hh, carry):
            _gdn_prepare(chunk_list(hh, n, None, (ck_ref, ck_ref), (cg_ref, cg_ref)), False, *scr)
            return carry

        lax.fori_loop(0, heads, ctx_head, 0)
        _gdn_recur(n, chains(n, cv_ref, cv_ref, None, None), False, *scr)

    n = n_chunks

    def main_head(hh, carry):
        _gdn_prepare(chunk_list(hh, n, (qf_ref, qb_ref), (kf_ref, kb_ref), (gf_ref, gb_ref)), True, *scr)
        return carry

    lax.fori_loop(0, heads, main_head, 0)
    _gdn_recur(n, chains(n, vf_ref, vb_ref, of_ref, ob_ref), True, *scr)


GDN_HEADS = 4


def _gdn_scan(qkv, gates, qkv_ctx, gates_ctx, tb):
    bsz, _, length, _ = qkv.shape
    lc = qkv_ctx.shape[2]
    nt = length // tb
    nc, ncc = tb // CHUNK, lc // CHUNK
    hb = GDN_HEADS
    slots = 2 * hb * max(nc, ncc)
    blk = (1, hb, tb, HEAD_DIM)
    cblk = (1, hb, lc, HEAD_DIM)
    fwd = lambda off: pl.BlockSpec(blk, lambda b, h, t: (b, off // hb + h, t, 0))
    bwd = lambda off: pl.BlockSpec(blk, lambda b, h, t: (b, off // hb + h, nt - 1 - t, 0))
    ctx = lambda off: pl.BlockSpec(cblk, lambda b, h, t: (b, off // hb + h, 0, 0))
    out_shape = jax.ShapeDtypeStruct((bsz, N_HEADS, length, HEAD_DIM), F32)
    return pl.pallas_call(
        functools.partial(_gdn_body, n_chunks=nc, n_ctx_chunks=ncc, heads=hb),
        grid=(bsz, N_HEADS // hb, nt),
        in_specs=[fwd(0), fwd(N_HEADS), fwd(2 * N_HEADS),
                  pl.BlockSpec((1, 3, 2 * N_HEADS, tb), lambda b, h, t: (b, 0, 0, t)),
                  bwd(0), bwd(N_HEADS), bwd(2 * N_HEADS),
                  pl.BlockSpec((1, 3, 2 * N_HEADS, tb), lambda b, h, t: (b, 0, 0, nt - 1 - t)),
                  ctx(N_HEADS), ctx(2 * N_HEADS),
                  pl.BlockSpec((1, 3, 2 * N_HEADS, lc), lambda b, h, t: (b, 0, 0, 0))],
        out_specs=[pl.BlockSpec(blk, lambda b, h, t: (b, h, t, 0)),
                   pl.BlockSpec(blk, lambda b, h, t: (b, h, nt - 1 - t, 0))],
        out_shape=[out_shape, out_shape],
        scratch_shapes=[pltpu.VMEM((2 * hb, HEAD_DIM, HEAD_DIM), F32),
                        pltpu.VMEM((slots, 2 * CHUNK, HEAD_DIM), BF16),
                        pltpu.VMEM((slots, CHUNK, CHUNK), BF16),
                        pltpu.VMEM((slots, CHUNK, CHUNK), BF16),
                        pltpu.VMEM((slots, HEAD_DIM, CHUNK), BF16),
                        pltpu.VMEM((slots, 1, HEAD_DIM), F32)],
        compiler_params=pltpu.CompilerParams(
            dimension_semantics=("parallel", "parallel", "arbitrary"),
            vmem_limit_bytes=VMEM_LIMIT),
    )(qkv, qkv, qkv, gates, qkv, qkv, qkv, gates, qkv_ctx, qkv_ctx, gates_ctx)


def _out_body(oaf_ref, oab_ref, obf_ref, obb_ref, za_ref, zb_ref, naw_ref, nbw_ref,
              w_ref, x_ref, gate_ref, fw_ref, o_ref):
    parts = []
    for of_ref, ob_ref, z_ref, nw_ref in ((oaf_ref, oab_ref, za_ref, naw_ref),
                                          (obf_ref, obb_ref, zb_ref, nbw_ref)):
        for h in range(N_HEADS):
            o = of_ref[0, h] + ob_ref[0, h]
            o = o * lax.rsqrt(jnp.mean(o * o, axis=-1, keepdims=True) + NORM_EPS) * nw_ref[h:h + 1, :]
            parts.append((_silu(z_ref[0, h]) * o).astype(BF16))
    y = jnp.concatenate(parts, axis=1)
    xo = x_ref[0] + gate_ref[0] * _dot(y, w_ref[...])
    ms = jnp.mean(xo * xo, axis=-1, keepdims=True)
    o_ref[0] = xo * lax.rsqrt(ms + NORM_EPS) * fw_ref[...]


def _out_stage(oa_f, oa_b, ob_f, ob_b, yh, na_w, nb_w, w_out, x, gate, final_w, tm):
    bsz, length, d = x.shape
    hblk = (1, N_HEADS, tm, HEAD_DIM)
    ospec = pl.BlockSpec(hblk, lambda b, m: (b, 0, m, 0))
    full2 = lambda a: pl.BlockSpec(a.shape, lambda b, m: (0, 0))
    return pl.pallas_call(
        _out_body,
        grid=(bsz, length // tm),
        in_specs=[ospec, ospec, ospec, ospec,
                  pl.BlockSpec(hblk, lambda b, m: (b, OFF_AZ // N_HEADS, m, 0)),
                  pl.BlockSpec(hblk, lambda b, m: (b, OFF_BZ // N_HEADS, m, 0)),
                  full2(na_w), full2(nb_w), full2(w_out),
                  pl.BlockSpec((1, tm, d), lambda b, m: (b, m, 0)),
                  pl.BlockSpec((1, 1, d), lambda b, m: (b, 0, 0)),
                  full2(final_w)],
        out_specs=pl.BlockSpec((1, tm, d), lambda b, m: (b, m, 0)),
        out_shape=jax.ShapeDtypeStruct((bsz, length, d), F32),
        compiler_params=pltpu.CompilerParams(dimension_semantics=("parallel", "parallel"),
                                             vmem_limit_bytes=VMEM_LIMIT),
    )(oa_f, oa_b, ob_f, ob_b, yh, yh, na_w, nb_w, w_out, x, gate, final_w)


def kernel(x, c, ctx, c_ctx, norm_w, ada_w, ada_b, w_in, conv_w, hg_lb_logits, gdn_a_log,
           gdn_dt_bias, ha_norm_w, hb_norm_w, w_out, final_norm_w):
    bsz, length, d = x.shape
    lc = ctx.shape[1]
    assert d == D_MODEL and length % 512 == 0 and length % GRID_W == 0 and lc % CHUNK == 0
    assert w_in.shape[0] == 1, "single-layer block"

    n_cond = bsz + 1
    cond = jnp.concatenate([c, c_ctx[None, :], jnp.zeros((-n_cond % 8, d), F32)], axis=0)
    mod = _adaln(cond, ada_w[0], ada_b[0])
    shift, scale, gate = mod[:, :d], mod[:, d:2 * d], mod[:, 2 * d:]
    lat = lambda m: m[:bsz, None, :]
    rep_ctx = lambda m: jnp.broadcast_to(m[bsz:bsz + 1, None, :], (bsz, 1, d))

    w_main = w_in[0, :, :N_MAIN].astype(BF16)
    w_gate_t = w_in[0, :, N_MAIN:].T.astype(BF16)
    nw = norm_w[0].reshape(1, d)
    yh, gates = _inproj(x, nw, lat(scale), lat(shift), w_main, w_gate_t, tm=min(1024, length))
    yh_c, gates_c = _inproj(ctx, nw, rep_ctx(scale), rep_ctx(shift), w_main, w_gate_t, tm=lc)

    lb = jax.nn.softmax(hg_lb_logits.astype(F32), axis=0)[0]
    lb = lb.reshape(2, N_HEADS, HEAD_DIM).transpose(1, 0, 2)
    tb = min(512, length)
    oa_f, oa_b = _hgrn2_scan(yh, yh_c, lb, tb)

    cw = conv_w[0].reshape(9, 3 * N_HEADS, HEAD_DIM).transpose(1, 0, 2)
    qkv = _gdn_conv(yh, cw, two_d=True)
    qkv_c = _gdn_conv(yh_c, cw, two_d=False)
    params = jnp.stack([gdn_a_log[0].reshape(-1), gdn_dt_bias[0].reshape(-1)], axis=1).astype(F32)
    ob_f, ob_b = _gdn_scan(qkv, _gate_rows(gates, params), qkv_c, _gate_rows(gates_c, params), tb)

    return _out_stage(oa_f, oa_b, ob_f, ob_b, yh, ha_norm_w[0], hb_norm_w[0],
                      w_out[0].astype(BF16), x, lat(gate), final_norm_w.reshape(1, d),
                      tm=min(256, length))
```

```python
import functools

import jax
import jax.numpy as jnp
import numpy as np
from jax import lax
from jax.experimental import pallas as pl
from jax.experimental.pallas import tpu as pltpu

F32 = jnp.float32
BF16 = jnp.bfloat16

D_MODEL = 1024
N_HEADS = 8
HEAD_DIM = 128
CHUNK = 64
GRID_W = 64
NORM_EPS = 1e-6
N_MAIN = 72 * HEAD_DIM
N_GATE = 4 * N_HEADS
OFF_AQ, OFF_AFF, OFF_AFB, OFF_AI, OFF_AZ, OFF_BQ, OFF_BZ = 0, 8, 16, 24, 32, 40, 64
NEG_BIG = -1e30
VMEM_LIMIT = 56 * 1024 * 1024


def _dot(a, b):
    return jnp.dot(a, b, preferred_element_type=F32)


def _dot_nt(a, b):
    return lax.dot_general(a, b, (((1,), (1,)), ((), ())), preferred_element_type=F32)


def _sigmoid(x):
    return 1.0 / (1.0 + jnp.exp(-x))


def _silu(x):
    return x * _sigmoid(x)


def _adaln_body(c_ref, w_ref, b_ref, o_ref):
    o_ref[...] = _dot(_silu(c_ref[...]), w_ref[...]) + b_ref[...]


def _adaln(cond, ada_w, ada_b):
    rows, d = cond.shape
    n = ada_w.shape[1]
    tn = 1024
    return pl.pallas_call(
        _adaln_body,
        grid=(n // tn,),
        in_specs=[pl.BlockSpec((rows, d), lambda j: (0, 0)),
                  pl.BlockSpec((d, tn), lambda j: (0, j)),
                  pl.BlockSpec((1, tn), lambda j: (0, j))],
        out_specs=pl.BlockSpec((rows, tn), lambda j: (0, j)),
        out_shape=jax.ShapeDtypeStruct((rows, n), F32),
        compiler_params=pltpu.CompilerParams(dimension_semantics=("arbitrary",),
                                             vmem_limit_bytes=VMEM_LIMIT),
    )(cond, ada_w, ada_b.reshape(1, n))


def _inproj_body(x_ref, nw_ref, sc_ref, sh_ref, w_ref, wg_ref, y_ref, yg_ref, h_scr, *, tn):
    @pl.when(pl.program_id(2) == 0)
    def _():
        x = x_ref[0]
        ms = jnp.mean(x * x, axis=-1, keepdims=True)
        h = x * lax.rsqrt(ms + NORM_EPS) * nw_ref[...]
        h = (h * (1.0 + sc_ref[0]) + sh_ref[0]).astype(BF16)
        h_scr[...] = h
        yg_ref[0] = _dot_nt(wg_ref[...], h)

    acc = _dot(h_scr[...], w_ref[...])
    for j in range(tn // HEAD_DIM):
        y_ref[0, j] = acc[:, j * HEAD_DIM:(j + 1) * HEAD_DIM].astype(y_ref.dtype)


def _inproj(x, norm_w, scale, shift, w_main, w_gate_t, tm):
    bsz, length, d = x.shape
    tn = 1024
    grid = (bsz, length // tm, N_MAIN // tn)
    return pl.pallas_call(
        functools.partial(_inproj_body, tn=tn),
        grid=grid,
        in_specs=[pl.BlockSpec((1, tm, d), lambda b, m, n: (b, m, 0)),
                  pl.BlockSpec((1, d), lambda b, m, n: (0, 0)),
                  pl.BlockSpec((1, 1, d), lambda b, m, n: (b, 0, 0)),
                  pl.BlockSpec((1, 1, d), lambda b, m, n: (b, 0, 0)),
                  pl.BlockSpec((d, tn), lambda b, m, n: (0, n)),
                  pl.BlockSpec((N_GATE, d), lambda b, m, n: (0, 0))],
        out_specs=[pl.BlockSpec((1, tn // HEAD_DIM, tm, HEAD_DIM), lambda b, m, n: (b, n, m, 0)),
                   pl.BlockSpec((1, N_GATE, tm), lambda b, m, n: (b, 0, m))],
        out_shape=[jax.ShapeDtypeStruct((bsz, N_MAIN // HEAD_DIM, length, HEAD_DIM), BF16),
                   jax.ShapeDtypeStruct((bsz, N_GATE, length), F32)],
        scratch_shapes=[pltpu.VMEM((tm, d), BF16)],
        compiler_params=pltpu.CompilerParams(
            dimension_semantics=("parallel", "parallel", "arbitrary"),
            vmem_limit_bytes=VMEM_LIMIT),
    )(x, norm_w, scale, shift, w_main, w_gate_t)


CONV_PAD = 72
CONV_HALO = 8


def _conv_body(x_ref, w_ref, o_ref, pad_scr, *, length, width, two_d, rt):
    blk = pl.program_id(1)
    zeros = jnp.zeros((CONV_PAD, HEAD_DIM), F32)
    pad_scr[0:CONV_PAD, :] = zeros
    pad_scr[CONV_PAD + length:CONV_PAD + length + CONV_PAD, :] = zeros
    pad_scr[CONV_PAD:CONV_PAD + length, :] = x_ref[0, 0].astype(F32)
    w = w_ref[0]
    win_rows = rt + 2 * CONV_HALO
    is_q = blk < N_HEADS
    is_qk = blk < 2 * N_HEADS

    def tile(i, carry):
        s = pl.multiple_of(i * rt, rt)
        col = (lax.broadcasted_iota(jnp.int32, (rt, HEAD_DIM), 0) + s) & (width - 1)
        sums = [None, None, None]
        for dr in ((-1, 0, 1) if two_d else (0,)):
            base = pl.multiple_of(s + (CONV_PAD + dr * width - CONV_HALO), 8)
            win = pad_scr[pl.ds(base, win_rows), :]
            for j in range(3):
                tap = (dr + 1) * 3 + j
                term = win * w[tap:tap + 1, :]
                sums[j] = term if sums[j] is None else sums[j] + term
        inner = slice(CONV_HALO, CONV_HALO + rt)
        left = pltpu.roll(sums[0], 1, 0)[inner]
        right = pltpu.roll(sums[2], win_rows - 1, 0)[inner]
        acc = sums[1][inner] + jnp.where(col >= 1, left, 0.0) + jnp.where(col <= width - 2, right, 0.0)
        hs = 0.5 * acc
        a = hs + hs * jnp.tanh(hs)
        nrm = lax.rsqrt(jnp.sum(a * a, axis=-1, keepdims=True) + NORM_EPS)
        f = jnp.where(is_q, nrm * HEAD_DIM ** -0.5, jnp.where(is_qk, nrm, 1.0))
        o_ref[0, 0, pl.ds(s, rt), :] = (a * f).astype(o_ref.dtype)
        return carry

    lax.fori_loop(0, length // rt, tile, 0)


def _gdn_conv(yh, conv_w, two_d):
    bsz, _, length, _ = yh.shape
    width = GRID_W if two_d else length
    assert width & (width - 1) == 0, "grid width must be a power of two (column index by bit mask)"
    rt = min(512, length)
    nblk = 3 * N_HEADS
    return pl.pallas_call(
        functools.partial(_conv_body, length=length, width=width, two_d=two_d, rt=rt),
        grid=(bsz, nblk),
        in_specs=[pl.BlockSpec((1, 1, length, HEAD_DIM), lambda b, j: (b, OFF_BQ + j, 0, 0)),
                  pl.BlockSpec((1, 9, HEAD_DIM), lambda b, j: (j, 0, 0))],
        out_specs=pl.BlockSpec((1, 1, length, HEAD_DIM), lambda b, j: (b, j, 0, 0)),
        out_shape=jax.ShapeDtypeStruct((bsz, nblk, length, HEAD_DIM), BF16),
        scratch_shapes=[pltpu.VMEM((length + 2 * CONV_PAD, HEAD_DIM), F32)],
        compiler_params=pltpu.CompilerParams(dimension_semantics=("parallel", "parallel"),
                                             vmem_limit_bytes=VMEM_LIMIT),
    )(yh, conv_w)


HG_LEVELS = (32, 16, 8, 4, 2, 1)
HG_PAIRS = ((0, 1), (2, 3), (4, 5))
HG_GROUP = 4
LOG2E = 1.4426950408889634


def _hg_constants():
    t = np.arange(CHUNK)[:, None]
    col = np.arange(HEAD_DIM)[None, :]
    s = col % CHUNK
    out = [np.broadcast_to(np.where((t & m) != 0, LOG2E, -LOG2E), (CHUNK, HEAD_DIM)) for m in HG_LEVELS]
    for rev in (False, True):
        for pa, pb in HG_PAIRS:
            keep = np.zeros((CHUNK, HEAD_DIM), bool)
            for half, lv in ((col < CHUNK, pa), (col >= CHUNK, pb)):
                m = HG_LEVELS[lv]
                split = ((t ^ s) >> (m.bit_length() - 1)) == 1
                t_hi = (t & m) != 0
                keep |= half & split & (~t_hi if rev else t_hi)
            out.append(keep)
    out.append((col == t) & (col < CHUNK))
    return jnp.asarray(np.stack([np.asarray(o, np.float32) for o in out]))


def _hg_triangles():
    t = np.arange(CHUNK)
    lower = (t[None, :] <= t[:, None]).astype(np.float32)
    return jnp.asarray(np.stack([np.tile(lower, (1, 3)), np.tile(lower.T, (1, 3))]), BF16)


def _cumsum_rows(g, tri3):
    hi = g.astype(BF16)
    r1 = g - hi.astype(F32)
    mid = r1.astype(BF16)
    lo = (r1 - mid.astype(F32)).astype(BF16)
    return _dot(tri3, jnp.concatenate([hi, mid, lo], axis=0))


def _level_operand(b, b_ref, q, k, m, rev):
    parts = []
    for blk in range(CHUNK // m):
        rows = slice(blk * m, (blk + 1) * m)
        r = (blk // 2) * 2 * m + m
        ref = jnp.broadcast_to(b_ref[r:r + 1, :], (m, HEAD_DIM))
        q_side = (blk % 2 == 1) != rev
        e = (b[rows] - ref) if q_side else (ref - b[rows])
        parts.append(((q if q_side else k)[rows] * jnp.exp(e)).astype(BF16))
    return jnp.concatenate(parts, axis=0)


def _level_ref(b_ref, m, sub):
    bc = lambda r, n: jnp.broadcast_to(b_ref[r:r + 1, :], (n, HEAD_DIM))
    if m >= 4:
        return jnp.concatenate([bc(blk * 2 * m + m, 2 * m) for blk in range(CHUNK // (2 * m))], axis=0)
    lo = jnp.concatenate([bc(v * 8 + 2, 8) for v in range(CHUNK // 8)], axis=0)
    hi = jnp.concatenate([bc(v * 8 + 6, 8) for v in range(CHUNK // 8)], axis=0)
    return jnp.where(sub < 4, lo, hi)


def _hg_prepare(items, readout, cst_ref, tri_ref, b_scr, sc_scr, vv_scr, u_scr, dec_scr):
    row = lax.broadcasted_iota(jnp.int32, (CHUNK, HEAD_DIM), 0)
    sub = row & 7
    even = (row & 1) == 0
    gs, ks, bs = [], [], []
    for i, (_, _, load_f, _, lb, rev) in enumerate(items):
        c1 = 0.5 - 0.5 * lb
        p = c1 * jnp.tanh(0.5 * load_f())
        g = jnp.log((0.5 + 0.5 * lb) + p)
        gs.append(g)
        ks.append(c1 - p)
        b = _cumsum_rows(g, tri_ref[1 if rev else 0])
        bs.append(b)
        b_scr[i] = b
    for i, (slot, _, _, load_v, _, rev) in enumerate(items):
        b = bs[i]
        btot = b[0:1, :] if rev else b[CHUNK - 1:CHUNK, :]
        v = load_v()
        vv_t = jnp.concatenate([v, v], axis=0).T.astype(BF16)
        k_dec = (ks[i] * jnp.exp(btot - b)).astype(BF16)
        u_scr[slot] = _dot(vv_t[:, :CHUNK], k_dec)
        dec_scr[slot] = jnp.exp(btot)
        vv_scr[slot] = vv_t
    if not readout:
        return

    qs = []
    for _, load_q, _, _, _, _ in items:
        q_raw = load_q()
        hs = (0.5 * HEAD_DIM ** -0.5) * q_raw
        qs.append(hs + hs * jnp.tanh(0.5 * q_raw))
    scores = [jnp.zeros((CHUNK, HEAD_DIM), F32) for _ in items]
    zero_blk = jnp.zeros((CHUNK, HEAD_DIM), BF16)
    for pi, pair in enumerate(HG_PAIRS):
        gps = []
        for i, (_, _, _, _, _, rev) in enumerate(items):
            a_mats = []
            for lv in pair:
                m = HG_LEVELS[lv]
                if m >= 8:
                    a_mats.append(_level_operand(bs[i], b_scr.at[i], qs[i], ks[i], m, rev))
                    continue
                sign = cst_ref[lv]
                if m == 1:
                    x = jnp.exp(jnp.where(even, gs[i] if rev else pltpu.roll(gs[i], CHUNK - 1, 0), 0.0))
                else:
                    d = bs[i] - _level_ref(b_scr.at[i], m, sub)
                    x = jnp.exp2((-d if rev else d) * sign)
                t_is_q = (sign < 0.0) if rev else (sign > 0.0)
                a_mats.append((jnp.where(t_is_q, qs[i], ks[i]) * x).astype(BF16))
            lhs = jnp.concatenate(a_mats, axis=1)
            rhs = jnp.concatenate([jnp.concatenate([a_mats[0], zero_blk], axis=1),
                                   jnp.concatenate([zero_blk, a_mats[1]], axis=1)], axis=0)
            gps.append(_dot_nt(lhs, rhs))
        for i, (_, _, _, _, _, rev) in enumerate(items):
            scores[i] = scores[i] + gps[i] * cst_ref[6 + (3 if rev else 0) + pi]
    for i, (slot, _, _, _, _, _) in enumerate(items):
        diag = jnp.sum(qs[i] * ks[i], axis=-1, keepdims=True) * cst_ref[12]
        sc_scr[slot, :, :HEAD_DIM] = (scores[i] + diag).astype(BF16)
        sc_scr[slot, :, HEAD_DIM:] = (qs[i] * jnp.exp(bs[i])).astype(BF16)


def _hg_recur(n, dirs, readout, sc_scr, vv_scr, u_scr, dec_scr):
    states = [s_scr[...] for s_scr, _, _ in dirs]
    for c in range(n):
        for i, (_, slot_of, store_o) in enumerate(dirs):
            slot = slot_of(c)
            if readout:
                rhs = jnp.concatenate([vv_scr[slot], states[i].astype(BF16)], axis=1)
                store_o(c, _dot_nt(sc_scr[slot], rhs))
            states[i] = states[i] * dec_scr[slot] + u_scr[slot]
    for (s_scr, _, _), st in zip(dirs, states):
        s_scr[...] = st


def _hg_body(cst_ref, tri_ref, qf_ref, ff_ref, vf_ref, qb_ref, fb_ref, vb_ref,
             cff_ref, cfb_ref, cv_ref, lb_ref, of_ref, ob_ref,
             sf_scr, sb_scr, b_scr, sc_scr, vv_scr, u_scr, dec_scr, *, n_chunks, n_ctx_chunks):
    lb_f = lb_ref[0, 0:1, :]
    lb_b = lb_ref[0, 1:2, :]
    scr = (sc_scr, vv_scr, u_scr, dec_scr)
    load = lambda ref, rows: (lambda: ref[0, 0, rows, :].astype(F32))

    @pl.when(pl.program_id(2) == 0)
    def _():
        sf_scr[...] = jnp.zeros_like(sf_scr)
        sb_scr[...] = jnp.zeros_like(sb_scr)
        n = n_ctx_chunks
        for c0 in range(0, n, HG_GROUP):
            items = []
            for c in range(c0, min(c0 + HG_GROUP, n)):
                sl = slice(c * CHUNK, (c + 1) * CHUNK)
                items.append((c, None, load(cff_ref, sl), load(cv_ref, sl), lb_f, False))
                items.append((n + c, None, load(cfb_ref, sl), load(cv_ref, sl), lb_b, True))
            _hg_prepare(items, False, cst_ref, tri_ref, b_scr, *scr)
        _hg_recur(n, [(sf_scr, lambda c: c, None), (sb_scr, lambda c: 2 * n - 1 - c, None)], False, *scr)

    n = n_chunks

    def group(gi, carry):
        items = []
        for j in range(HG_GROUP):
            c = gi * HG_GROUP + j
            r = pl.ds(pl.multiple_of(c * CHUNK, CHUNK), CHUNK)
            items.append((c, load(qf_ref, r), load(ff_ref, r), load(vf_ref, r), lb_f, False))
            items.append((n + c, load(qb_ref, r), load(fb_ref, r), load(vb_ref, r), lb_b, True))
        _hg_prepare(items, True, cst_ref, tri_ref, b_scr, *scr)
        return carry

    lax.fori_loop(0, n // HG_GROUP, group, 0)

    def store_f(c, o):
        of_ref[0, 0, c * CHUNK:(c + 1) * CHUNK, :] = o.astype(of_ref.dtype)

    def store_b(c, o):
        ob_ref[0, 0, (n - 1 - c) * CHUNK:(n - c) * CHUNK, :] = o.astype(ob_ref.dtype)

    _hg_recur(n, [(sf_scr, lambda c: c, store_f), (sb_scr, lambda c: 2 * n - 1 - c, store_b)], True, *scr)


def _hgrn2_scan(yh, yh_ctx, lb, tb):
    bsz, _, length, _ = yh.shape
    lc = yh_ctx.shape[2]
    nt = length // tb
    nc, ncc = tb // CHUNK, lc // CHUNK
    assert nc % HG_GROUP == 0
    slots = 2 * max(nc, ncc)
    blk = (1, 1, tb, HEAD_DIM)
    cblk = (1, 1, lc, HEAD_DIM)
    fwd = lambda off: pl.BlockSpec(blk, lambda b, h, t: (b, off + h, t, 0))
    bwd = lambda off: pl.BlockSpec(blk, lambda b, h, t: (b, off + h, nt - 1 - t, 0))
    ctx = lambda off: pl.BlockSpec(cblk, lambda b, h, t: (b, off + h, 0, 0))
    consts = _hg_constants()
    tri3 = _hg_triangles()
    out_shape = jax.ShapeDtypeStruct((bsz, N_HEADS, length, HEAD_DIM), BF16)
    return pl.pallas_call(
        functools.partial(_hg_body, n_chunks=nc, n_ctx_chunks=ncc),
        grid=(bsz, N_HEADS, nt),
        in_specs=[pl.BlockSpec(consts.shape, lambda b, h, t: (0, 0, 0)),
                  pl.BlockSpec(tri3.shape, lambda b, h, t: (0, 0, 0)),
                  fwd(OFF_AQ), fwd(OFF_AFF), fwd(OFF_AI), bwd(OFF_AQ), bwd(OFF_AFB), bwd(OFF_AI),
                  ctx(OFF_AFF), ctx(OFF_AFB), ctx(OFF_AI),
                  pl.BlockSpec((1, 2, HEAD_DIM), lambda b, h, t: (h, 0, 0))],
        out_specs=[pl.BlockSpec(blk, lambda b, h, t: (b, h, t, 0)),
                   pl.BlockSpec(blk, lambda b, h, t: (b, h, nt - 1 - t, 0))],
        out_shape=[out_shape, out_shape],
        scratch_shapes=[pltpu.VMEM((HEAD_DIM, HEAD_DIM), F32), pltpu.VMEM((HEAD_DIM, HEAD_DIM), F32),
                        pltpu.VMEM((2 * HG_GROUP, CHUNK, HEAD_DIM), F32),
                        pltpu.VMEM((slots, CHUNK, 2 * HEAD_DIM), BF16),
                        pltpu.VMEM((slots, HEAD_DIM, HEAD_DIM), BF16),
                        pltpu.VMEM((slots, HEAD_DIM, HEAD_DIM), F32),
                        pltpu.VMEM((slots, 1, HEAD_DIM), F32)],
        compiler_params=pltpu.CompilerParams(
            dimension_semantics=("parallel", "parallel", "arbitrary"),
            vmem_limit_bytes=VMEM_LIMIT),
    )(consts, tri3, yh, yh, yh, yh, yh, yh, yh_ctx, yh_ctx, yh_ctx, lb)


def _seg_cumsum_lanes(x, rev, lane):
    total = x.shape[1]
    seg = lane & (CHUNK - 1)
    for sh in (1, 2, 4, 8, 16, 32):
        if rev:
            x = x + jnp.where(seg < CHUNK - sh, pltpu.roll(x, total - sh, 1), 0.0)
        else:
            x = x + jnp.where(seg >= sh, pltpu.roll(x, sh, 1), 0.0)
    return x


def _gate_rows_body(g_ref, p_ref, o_ref):
    nrow = 2 * N_HEADS
    total = g_ref.shape[2]
    lane = lax.broadcasted_iota(jnp.int32, (nrow, total), 1)
    row = lax.broadcasted_iota(jnp.int32, (nrow, total), 0)
    z = g_ref[0, :nrow, :] + p_ref[:, 1:2]
    softplus = jnp.maximum(z, 0.0) + jnp.log1p(jnp.exp(-jnp.abs(z)))
    g = -jnp.exp(p_ref[:, 0:1]) * softplus
    prefix = _seg_cumsum_lanes(g, False, lane)
    suffix = _seg_cumsum_lanes(g, True, lane)
    o_ref[0, 0] = jnp.where(row < N_HEADS, prefix, suffix)
    o_ref[0, 1] = _sigmoid(g_ref[0, nrow:, :])
    o_ref[0, 2] = prefix + suffix - g


def _gate_rows(gates, params):
    bsz, _, length = gates.shape
    tg = min(2048, length)
    nrow = 2 * N_HEADS
    return pl.pallas_call(
        _gate_rows_body,
        grid=(bsz, length // tg),
        in_specs=[pl.BlockSpec((1, N_GATE, tg), lambda b, t: (b, 0, t)),
                  pl.BlockSpec(params.shape, lambda b, t: (0, 0))],
        out_specs=pl.BlockSpec((1, 3, nrow, tg), lambda b, t: (b, 0, 0, t)),
        out_shape=jax.ShapeDtypeStruct((bsz, 3, nrow, length), F32),
        compiler_params=pltpu.CompilerParams(dimension_semantics=("parallel", "parallel"),
                                             vmem_limit_bytes=VMEM_LIMIT),
    )(gates, params)


def _gdn_gates(g_ref, d, head):
    r = pl.ds(d * N_HEADS + head, 1)
    b = g_ref[0, 0, r, :]
    rows = jnp.concatenate([b, g_ref[0, 1, r, :], g_ref[0, 2, r, :]], axis=0)
    return rows, jnp.broadcast_to(b, (HEAD_DIM, b.shape[1])).T


def _gdn_prepare(chunks, readout, qkg_scr, tb_scr, pm_scr, kdt_scr, dec_scr):
    t_i = lax.broadcasted_iota(jnp.int32, (CHUNK, CHUNK), 0)
    s_i = lax.broadcasted_iota(jnp.int32, (CHUNK, CHUNK), 1)
    eye = (t_i == s_i).astype(F32)
    tri = {False: (s_i <= t_i, s_i < t_i), True: (s_i >= t_i, s_i > t_i)}

    dmats, qks, xs, ys = [], [], [], []
    for _, load_q, load_k, rows, bcol, rev in chunks:
        incl, strict = tri[rev]
        dmat = jnp.exp(jnp.where(incl, bcol[:, :CHUNK] - rows[0:1, :], NEG_BIG))
        kb = load_k()
        if readout:
            qkk = _dot_nt(jnp.concatenate([load_q(), kb], axis=0), kb)
            qks.append(qkk[:CHUNK])
            kk = qkk[CHUNK:]
        else:
            kk = _dot_nt(kb, kb)
        n_mat = jnp.where(strict, kk * dmat, 0.0) * rows[1:2, :]
        dmats.append(dmat)
        xs.append(eye - n_mat)
        ys.append(n_mat.astype(BF16))
    ys = [_dot(nb, nb) for nb in ys]
    for _ in range(4):
        xy = [_dot(jnp.concatenate([x, y], axis=0).astype(BF16), y.astype(BF16)) for x, y in zip(xs, ys)]
        xs = [x + p[:CHUNK] for x, p in zip(xs, xy)]
        ys = [p[CHUNK:] for p in xy]
    xs = [x + _dot(x.astype(BF16), y.astype(BF16)) for x, y in zip(xs, ys)]

    for i, (slot, load_q, load_k, rows, bcol, rev) in enumerate(chunks):
        b_row, beta_row, tot_row = rows[0:1, :], rows[1:2, :], rows[2:3, :]
        e_col = jnp.exp(bcol)
        k = load_k().astype(F32)
        tb_scr[slot] = xs[i].astype(BF16)
        qkg_scr[slot, CHUNK:, :] = (k * e_col).astype(BF16)
        if readout:
            qkg_scr[slot, :CHUNK, :] = (load_q().astype(F32) * e_col).astype(BF16)
            pm_scr[slot] = (qks[i] * dmats[i] * beta_row).astype(BF16)
        k_t = jnp.concatenate([k, k], axis=0).T[:, :CHUNK]
        kdt_scr[slot] = (k_t * (jnp.exp(tot_row - b_row) * beta_row)).astype(BF16)
        dec_scr[slot] = jnp.exp(jnp.concatenate([tot_row, tot_row], axis=1))


def _gdn_recur(n, dirs, readout, qkg_scr, tb_scr, pm_scr, kdt_scr, dec_scr):
    def step(c, carry):
        slots = [slot_of(c) for _, slot_of, _, _ in dirs]
        states = [s_ref[...] for s_ref, _, _, _ in dirs]
        if readout:
            qks = [_dot(qkg_scr[slot], s.astype(BF16)) for slot, s in zip(slots, states)]
            kss = [p[CHUNK:] for p in qks]
        else:
            kss = [_dot(qkg_scr[slot, CHUNK:, :], s.astype(BF16)) for slot, s in zip(slots, states)]
        ws = [_dot(tb_scr[slot], (load_v(c) - ks).astype(BF16)).astype(BF16)
              for slot, ks, (_, _, load_v, _) in zip(slots, kss, dirs)]
        for slot, s, w, (s_ref, _, _, _) in zip(slots, states, ws, dirs):
            s_ref[...] = s * dec_scr[slot] + _dot(kdt_scr[slot], w)
        if readout:
            for slot, p, w, (_, _, _, store_o) in zip(slots, qks, ws, dirs):
                store_o(c, p[:CHUNK] + _dot(pm_scr[slot], w))
        return carry

    lax.fori_loop(0, n, step, 0)


def _gdn_body(qf_ref, kf_ref, vf_ref, gf_ref, qb_ref, kb_ref, vb_ref, gb_ref,
              ck_ref, cv_ref, cg_ref, of_ref, ob_ref,
              s_scr, qkg_scr, tb_scr, pm_scr, kdt_scr, dec_scr,
              *, n_chunks, n_ctx_chunks, heads):
    h0 = pl.program_id(1) * heads
    scr = (qkg_scr, tb_scr, pm_scr, kdt_scr, dec_scr)
    ds = lambda c: pl.ds(pl.multiple_of(c * CHUNK, CHUNK), CHUNK)

    def chunk_list(hh, n, q_refs, k_refs, g_refs):
        head = h0 + hh
        out = []
        for d in (0, 1):
            rows, bcol = _gdn_gates(g_refs[d], d, head)
            for c in range(n):
                sl = slice(c * CHUNK, (c + 1) * CHUNK)
                load_q = None if q_refs is None else (lambda r=q_refs[d], sl=sl: r[0, hh, sl, :])
                load_k = lambda r=k_refs[d], sl=sl: r[0, hh, sl, :]
                out.append(((2 * hh + d) * n + c, load_q, load_k, rows[:, sl], bcol[sl, :], d == 1))
        return out

    def chains(n, vf, vb, of, ob):
        out = []
        for hh in range(heads):
            out.append((s_scr.at[2 * hh], lambda c, hh=hh: 2 * hh * n + c,
                        lambda c, hh=hh: vf[0, hh, ds(c), :].astype(F32),
                        None if of is None else (lambda c, o, hh=hh: of.__setitem__(
                            (0, hh, ds(c), slice(None)), o.astype(of.dtype)))))
            out.append((s_scr.at[2 * hh + 1], lambda c, hh=hh: (2 * hh + 1) * n + (n - 1 - c),
                        lambda c, hh=hh: vb[0, hh, ds(n - 1 - c), :].astype(F32),
                        None if ob is None else (lambda c, o, hh=hh: ob.__setitem__(
                            (0, hh, ds(n - 1 - c), slice(None)), o.astype(ob.dtype)))))
        return out

    @pl.when(pl.program_id(2) == 0)
    def _():
        s_scr[...] = jnp.zeros_like(s_scr)
        n = n_ctx_chunks

        def ctx_head(hh, carry):
            _gdn_prepare(chunk_list(hh, n, None, (ck_ref, ck_ref), (cg_ref, cg_ref)), False, *scr)
            return carry

        lax.fori_loop(0, heads, ctx_head, 0)
        _gdn_recur(n, chains(n, cv_ref, cv_ref, None, None), False, *scr)

    n = n_chunks

    def main_head(hh, carry):
        _gdn_prepare(chunk_list(hh, n, (qf_ref, qb_ref), (kf_ref, kb_ref), (gf_ref, gb_ref)), True, *scr)
        return carry

    lax.fori_loop(0, heads, main_head, 0)
    _gdn_recur(n, chains(n, vf_ref, vb_ref, of_ref, ob_ref), True, *scr)


GDN_HEADS = 4


def _gdn_scan(qkv, gates, qkv_ctx, gates_ctx, tb):
    bsz, _, length, _ = qkv.shape
    lc = qkv_ctx.shape[2]
    nt = length // tb
    nc, ncc = tb // CHUNK, lc // CHUNK
    hb = GDN_HEADS
    slots = 2 * hb * max(nc, ncc)
    blk = (1, hb, tb, HEAD_DIM)
    cblk = (1, hb, lc, HEAD_DIM)
    fwd = lambda off: pl.BlockSpec(blk, lambda b, h, t: (b, off // hb + h, t, 0))
    bwd = lambda off: pl.BlockSpec(blk, lambda b, h, t: (b, off // hb + h, nt - 1 - t, 0))
    ctx = lambda off: pl.BlockSpec(cblk, lambda b, h, t: (b, off // hb + h, 0, 0))
    out_shape = jax.ShapeDtypeStruct((bsz, N_HEADS, length, HEAD_DIM), BF16)
    return pl.pallas_call(
        functools.partial(_gdn_body, n_chunks=nc, n_ctx_chunks=ncc, heads=hb),
        grid=(bsz, N_HEADS // hb, nt),
        in_specs=[fwd(0), fwd(N_HEADS), fwd(2 * N_HEADS),
                  pl.BlockSpec((1, 3, 2 * N_HEADS, tb), lambda b, h, t: (b, 0, 0, t)),
                  bwd(0), bwd(N_HEADS), bwd(2 * N_HEADS),
                  pl.BlockSpec((1, 3, 2 * N_HEADS, tb), lambda b, h, t: (b, 0, 0, nt - 1 - t)),
                  ctx(N_HEADS), ctx(2 * N_HEADS),
                  pl.BlockSpec((1, 3, 2 * N_HEADS, lc), lambda b, h, t: (b, 0, 0, 0))],
        out_specs=[pl.BlockSpec(blk, lambda b, h, t: (b, h, t, 0)),
                   pl.BlockSpec(blk, lambda b, h, t: (b, h, nt - 1 - t, 0))],
        out_shape=[out_shape, out_shape],
        scratch_shapes=[pltpu.VMEM((2 * hb, HEAD_DIM, HEAD_DIM), F32),
                        pltpu.VMEM((slots, 2 * CHUNK, HEAD_DIM), BF16),
                        pltpu.VMEM((slots, CHUNK, CHUNK), BF16),
                        pltpu.VMEM((slots, CHUNK, CHUNK), BF16),
                        pltpu.VMEM((slots, HEAD_DIM, CHUNK), BF16),
                        pltpu.VMEM((slots, 1, HEAD_DIM), F32)],
        compiler_params=pltpu.CompilerParams(
            dimension_semantics=("parallel", "parallel", "arbitrary"),
            vmem_limit_bytes=VMEM_LIMIT),
    )(qkv, qkv, qkv, gates, qkv, qkv, qkv, gates, qkv_ctx, qkv_ctx, gates_ctx)


def _out_body(oaf_ref, oab_ref, obf_ref, obb_ref, za_ref, zb_ref, naw_ref, nbw_ref,
              w_ref, x_ref, gate_ref, fw_ref, o_ref):
    parts = []
    for of_ref, ob_ref, z_ref, nw_ref in ((oaf_ref, oab_ref, za_ref, naw_ref),
                                          (obf_ref, obb_ref, zb_ref, nbw_ref)):
        for h in range(N_HEADS):
            o = of_ref[0, h].astype(F32) + ob_ref[0, h].astype(F32)
            o = o * lax.rsqrt(jnp.mean(o * o, axis=-1, keepdims=True) + NORM_EPS) * nw_ref[h:h + 1, :]
            hz = 0.5 * z_ref[0, h].astype(F32)
            parts.append(((hz + hz * jnp.tanh(hz)) * o).astype(BF16))
    y = jnp.concatenate(parts, axis=1)
    xo = x_ref[0] + gate_ref[0] * _dot(y, w_ref[...])
    ms = jnp.mean(xo * xo, axis=-1, keepdims=True)
    o_ref[0] = xo * lax.rsqrt(ms + NORM_EPS) * fw_ref[...]


def _out_stage(oa_f, oa_b, ob_f, ob_b, yh, na_w, nb_w, w_out, x, gate, final_w, tm):
    bsz, length, d = x.shape
    hblk = (1, N_HEADS, tm, HEAD_DIM)
    ospec = pl.BlockSpec(hblk, lambda b, m: (b, 0, m, 0))
    full2 = lambda a: pl.BlockSpec(a.shape, lambda b, m: (0, 0))
    return pl.pallas_call(
        _out_body,
        grid=(bsz, length // tm),
        in_specs=[ospec, ospec, ospec, ospec,
                  pl.BlockSpec(hblk, lambda b, m: (b, OFF_AZ // N_HEADS, m, 0)),
                  pl.BlockSpec(hblk, lambda b, m: (b, OFF_BZ // N_HEADS, m, 0)),
                  full2(na_w), full2(nb_w), full2(w_out),
                  pl.BlockSpec((1, tm, d), lambda b, m: (b, m, 0)),
                  pl.BlockSpec((1, 1, d), lambda b, m: (b, 0, 0)),
                  full2(final_w)],
        out_specs=pl.BlockSpec((1, tm, d), lambda b, m: (b, m, 0)),
        out_shape=jax.ShapeDtypeStruct((bsz, length, d), F32),
        compiler_params=pltpu.CompilerParams(dimension_semantics=("parallel", "parallel"),
                                             vmem_limit_bytes=VMEM_LIMIT),
    )(oa_f, oa_b, ob_f, ob_b, yh, yh, na_w, nb_w, w_out, x, gate, final_w)


def kernel(x, c, ctx, c_ctx, norm_w, ada_w, ada_b, w_in, conv_w, hg_lb_logits, gdn_a_log,
           gdn_dt_bias, ha_norm_w, hb_norm_w, w_out, final_norm_w):
    bsz, length, d = x.shape
    lc = ctx.shape[1]
    assert d == D_MODEL and length % 512 == 0 and length % GRID_W == 0 and lc % CHUNK == 0
    assert w_in.shape[0] == 1, "single-layer block"

    n_cond = bsz + 1
    cond = jnp.concatenate([c, c_ctx[None, :], jnp.zeros((-n_cond % 8, d), F32)], axis=0)
    mod = _adaln(cond, ada_w[0], ada_b[0])
    shift, scale, gate = mod[:, :d], mod[:, d:2 * d], mod[:, 2 * d:]
    lat = lambda m: m[:bsz, None, :]
    rep_ctx = lambda m: jnp.broadcast_to(m[bsz:bsz + 1, None, :], (bsz, 1, d))

    w_main = w_in[0, :, :N_MAIN].astype(BF16)
    w_gate_t = w_in[0, :, N_MAIN:].T.astype(BF16)
    nw = norm_w[0].reshape(1, d)
    yh, gates = _inproj(x, nw, lat(scale), lat(shift), w_main, w_gate_t, tm=min(2048, length))
    yh_c, gates_c = _inproj(ctx, nw, rep_ctx(scale), rep_ctx(shift), w_main, w_gate_t, tm=lc)

    lb = jax.nn.softmax(hg_lb_logits.astype(F32), axis=0)[0]
    lb = lb.reshape(2, N_HEADS, HEAD_DIM).transpose(1, 0, 2)
    tb = min(512, length)
    oa_f, oa_b = _hgrn2_scan(yh, yh_c, lb, tb)

    cw = conv_w[0].reshape(9, 3 * N_HEADS, HEAD_DIM).transpose(1, 0, 2)
    qkv = _gdn_conv(yh, cw, two_d=True)
    qkv_c = _gdn_conv(yh_c, cw, two_d=False)
    params = jnp.stack([gdn_a_log[0].reshape(-1), gdn_dt_bias[0].reshape(-1)], axis=1).astype(F32)
    ob_f, ob_b = _gdn_scan(qkv, _gate_rows(gates, params), qkv_c, _gate_rows(gates_c, params), tb)

    return _out_stage(oa_f, oa_b, ob_f, ob_b, yh, ha_norm_w[0], hb_norm_w[0],
                      w_out[0].astype(BF16), x, lat(gate), final_norm_w.reshape(1, d),
                      tm=min(512, length))
```

```python
import functools

import jax
import jax.numpy as jnp
import numpy as np
from jax import lax
from jax.experimental import pallas as pl
from jax.experimental.pallas import tpu as pltpu

F32 = jnp.float32
BF16 = jnp.bfloat16

D_MODEL = 1024
N_HEADS = 8
HEAD_DIM = 128
CHUNK = 64
GRID_W = 64
NORM_EPS = 1e-6
N_MAIN = 72 * HEAD_DIM
N_GATE = 4 * N_HEADS
OFF_AQ, OFF_AFF, OFF_AFB, OFF_AI, OFF_AZ, OFF_BQ, OFF_BZ = 0, 8, 16, 24, 32, 40, 64
NEG_BIG = -1e30
VMEM_LIMIT = 56 * 1024 * 1024


def _dot(a, b):
    return jnp.dot(a, b, preferred_element_type=F32)


def _dot_nt(a, b):
    return lax.dot_general(a, b, (((1,), (1,)), ((), ())), preferred_element_type=F32)


def _sigmoid(x):
    return 1.0 / (1.0 + jnp.exp(-x))


def _silu(x):
    return x * _sigmoid(x)


def _adaln_body(c_ref, w_ref, b_ref, o_ref):
    o_ref[...] = _dot(_silu(c_ref[...]), w_ref[...]) + b_ref[...]


def _adaln(cond, ada_w, ada_b):
    rows, d = cond.shape
    n = ada_w.shape[1]
    tn = 1024
    return pl.pallas_call(
        _adaln_body,
        grid=(n // tn,),
        in_specs=[pl.BlockSpec((rows, d), lambda j: (0, 0)),
                  pl.BlockSpec((d, tn), lambda j: (0, j)),
                  pl.BlockSpec((1, tn), lambda j: (0, j))],
        out_specs=pl.BlockSpec((rows, tn), lambda j: (0, j)),
        out_shape=jax.ShapeDtypeStruct((rows, n), F32),
        compiler_params=pltpu.CompilerParams(dimension_semantics=("arbitrary",),
                                             vmem_limit_bytes=VMEM_LIMIT),
    )(cond, ada_w, ada_b.reshape(1, n))


def _inproj_body(x_ref, nw_ref, sc_ref, sh_ref, w_ref, wg_ref, y_ref, yg_ref, h_scr, *, tn):
    @pl.when(pl.program_id(2) == 0)
    def _():
        x = x_ref[0]
        ms = jnp.mean(x * x, axis=-1, keepdims=True)
        h = x * lax.rsqrt(ms + NORM_EPS) * nw_ref[...]
        h = (h * (1.0 + sc_ref[0]) + sh_ref[0]).astype(BF16)
        h_scr[...] = h
        yg_ref[0] = _dot_nt(wg_ref[...], h)

    acc = _dot(h_scr[...], w_ref[...])
    for j in range(tn // HEAD_DIM):
        y_ref[0, j] = acc[:, j * HEAD_DIM:(j + 1) * HEAD_DIM].astype(y_ref.dtype)


def _inproj(x, norm_w, scale, shift, w_main, w_gate_t, tm):
    bsz, length, d = x.shape
    tn = 1024
    grid = (bsz, length // tm, N_MAIN // tn)
    return pl.pallas_call(
        functools.partial(_inproj_body, tn=tn),
        grid=grid,
        in_specs=[pl.BlockSpec((1, tm, d), lambda b, m, n: (b, m, 0)),
                  pl.BlockSpec((1, d), lambda b, m, n: (0, 0)),
                  pl.BlockSpec((1, 1, d), lambda b, m, n: (b, 0, 0)),
                  pl.BlockSpec((1, 1, d), lambda b, m, n: (b, 0, 0)),
                  pl.BlockSpec((d, tn), lambda b, m, n: (0, n)),
                  pl.BlockSpec((N_GATE, d), lambda b, m, n: (0, 0))],
        out_specs=[pl.BlockSpec((1, tn // HEAD_DIM, tm, HEAD_DIM), lambda b, m, n: (b, n, m, 0)),
                   pl.BlockSpec((1, N_GATE, tm), lambda b, m, n: (b, 0, m))],
        out_shape=[jax.ShapeDtypeStruct((bsz, N_MAIN // HEAD_DIM, length, HEAD_DIM), BF16),
                   jax.ShapeDtypeStruct((bsz, N_GATE, length), F32)],
        scratch_shapes=[pltpu.VMEM((tm, d), BF16)],
        compiler_params=pltpu.CompilerParams(
            dimension_semantics=("parallel", "parallel", "arbitrary"),
            vmem_limit_bytes=VMEM_LIMIT),
    )(x, norm_w, scale, shift, w_main, w_gate_t)


CONV_PAD = 72
CONV_HALO = 8


def _conv_body(x_ref, w_ref, o_ref, pad_scr, *, length, width, two_d, rt):
    blk = pl.program_id(1)
    zeros = jnp.zeros((CONV_PAD, HEAD_DIM), F32)
    pad_scr[0:CONV_PAD, :] = zeros
    pad_scr[CONV_PAD + length:CONV_PAD + length + CONV_PAD, :] = zeros
    pad_scr[CONV_PAD:CONV_PAD + length, :] = x_ref[0, 0].astype(F32)
    w = w_ref[0]
    win_rows = rt + 2 * CONV_HALO
    is_q = blk < N_HEADS
    is_qk = blk < 2 * N_HEADS

    def tile(i, carry):
        s = pl.multiple_of(i * rt, rt)
        col = (lax.broadcasted_iota(jnp.int32, (rt, HEAD_DIM), 0) + s) & (width - 1)
        sums = [None, None, None]
        for dr in ((-1, 0, 1) if two_d else (0,)):
            base = pl.multiple_of(s + (CONV_PAD + dr * width - CONV_HALO), 8)
            win = pad_scr[pl.ds(base, win_rows), :]
            for j in range(3):
                tap = (dr + 1) * 3 + j
                term = win * w[tap:tap + 1, :]
                sums[j] = term if sums[j] is None else sums[j] + term
        inner = slice(CONV_HALO, CONV_HALO + rt)
        left = pltpu.roll(sums[0], 1, 0)[inner]
        right = pltpu.roll(sums[2], win_rows - 1, 0)[inner]
        acc = sums[1][inner] + jnp.where(col >= 1, left, 0.0) + jnp.where(col <= width - 2, right, 0.0)
        hs = 0.5 * acc
        a = hs + hs * jnp.tanh(hs)
        nrm = lax.rsqrt(jnp.sum(a * a, axis=-1, keepdims=True) + NORM_EPS)
        f = jnp.where(is_q, nrm * HEAD_DIM ** -0.5, jnp.where(is_qk, nrm, 1.0))
        o_ref[0, 0, pl.ds(s, rt), :] = (a * f).astype(o_ref.dtype)
        return carry

    lax.fori_loop(0, length // rt, tile, 0)


def _gdn_conv(yh, conv_w, two_d):
    bsz, _, length, _ = yh.shape
    width = GRID_W if two_d else length
    assert width & (width - 1) == 0, "grid width must be a power of two (column index by bit mask)"
    rt = min(512, length)
    nblk = 3 * N_HEADS
    return pl.pallas_call(
        functools.partial(_conv_body, length=length, width=width, two_d=two_d, rt=rt),
        grid=(bsz, nblk),
        in_specs=[pl.BlockSpec((1, 1, length, HEAD_DIM), lambda b, j: (b, OFF_BQ + j, 0, 0)),
                  pl.BlockSpec((1, 9, HEAD_DIM), lambda b, j: (j, 0, 0))],
        out_specs=pl.BlockSpec((1, 1, length, HEAD_DIM), lambda b, j: (b, j, 0, 0)),
        out_shape=jax.ShapeDtypeStruct((bsz, nblk, length, HEAD_DIM), BF16),
        scratch_shapes=[pltpu.VMEM((length + 2 * CONV_PAD, HEAD_DIM), F32)],
        compiler_params=pltpu.CompilerParams(dimension_semantics=("parallel", "parallel"),
                                             vmem_limit_bytes=VMEM_LIMIT),
    )(yh, conv_w)


HG_LEVELS = (32, 16, 8, 4, 2, 1)
HG_PAIRS = ((0, 1), (2, 3), (4, 5))
HG_GROUP = 4
LOG2E = 1.4426950408889634


def _hg_constants():
    t = np.arange(CHUNK)[:, None]
    col = np.arange(HEAD_DIM)[None, :]
    s = col % CHUNK
    out = [np.broadcast_to(np.where((t & m) != 0, LOG2E, -LOG2E), (CHUNK, HEAD_DIM)) for m in HG_LEVELS]
    for rev in (False, True):
        for pa, pb in HG_PAIRS:
            keep = np.zeros((CHUNK, HEAD_DIM), bool)
            for half, lv in ((col < CHUNK, pa), (col >= CHUNK, pb)):
                m = HG_LEVELS[lv]
                split = ((t ^ s) >> (m.bit_length() - 1)) == 1
                t_hi = (t & m) != 0
                keep |= half & split & (~t_hi if rev else t_hi)
            out.append(keep)
    out.append((col == t) & (col < CHUNK))
    return jnp.asarray(np.stack([np.asarray(o, np.float32) for o in out]))


def _hg_triangles():
    t = np.arange(CHUNK)
    lower = (t[None, :] <= t[:, None]).astype(np.float32)
    return jnp.asarray(np.stack([np.tile(lower, (1, 3)), np.tile(lower.T, (1, 3))]), BF16)


def _cumsum_rows(g, tri3):
    hi = g.astype(BF16)
    r1 = g - hi.astype(F32)
    mid = r1.astype(BF16)
    lo = (r1 - mid.astype(F32)).astype(BF16)
    return _dot(tri3, jnp.concatenate([hi, mid, lo], axis=0))


def _level_operand(b, b_ref, q, k, m, rev):
    parts = []
    for blk in range(CHUNK // m):
        rows = slice(blk * m, (blk + 1) * m)
        r = (blk // 2) * 2 * m + m
        ref = jnp.broadcast_to(b_ref[r:r + 1, :], (m, HEAD_DIM))
        q_side = (blk % 2 == 1) != rev
        e = (b[rows] - ref) if q_side else (ref - b[rows])
        parts.append(((q if q_side else k)[rows] * jnp.exp(e)).astype(BF16))
    return jnp.concatenate(parts, axis=0)


def _level_ref(b_ref, m, sub):
    bc = lambda r, n: jnp.broadcast_to(b_ref[r:r + 1, :], (n, HEAD_DIM))
    if m >= 4:
        return jnp.concatenate([bc(blk * 2 * m + m, 2 * m) for blk in range(CHUNK // (2 * m))], axis=0)
    lo = jnp.concatenate([bc(v * 8 + 2, 8) for v in range(CHUNK // 8)], axis=0)
    hi = jnp.concatenate([bc(v * 8 + 6, 8) for v in range(CHUNK // 8)], axis=0)
    return jnp.where(sub < 4, lo, hi)


def _hg_prepare(items, readout, cst_ref, tri_ref, b_scr, sc_scr, vv_scr, u_scr, dec_scr):
    row = lax.broadcasted_iota(jnp.int32, (CHUNK, HEAD_DIM), 0)
    sub = row & 7
    even = (row & 1) == 0
    gs, ks, bs = [], [], []
    for i, (_, _, load_f, _, lb, rev) in enumerate(items):
        c1 = 0.5 - 0.5 * lb
        p = c1 * jnp.tanh(0.5 * load_f())
        g = jnp.log((0.5 + 0.5 * lb) + p)
        gs.append(g)
        ks.append(c1 - p)
        b = _cumsum_rows(g, tri_ref[1 if rev else 0])
        bs.append(b)
        b_scr[i] = b
    for i, (slot, _, _, load_v, _, rev) in enumerate(items):
        b = bs[i]
        btot = b[0:1, :] if rev else b[CHUNK - 1:CHUNK, :]
        v = load_v()
        vv_t = jnp.concatenate([v, v], axis=0).T.astype(BF16)
        k_dec = (ks[i] * jnp.exp(btot - b)).astype(BF16)
        u_scr[slot] = _dot(vv_t[:, :CHUNK], k_dec)
        dec_scr[slot] = jnp.exp(btot)
        vv_scr[slot] = vv_t
    if not readout:
        return

    qs = []
    for _, load_q, _, _, _, _ in items:
        q_raw = load_q()
        hs = (0.5 * HEAD_DIM ** -0.5) * q_raw
        qs.append(hs + hs * jnp.tanh(0.5 * q_raw))
    scores = [jnp.zeros((CHUNK, HEAD_DIM), F32) for _ in items]
    zero_blk = jnp.zeros((CHUNK, HEAD_DIM), BF16)
    for pi, pair in enumerate(HG_PAIRS):
        gps = []
        for i, (_, _, _, _, _, rev) in enumerate(items):
            a_mats = []
            for lv in pair:
                m = HG_LEVELS[lv]
                if m >= 8:
                    a_mats.append(_level_operand(bs[i], b_scr.at[i], qs[i], ks[i], m, rev))
                    continue
                sign = cst_ref[lv]
                if m == 1:
                    x = jnp.exp(jnp.where(even, gs[i] if rev else pltpu.roll(gs[i], CHUNK - 1, 0), 0.0))
                else:
                    d = bs[i] - _level_ref(b_scr.at[i], m, sub)
                    x = jnp.exp2((-d if rev else d) * sign)
                t_is_q = (sign < 0.0) if rev else (sign > 0.0)
                a_mats.append((jnp.where(t_is_q, qs[i], ks[i]) * x).astype(BF16))
            lhs = jnp.concatenate(a_mats, axis=1)
            rhs = jnp.concatenate([jnp.concatenate([a_mats[0], zero_blk], axis=1),
                                   jnp.concatenate([zero_blk, a_mats[1]], axis=1)], axis=0)
            gps.append(_dot_nt(lhs, rhs))
        for i, (_, _, _, _, _, rev) in enumerate(items):
            scores[i] = scores[i] + gps[i] * cst_ref[6 + (3 if rev else 0) + pi]
    for i, (slot, _, _, _, _, _) in enumerate(items):
        diag = jnp.sum(qs[i] * ks[i], axis=-1, keepdims=True) * cst_ref[12]
        sc_scr[slot, :, :HEAD_DIM] = (scores[i] + diag).astype(BF16)
        sc_scr[slot, :, HEAD_DIM:] = (qs[i] * jnp.exp(bs[i])).astype(BF16)


def _hg_recur(n, dirs, readout, sc_scr, vv_scr, u_scr, dec_scr):
    states = [s_scr[...] for s_scr, _, _ in dirs]
    for c in range(n):
        for i, (_, slot_of, store_o) in enumerate(dirs):
            slot = slot_of(c)
            if readout:
                rhs = jnp.concatenate([vv_scr[slot], states[i].astype(BF16)], axis=1)
                store_o(c, _dot_nt(sc_scr[slot], rhs))
            states[i] = states[i] * dec_scr[slot] + u_scr[slot]
    for (s_scr, _, _), st in zip(dirs, states):
        s_scr[...] = st


def _hg_body(cst_ref, tri_ref, qf_ref, ff_ref, vf_ref, qb_ref, fb_ref, vb_ref,
             cff_ref, cfb_ref, cv_ref, lb_ref, of_ref, ob_ref,
             sf_scr, sb_scr, b_scr, sc_scr, vv_scr, u_scr, dec_scr, *, n_chunks, n_ctx_chunks):
    lb_f = lb_ref[0, 0:1, :]
    lb_b = lb_ref[0, 1:2, :]
    scr = (sc_scr, vv_scr, u_scr, dec_scr)
    load = lambda ref, rows: (lambda: ref[0, 0, rows, :].astype(F32))

    @pl.when(pl.program_id(2) == 0)
    def _():
        sf_scr[...] = jnp.zeros_like(sf_scr)
        sb_scr[...] = jnp.zeros_like(sb_scr)
        n = n_ctx_chunks
        for c0 in range(0, n, HG_GROUP):
            items = []
            for c in range(c0, min(c0 + HG_GROUP, n)):
                sl = slice(c * CHUNK, (c + 1) * CHUNK)
                items.append((c, None, load(cff_ref, sl), load(cv_ref, sl), lb_f, False))
                items.append((n + c, None, load(cfb_ref, sl), load(cv_ref, sl), lb_b, True))
            _hg_prepare(items, False, cst_ref, tri_ref, b_scr, *scr)
        _hg_recur(n, [(sf_scr, lambda c: c, None), (sb_scr, lambda c: 2 * n - 1 - c, None)], False, *scr)

    n = n_chunks

    def group(gi, carry):
        items = []
        for j in range(HG_GROUP):
            c = gi * HG_GROUP + j
            r = pl.ds(pl.multiple_of(c * CHUNK, CHUNK), CHUNK)
            items.append((c, load(qf_ref, r), load(ff_ref, r), load(vf_ref, r), lb_f, False))
            items.append((n + c, load(qb_ref, r), load(fb_ref, r), load(vb_ref, r), lb_b, True))
        _hg_prepare(items, True, cst_ref, tri_ref, b_scr, *scr)
        return carry

    lax.fori_loop(0, n // HG_GROUP, group, 0)

    def store_f(c, o):
        of_ref[0, 0, c * CHUNK:(c + 1) * CHUNK, :] = o.astype(of_ref.dtype)

    def store_b(c, o):
        ob_ref[0, 0, (n - 1 - c) * CHUNK:(n - c) * CHUNK, :] = o.astype(ob_ref.dtype)

    _hg_recur(n, [(sf_scr, lambda c: c, store_f), (sb_scr, lambda c: 2 * n - 1 - c, store_b)], True, *scr)


def _hgrn2_scan(yh, yh_ctx, lb, tb):
    bsz, _, length, _ = yh.shape
    lc = yh_ctx.shape[2]
    nt = length // tb
    nc, ncc = tb // CHUNK, lc // CHUNK
    assert nc % HG_GROUP == 0
    slots = 2 * max(nc, ncc)
    blk = (1, 1, tb, HEAD_DIM)
    cblk = (1, 1, lc, HEAD_DIM)
    fwd = lambda off: pl.BlockSpec(blk, lambda b, h, t: (b, off + h, t, 0))
    bwd = lambda off: pl.BlockSpec(blk, lambda b, h, t: (b, off + h, nt - 1 - t, 0))
    ctx = lambda off: pl.BlockSpec(cblk, lambda b, h, t: (b, off + h, 0, 0))
    consts = _hg_constants()
    tri3 = _hg_triangles()
    out_shape = jax.ShapeDtypeStruct((bsz, N_HEADS, length, HEAD_DIM), BF16)
    return pl.pallas_call(
        functools.partial(_hg_body, n_chunks=nc, n_ctx_chunks=ncc),
        grid=(bsz, N_HEADS, nt),
        in_specs=[pl.BlockSpec(consts.shape, lambda b, h, t: (0, 0, 0)),
                  pl.BlockSpec(tri3.shape, lambda b, h, t: (0, 0, 0)),
                  fwd(OFF_AQ), fwd(OFF_AFF), fwd(OFF_AI), bwd(OFF_AQ), bwd(OFF_AFB), bwd(OFF_AI),
                  ctx(OFF_AFF), ctx(OFF_AFB), ctx(OFF_AI),
                  pl.BlockSpec((1, 2, HEAD_DIM), lambda b, h, t: (h, 0, 0))],
        out_specs=[pl.BlockSpec(blk, lambda b, h, t: (b, h, t, 0)),
                   pl.BlockSpec(blk, lambda b, h, t: (b, h, nt - 1 - t, 0))],
        out_shape=[out_shape, out_shape],
        scratch_shapes=[pltpu.VMEM((HEAD_DIM, HEAD_DIM), F32), pltpu.VMEM((HEAD_DIM, HEAD_DIM), F32),
                        pltpu.VMEM((2 * HG_GROUP, CHUNK, HEAD_DIM), F32),
                        pltpu.VMEM((slots, CHUNK, 2 * HEAD_DIM), BF16),
                        pltpu.VMEM((slots, HEAD_DIM, HEAD_DIM), BF16),
                        pltpu.VMEM((slots, HEAD_DIM, HEAD_DIM), F32),
                        pltpu.VMEM((slots, 1, HEAD_DIM), F32)],
        compiler_params=pltpu.CompilerParams(
            dimension_semantics=("parallel", "parallel", "arbitrary"),
            vmem_limit_bytes=VMEM_LIMIT),
    )(consts, tri3, yh, yh, yh, yh, yh, yh, yh_ctx, yh_ctx, yh_ctx, lb)


def _seg_cumsum_lanes(x, rev, lane):
    total = x.shape[1]
    seg = lane & (CHUNK - 1)
    for sh in (1, 2, 4, 8, 16, 32):
        if rev:
            x = x + jnp.where(seg < CHUNK - sh, pltpu.roll(x, total - sh, 1), 0.0)
        else:
            x = x + jnp.where(seg >= sh, pltpu.roll(x, sh, 1), 0.0)
    return x


def _gate_rows_body(g_ref, p_ref, o_ref):
    nrow = 2 * N_HEADS
    total = g_ref.shape[2]
    lane = lax.broadcasted_iota(jnp.int32, (nrow, total), 1)
    row = lax.broadcasted_iota(jnp.int32, (nrow, total), 0)
    z = g_ref[0, :nrow, :] + p_ref[:, 1:2]
    softplus = jnp.maximum(z, 0.0) + jnp.log1p(jnp.exp(-jnp.abs(z)))
    g = -jnp.exp(p_ref[:, 0:1]) * softplus
    prefix = _seg_cumsum_lanes(g, False, lane)
    suffix = _seg_cumsum_lanes(g, True, lane)
    o_ref[0, 0] = jnp.where(row < N_HEADS, prefix, suffix)
    o_ref[0, 1] = _sigmoid(g_ref[0, nrow:, :])
    o_ref[0, 2] = prefix + suffix - g


def _gate_rows(gates, params):
    bsz, _, length = gates.shape
    tg = min(2048, length)
    nrow = 2 * N_HEADS
    return pl.pallas_call(
        _gate_rows_body,
        grid=(bsz, length // tg),
        in_specs=[pl.BlockSpec((1, N_GATE, tg), lambda b, t: (b, 0, t)),
                  pl.BlockSpec(params.shape, lambda b, t: (0, 0))],
        out_specs=pl.BlockSpec((1, 3, nrow, tg), lambda b, t: (b, 0, 0, t)),
        out_shape=jax.ShapeDtypeStruct((bsz, 3, nrow, length), F32),
        compiler_params=pltpu.CompilerParams(dimension_semantics=("parallel", "parallel"),
                                             vmem_limit_bytes=VMEM_LIMIT),
    )(gates, params)


def _gdn_gates(g_ref, d, head):
    r = pl.ds(d * N_HEADS + head, 1)
    b = g_ref[0, 0, r, :]
    rows = jnp.concatenate([b, g_ref[0, 1, r, :], g_ref[0, 2, r, :]], axis=0)
    return rows, jnp.broadcast_to(b, (HEAD_DIM, b.shape[1])).T


GDN_QUAD = 4
GDN_PREP_HEADS = 2


def _gdn_prepare(quads, readout, qkg_scr, tb_scr, pm_scr, kdt_scr, dec_scr):
    wide = GDN_QUAD * CHUNK
    t_i = lax.broadcasted_iota(jnp.int32, (CHUNK, wide), 0)
    lane = lax.broadcasted_iota(jnp.int32, (CHUNK, wide), 1)
    s_i = lane & (CHUNK - 1)
    blk = [(lane >= j * CHUNK) & (lane < (j + 1) * CHUNK) for j in range(GDN_QUAD)]
    eye = (t_i == s_i).astype(F32)
    tri = {False: (s_i <= t_i, s_i < t_i), True: (s_i >= t_i, s_i > t_i)}
    rows_of = lambda j: slice(j * CHUNK, (j + 1) * CHUNK)

    def diag_blocks(m, off):
        out = jnp.where(blk[0], m[off:off + CHUNK], 0.0)
        for j in range(1, GDN_QUAD):
            out = jnp.where(blk[j], m[off + j * CHUNK:off + (j + 1) * CHUNK], out)
        return out

    def block_diag(y):
        return jnp.concatenate([jnp.where(blk[j], y, 0.0).astype(BF16) for j in range(GDN_QUAD)], axis=0)

    dmats, qks, xs, ys = [], [], [], []
    for _, load_q, load_k, rows, bcol, rev in quads:
        incl, strict = tri[rev]
        bc = jnp.concatenate([bcol, bcol], axis=1)
        dmat = jnp.exp(jnp.where(incl, diag_blocks(bc, 0) - rows[0:1, :], NEG_BIG))
        kb = load_k()
        if readout:
            qkk = _dot_nt(jnp.concatenate([load_q(), kb], axis=0), kb)
            qks.append(diag_blocks(qkk, 0))
            kk = diag_blocks(qkk, wide)
        else:
            kk = diag_blocks(_dot_nt(kb, kb), 0)
        n_mat = jnp.where(strict, kk * dmat, 0.0) * rows[1:2, :]
        dmats.append(dmat)
        xs.append(eye - n_mat)
        ys.append(n_mat)
    ys = [_dot(y.astype(BF16), block_diag(y)) for y in ys]
    for _ in range(4):
        xy = [_dot(jnp.concatenate([x, y], axis=0).astype(BF16), block_diag(y)) for x, y in zip(xs, ys)]
        xs = [x + p[:CHUNK] for x, p in zip(xs, xy)]
        ys = [p[CHUNK:] for p in xy]
    xs = [x + _dot(x.astype(BF16), block_diag(y)) for x, y in zip(xs, ys)]

    for i, (slot0, load_q, load_k, rows, bcol, rev) in enumerate(quads):
        k_all = load_k().astype(F32)
        e_col = jnp.exp(bcol)
        qkg_all_k = (k_all * e_col).astype(BF16)
        if readout:
            qkg_all_q = (load_q().astype(F32) * e_col).astype(BF16)
            pm_all = qks[i] * dmats[i] * rows[1:2, :]
        k_t = k_all.T
        kdt_all = (k_t * (jnp.exp(rows[2:3, :] - rows[0:1, :]) * rows[1:2, :])).astype(BF16)
        for j in range(GDN_QUAD):
            slot = slot0 + j
            shift = (wide - j * CHUNK) % wide
            take = lambda m: (m if shift == 0 else pltpu.roll(m, shift, 1))[:, :CHUNK]
            tb_scr[slot] = take(xs[i]).astype(BF16)
            qkg_scr[slot, CHUNK:, :] = qkg_all_k[rows_of(j)]
            if readout:
                qkg_scr[slot, :CHUNK, :] = qkg_all_q[rows_of(j)]
                pm_scr[slot] = take(pm_all).astype(BF16)
            kdt_scr[slot] = kdt_all[:, j * CHUNK:(j + 1) * CHUNK]
            tot = rows[2:3, j * CHUNK:(j + 1) * CHUNK]
            dec_scr[slot] = jnp.exp(jnp.concatenate([tot, tot], axis=1))


def _gdn_recur(n, dirs, readout, qkg_scr, tb_scr, pm_scr, kdt_scr, dec_scr):
    def step(c, carry):
        slots = [slot_of(c) for _, slot_of, _, _ in dirs]
        states = [s_ref[...] for s_ref, _, _, _ in dirs]
        if readout:
            qks = [_dot(qkg_scr[slot], s.astype(BF16)) for slot, s in zip(slots, states)]
            kss = [p[CHUNK:] for p in qks]
        else:
            kss = [_dot(qkg_scr[slot, CHUNK:, :], s.astype(BF16)) for slot, s in zip(slots, states)]
        ws = [_dot(tb_scr[slot], (load_v(c) - ks).astype(BF16)).astype(BF16)
              for slot, ks, (_, _, load_v, _) in zip(slots, kss, dirs)]
        for slot, s, w, (s_ref, _, _, _) in zip(slots, states, ws, dirs):
            s_ref[...] = s * dec_scr[slot] + _dot(kdt_scr[slot], w)
        if readout:
            for slot, p, w, (_, _, _, store_o) in zip(slots, qks, ws, dirs):
                store_o(c, p[:CHUNK] + _dot(pm_scr[slot], w))
        return carry

    lax.fori_loop(0, n, step, 0)


def _gdn_body(qf_ref, kf_ref, vf_ref, gf_ref, qb_ref, kb_ref, vb_ref, gb_ref,
              ck_ref, cv_ref, cg_ref, of_ref, ob_ref,
              s_scr, qkg_scr, tb_scr, pm_scr, kdt_scr, dec_scr,
              *, n_chunks, n_ctx_chunks, heads):
    h0 = pl.program_id(1) * heads
    scr = (qkg_scr, tb_scr, pm_scr, kdt_scr, dec_scr)
    ds = lambda c: pl.ds(pl.multiple_of(c * CHUNK, CHUNK), CHUNK)

    def chunk_list(hh, n, q_refs, k_refs, g_refs):
        head = h0 + hh
        wide = GDN_QUAD * CHUNK
        out = []
        for d in (0, 1):
            rows, bcol = _gdn_gates(g_refs[d], d, head)
            for c0 in range(0, n, GDN_QUAD):
                sl = slice(c0 * CHUNK, c0 * CHUNK + wide)
                load_q = None if q_refs is None else (lambda r=q_refs[d], sl=sl: r[0, hh, sl, :])
                load_k = lambda r=k_refs[d], sl=sl: r[0, hh, sl, :]
                out.append(((2 * hh + d) * n + c0, load_q, load_k, rows[:, sl], bcol[sl, :], d == 1))
        return out

    def chains(n, vf, vb, of, ob):
        out = []
        for hh in range(heads):
            out.append((s_scr.at[2 * hh], lambda c, hh=hh: 2 * hh * n + c,
                        lambda c, hh=hh: vf[0, hh, ds(c), :].astype(F32),
                        None if of is None else (lambda c, o, hh=hh: of.__setitem__(
                            (0, hh, ds(c), slice(None)), o.astype(of.dtype)))))
            out.append((s_scr.at[2 * hh + 1], lambda c, hh=hh: (2 * hh + 1) * n + (n - 1 - c),
                        lambda c, hh=hh: vb[0, hh, ds(n - 1 - c), :].astype(F32),
                        None if ob is None else (lambda c, o, hh=hh: ob.__setitem__(
                            (0, hh, ds(n - 1 - c), slice(None)), o.astype(ob.dtype)))))
        return out

    @pl.when(pl.program_id(2) == 0)
    def _():
        s_scr[...] = jnp.zeros_like(s_scr)
        n = n_ctx_chunks

        def ctx_head(hp, carry):
            quads = []
            for i in range(GDN_PREP_HEADS):
                quads += chunk_list(hp * GDN_PREP_HEADS + i, n, None, (ck_ref, ck_ref), (cg_ref, cg_ref))
            _gdn_prepare(quads, False, *scr)
            return carry

        lax.fori_loop(0, heads // GDN_PREP_HEADS, ctx_head, 0)
        _gdn_recur(n, chains(n, cv_ref, cv_ref, None, None), False, *scr)

    n = n_chunks

    def main_head(hp, carry):
        quads = []
        for i in range(GDN_PREP_HEADS):
            quads += chunk_list(hp * GDN_PREP_HEADS + i, n, (qf_ref, qb_ref), (kf_ref, kb_ref), (gf_ref, gb_ref))
        _gdn_prepare(quads, True, *scr)
        return carry

    lax.fori_loop(0, heads // GDN_PREP_HEADS, main_head, 0)
    _gdn_recur(n, chains(n, vf_ref, vb_ref, of_ref, ob_ref), True, *scr)


GDN_HEADS = 8


def _gdn_scan(qkv, gates, qkv_ctx, gates_ctx, tb):
    bsz, _, length, _ = qkv.shape
    lc = qkv_ctx.shape[2]
    nt = length // tb
    nc, ncc = tb // CHUNK, lc // CHUNK
    hb = GDN_HEADS
    slots = 2 * hb * max(nc, ncc)
    blk = (1, hb, tb, HEAD_DIM)
    cblk = (1, hb, lc, HEAD_DIM)
    fwd = lambda off: pl.BlockSpec(blk, lambda b, h, t: (b, off // hb + h, t, 0))
    bwd = lambda off: pl.BlockSpec(blk, lambda b, h, t: (b, off // hb + h, nt - 1 - t, 0))
    ctx = lambda off: pl.BlockSpec(cblk, lambda b, h, t: (b, off // hb + h, 0, 0))
    out_shape = jax.ShapeDtypeStruct((bsz, N_HEADS, length, HEAD_DIM), BF16)
    return pl.pallas_call(
        functools.partial(_gdn_body, n_chunks=nc, n_ctx_chunks=ncc, heads=hb),
        grid=(bsz, N_HEADS // hb, nt),
        in_specs=[fwd(0), fwd(N_HEADS), fwd(2 * N_HEADS),
                  pl.BlockSpec((1, 3, 2 * N_HEADS, tb), lambda b, h, t: (b, 0, 0, t)),
                  bwd(0), bwd(N_HEADS), bwd(2 * N_HEADS),
                  pl.BlockSpec((1, 3, 2 * N_HEADS, tb), lambda b, h, t: (b, 0, 0, nt - 1 - t)),
                  ctx(N_HEADS), ctx(2 * N_HEADS),
                  pl.BlockSpec((1, 3, 2 * N_HEADS, lc), lambda b, h, t: (b, 0, 0, 0))],
        out_specs=[pl.BlockSpec(blk, lambda b, h, t: (b, h, t, 0)),
                   pl.BlockSpec(blk, lambda b, h, t: (b, h, nt - 1 - t, 0))],
        out_shape=[out_shape, out_shape],
        scratch_shapes=[pltpu.VMEM((2 * hb, HEAD_DIM, HEAD_DIM), F32),
                        pltpu.VMEM((slots, 2 * CHUNK, HEAD_DIM), BF16),
                        pltpu.VMEM((slots, CHUNK, CHUNK), BF16),
                        pltpu.VMEM((slots, CHUNK, CHUNK), BF16),
                        pltpu.VMEM((slots, HEAD_DIM, CHUNK), BF16),
                        pltpu.VMEM((slots, 1, HEAD_DIM), F32)],
        compiler_params=pltpu.CompilerParams(
            dimension_semantics=("parallel", "parallel", "arbitrary"),
            vmem_limit_bytes=VMEM_LIMIT),
    )(qkv, qkv, qkv, gates, qkv, qkv, qkv, gates, qkv_ctx, qkv_ctx, gates_ctx)


def _out_body(oaf_ref, oab_ref, obf_ref, obb_ref, za_ref, zb_ref, naw_ref, nbw_ref,
              w_ref, x_ref, gate_ref, fw_ref, o_ref):
    parts = []
    for of_ref, ob_ref, z_ref, nw_ref in ((oaf_ref, oab_ref, za_ref, naw_ref),
                                          (obf_ref, obb_ref, zb_ref, nbw_ref)):
        for h in range(N_HEADS):
            o = of_ref[0, h].astype(F32) + ob_ref[0, h].astype(F32)
            o = o * lax.rsqrt(jnp.mean(o * o, axis=-1, keepdims=True) + NORM_EPS) * nw_ref[h:h + 1, :]
            hz = 0.5 * z_ref[0, h].astype(F32)
            parts.append(((hz + hz * jnp.tanh(hz)) * o).astype(BF16))
    y = jnp.concatenate(parts, axis=1)
    xo = x_ref[0] + gate_ref[0] * _dot(y, w_ref[...])
    ms = jnp.mean(xo * xo, axis=-1, keepdims=True)
    o_ref[0] = xo * lax.rsqrt(ms + NORM_EPS) * fw_ref[...]


def _out_stage(oa_f, oa_b, ob_f, ob_b, yh, na_w, nb_w, w_out, x, gate, final_w, tm):
    bsz, length, d = x.shape
    hblk = (1, N_HEADS, tm, HEAD_DIM)
    ospec = pl.BlockSpec(hblk, lambda b, m: (b, 0, m, 0))
    full2 = lambda a: pl.BlockSpec(a.shape, lambda b, m: (0, 0))
    return pl.pallas_call(
        _out_body,
        grid=(bsz, length // tm),
        in_specs=[ospec, ospec, ospec, ospec,
                  pl.BlockSpec(hblk, lambda b, m: (b, OFF_AZ // N_HEADS, m, 0)),
                  pl.BlockSpec(hblk, lambda b, m: (b, OFF_BZ // N_HEADS, m, 0)),
                  full2(na_w), full2(nb_w), full2(w_out),
                  pl.BlockSpec((1, tm, d), lambda b, m: (b, m, 0)),
                  pl.BlockSpec((1, 1, d), lambda b, m: (b, 0, 0)),
                  full2(final_w)],
        out_specs=pl.BlockSpec((1, tm, d), lambda b, m: (b, m, 0)),
        out_shape=jax.ShapeDtypeStruct((bsz, length, d), F32),
        compiler_params=pltpu.CompilerParams(dimension_semantics=("parallel", "parallel"),
                                             vmem_limit_bytes=VMEM_LIMIT),
    )(oa_f, oa_b, ob_f, ob_b, yh, yh, na_w, nb_w, w_out, x, gate, final_w)


def kernel(x, c, ctx, c_ctx, norm_w, ada_w, ada_b, w_in, conv_w, hg_lb_logits, gdn_a_log,
           gdn_dt_bias, ha_norm_w, hb_norm_w, w_out, final_norm_w):
    bsz, length, d = x.shape
    lc = ctx.shape[1]
    assert d == D_MODEL and length % 512 == 0 and length % GRID_W == 0 and lc % CHUNK == 0
    assert w_in.shape[0] == 1, "single-layer block"

    n_cond = bsz + 1
    cond = jnp.concatenate([c, c_ctx[None, :], jnp.zeros((-n_cond % 8, d), F32)], axis=0)
    mod = _adaln(cond, ada_w[0], ada_b[0])
    shift, scale, gate = mod[:, :d], mod[:, d:2 * d], mod[:, 2 * d:]
    lat = lambda m: m[:bsz, None, :]
    rep_ctx = lambda m: jnp.broadcast_to(m[bsz:bsz + 1, None, :], (bsz, 1, d))

    w_main = w_in[0, :, :N_MAIN].astype(BF16)
    w_gate_t = w_in[0, :, N_MAIN:].T.astype(BF16)
    nw = norm_w[0].reshape(1, d)
    yh, gates = _inproj(x, nw, lat(scale), lat(shift), w_main, w_gate_t, tm=min(2048, length))
    yh_c, gates_c = _inproj(ctx, nw, rep_ctx(scale), rep_ctx(shift), w_main, w_gate_t, tm=lc)

    lb = jax.nn.softmax(hg_lb_logits.astype(F32), axis=0)[0]
    lb = lb.reshape(2, N_HEADS, HEAD_DIM).transpose(1, 0, 2)
    tb = min(512, length)
    oa_f, oa_b = _hgrn2_scan(yh, yh_c, lb, tb)

    cw = conv_w[0].reshape(9, 3 * N_HEADS, HEAD_DIM).transpose(1, 0, 2)
    qkv = _gdn_conv(yh, cw, two_d=True)
    qkv_c = _gdn_conv(yh_c, cw, two_d=False)
    params = jnp.stack([gdn_a_log[0].reshape(-1), gdn_dt_bias[0].reshape(-1)], axis=1).astype(F32)
    ob_f, ob_b = _gdn_scan(qkv, _gate_rows(gates, params), qkv_c, _gate_rows(gates_c, params), tb)

    return _out_stage(oa_f, oa_b, ob_f, ob_b, yh, ha_norm_w[0], hb_norm_w[0],
                      w_out[0].astype(BF16), x, lat(gate), final_norm_w.reshape(1, d),
                      tm=min(512, length))
```

```python
import functools

import jax
import jax.numpy as jnp
import numpy as np
from jax import lax
from jax.experimental import pallas as pl
from jax.experimental.pallas import tpu as pltpu

F32 = jnp.float32
BF16 = jnp.bfloat16

D_MODEL = 1024
N_HEADS = 8
HEAD_DIM = 128
CHUNK = 64
GRID_W = 64
NORM_EPS = 1e-6
N_MAIN = 72 * HEAD_DIM
N_GATE = 4 * N_HEADS
OFF_AQ, OFF_AFF, OFF_AFB, OFF_AI, OFF_AZ, OFF_BQ, OFF_BZ = 0, 8, 16, 24, 32, 40, 64
NEG_BIG = -1e30
VMEM_LIMIT = 56 * 1024 * 1024


def _dot(a, b):
    return jnp.dot(a, b, preferred_element_type=F32)


def _dot_nt(a, b):
    return lax.dot_general(a, b, (((1,), (1,)), ((), ())), preferred_element_type=F32)


def _sigmoid(x):
    return 1.0 / (1.0 + jnp.exp(-x))


def _silu(x):
    return x * _sigmoid(x)


def _adaln_body(c_ref, w_ref, b_ref, o_ref):
    o_ref[...] = _dot(_silu(c_ref[...]), w_ref[...]) + b_ref[...]


def _adaln(cond, ada_w, ada_b):
    rows, d = cond.shape
    n = ada_w.shape[1]
    tn = 1024
    return pl.pallas_call(
        _adaln_body,
        grid=(n // tn,),
        in_specs=[pl.BlockSpec((rows, d), lambda j: (0, 0)),
                  pl.BlockSpec((d, tn), lambda j: (0, j)),
                  pl.BlockSpec((1, tn), lambda j: (0, j))],
        out_specs=pl.BlockSpec((rows, tn), lambda j: (0, j)),
        out_shape=jax.ShapeDtypeStruct((rows, n), F32),
        compiler_params=pltpu.CompilerParams(dimension_semantics=("arbitrary",),
                                             vmem_limit_bytes=VMEM_LIMIT),
    )(cond, ada_w, ada_b.reshape(1, n))


def _inproj_body(x_ref, nw_ref, sc_ref, sh_ref, w_ref, wg_ref, y_ref, yg_ref, h_scr, *, tn):
    @pl.when(pl.program_id(2) == 0)
    def _():
        x = x_ref[0]
        ms = jnp.mean(x * x, axis=-1, keepdims=True)
        h = x * lax.rsqrt(ms + NORM_EPS) * nw_ref[...]
        h = (h * (1.0 + sc_ref[0]) + sh_ref[0]).astype(BF16)
        h_scr[...] = h
        yg_ref[0] = _dot_nt(wg_ref[...], h)

    acc = _dot(h_scr[...], w_ref[...])
    for j in range(tn // HEAD_DIM):
        y_ref[0, j] = acc[:, j * HEAD_DIM:(j + 1) * HEAD_DIM].astype(y_ref.dtype)


def _inproj(x, norm_w, scale, shift, w_main, w_gate_t, tm):
    bsz, length, d = x.shape
    tn = 1024
    grid = (bsz, length // tm, N_MAIN // tn)
    return pl.pallas_call(
        functools.partial(_inproj_body, tn=tn),
        grid=grid,
        in_specs=[pl.BlockSpec((1, tm, d), lambda b, m, n: (b, m, 0)),
                  pl.BlockSpec((1, d), lambda b, m, n: (0, 0)),
                  pl.BlockSpec((1, 1, d), lambda b, m, n: (b, 0, 0)),
                  pl.BlockSpec((1, 1, d), lambda b, m, n: (b, 0, 0)),
                  pl.BlockSpec((d, tn), lambda b, m, n: (0, n)),
                  pl.BlockSpec((N_GATE, d), lambda b, m, n: (0, 0))],
        out_specs=[pl.BlockSpec((1, tn // HEAD_DIM, tm, HEAD_DIM), lambda b, m, n: (b, n, m, 0)),
                   pl.BlockSpec((1, N_GATE, tm), lambda b, m, n: (b, 0, m))],
        out_shape=[jax.ShapeDtypeStruct((bsz, N_MAIN // HEAD_DIM, length, HEAD_DIM), BF16),
                   jax.ShapeDtypeStruct((bsz, N_GATE, length), F32)],
        scratch_shapes=[pltpu.VMEM((tm, d), BF16)],
        compiler_params=pltpu.CompilerParams(
            dimension_semantics=("parallel", "parallel", "arbitrary"),
            vmem_limit_bytes=VMEM_LIMIT),
    )(x, norm_w, scale, shift, w_main, w_gate_t)


CONV_PAD = 72
CONV_HALO = 8


def _conv_body(x_ref, w_ref, o_ref, pad_scr, *, length, width, two_d, rt):
    blk = pl.program_id(1)
    zeros = jnp.zeros((CONV_PAD, HEAD_DIM), F32)
    pad_scr[0:CONV_PAD, :] = zeros
    pad_scr[CONV_PAD + length:CONV_PAD + length + CONV_PAD, :] = zeros
    pad_scr[CONV_PAD:CONV_PAD + length, :] = x_ref[0, 0].astype(F32)
    w = 0.5 * w_ref[0]
    win_rows = rt + 2 * CONV_HALO
    is_q = blk < N_HEADS
    is_qk = blk < 2 * N_HEADS
    col = lax.broadcasted_iota(jnp.int32, (rt, HEAD_DIM), 0) & (width - 1)

    def tile(i, carry):
        s = pl.multiple_of(i * rt, rt)
        sums = [None, None, None]
        for dr in ((-1, 0, 1) if two_d else (0,)):
            base = pl.multiple_of(s + (CONV_PAD + dr * width - CONV_HALO), 8)
            win = pad_scr[pl.ds(base, win_rows), :]
            for j in range(3):
                tap = (dr + 1) * 3 + j
                term = win * w[tap:tap + 1, :]
                sums[j] = term if sums[j] is None else sums[j] + term
        inner = slice(CONV_HALO, CONV_HALO + rt)
        left = pltpu.roll(sums[0], 1, 0)[inner]
        right = pltpu.roll(sums[2], win_rows - 1, 0)[inner]
        hs = sums[1][inner] + jnp.where(col >= 1, left, 0.0) + jnp.where(col <= width - 2, right, 0.0)
        a = hs + hs * jnp.tanh(hs)
        nrm = lax.rsqrt(jnp.sum(a * a, axis=-1, keepdims=True) + NORM_EPS)
        f = jnp.where(is_q, nrm * HEAD_DIM ** -0.5, jnp.where(is_qk, nrm, 1.0))
        o_ref[0, 0, pl.ds(s, rt), :] = (a * f).astype(o_ref.dtype)
        return carry

    lax.fori_loop(0, length // rt, tile, 0)


def _gdn_conv(yh, conv_w, two_d):
    bsz, _, length, _ = yh.shape
    width = GRID_W if two_d else length
    assert width & (width - 1) == 0, "grid width must be a power of two (column index by bit mask)"
    rt = min(512, length)
    assert rt % width == 0
    nblk = 3 * N_HEADS
    return pl.pallas_call(
        functools.partial(_conv_body, length=length, width=width, two_d=two_d, rt=rt),
        grid=(bsz, nblk),
        in_specs=[pl.BlockSpec((1, 1, length, HEAD_DIM), lambda b, j: (b, OFF_BQ + j, 0, 0)),
                  pl.BlockSpec((1, 9, HEAD_DIM), lambda b, j: (j, 0, 0))],
        out_specs=pl.BlockSpec((1, 1, length, HEAD_DIM), lambda b, j: (b, j, 0, 0)),
        out_shape=jax.ShapeDtypeStruct((bsz, nblk, length, HEAD_DIM), BF16),
        scratch_shapes=[pltpu.VMEM((length + 2 * CONV_PAD, HEAD_DIM), F32)],
        compiler_params=pltpu.CompilerParams(dimension_semantics=("parallel", "parallel"),
                                             vmem_limit_bytes=VMEM_LIMIT),
    )(yh, conv_w)


HG_LEVELS = (32, 16, 8, 4, 2, 1)
HG_PAIRS = ((0, 1), (2, 3), (4, 5))
HG_GROUP = 4
LOG2E = 1.4426950408889634


def _hg_constants():
    t = np.arange(CHUNK)[:, None]
    col = np.arange(HEAD_DIM)[None, :]
    s = col % CHUNK
    out = [np.broadcast_to(np.where((t & m) != 0, 1.0, -1.0), (CHUNK, HEAD_DIM)) for m in HG_LEVELS]
    for rev in (False, True):
        for pa, pb in HG_PAIRS:
            keep = np.zeros((CHUNK, HEAD_DIM), bool)
            for half, lv in ((col < CHUNK, pa), (col >= CHUNK, pb)):
                m = HG_LEVELS[lv]
                split = ((t ^ s) >> (m.bit_length() - 1)) == 1
                t_hi = (t & m) != 0
                keep |= half & split & (~t_hi if rev else t_hi)
            out.append(keep)
    out.append((col == t) & (col < CHUNK))
    return jnp.asarray(np.stack([np.asarray(o, np.float32) for o in out]))


def _hg_triangles():
    t = np.arange(CHUNK)
    lower = (t[None, :] <= t[:, None]).astype(np.float32)
    return jnp.asarray(np.stack([np.tile(lower, (1, 3)), np.tile(lower.T, (1, 3))]), BF16)


def _cumsum_rows(g, tri3):
    hi = g.astype(BF16)
    r1 = g - hi.astype(F32)
    mid = r1.astype(BF16)
    lo = (r1 - mid.astype(F32)).astype(BF16)
    return _dot(tri3, jnp.concatenate([hi, mid, lo], axis=0))


def _level_operand(b, b_ref, q, k, m, rev):
    parts = []
    for blk in range(CHUNK // m):
        rows = slice(blk * m, (blk + 1) * m)
        r = (blk // 2) * 2 * m + m
        ref = jnp.broadcast_to(b_ref[r:r + 1, :], (m, HEAD_DIM))
        q_side = (blk % 2 == 1) != rev
        e = (b[rows] - ref) if q_side else (ref - b[rows])
        parts.append(((q if q_side else k)[rows] * jnp.exp2(e)).astype(BF16))
    return jnp.concatenate(parts, axis=0)


def _level_ref(b_ref, m, sub):
    bc = lambda r, n: jnp.broadcast_to(b_ref[r:r + 1, :], (n, HEAD_DIM))
    if m >= 4:
        return jnp.concatenate([bc(blk * 2 * m + m, 2 * m) for blk in range(CHUNK // (2 * m))], axis=0)
    lo = jnp.concatenate([bc(v * 8 + 2, 8) for v in range(CHUNK // 8)], axis=0)
    hi = jnp.concatenate([bc(v * 8 + 6, 8) for v in range(CHUNK // 8)], axis=0)
    return jnp.where(sub < 4, lo, hi)


def _hg_prepare(items, readout, cst_ref, tri_ref, b_scr, sc_scr, vv_scr, u_scr, dec_scr):
    row = lax.broadcasted_iota(jnp.int32, (CHUNK, HEAD_DIM), 0)
    sub = row & 7
    even = (row & 1) == 0
    gs, ks, bs = [], [], []
    for i, (_, _, load_f, _, lb, rev) in enumerate(items):
        c1 = 0.5 - 0.5 * lb
        p = c1 * jnp.tanh(0.5 * load_f())
        g = jnp.log2((0.5 + 0.5 * lb) + p)
        gs.append(g)
        ks.append(c1 - p)
        b = _cumsum_rows(g, tri_ref[1 if rev else 0])
        bs.append(b)
        b_scr[i] = b
    for i, (slot, _, _, load_v, _, rev) in enumerate(items):
        b = bs[i]
        btot = b[0:1, :] if rev else b[CHUNK - 1:CHUNK, :]
        v = load_v()
        vv_t = jnp.concatenate([v, v], axis=0).T.astype(BF16)
        k_dec = (ks[i] * jnp.exp2(btot - b)).astype(BF16)
        u_scr[slot] = _dot(vv_t[:, :CHUNK], k_dec)
        dec_scr[slot] = jnp.exp2(btot)
        vv_scr[slot] = vv_t
    if not readout:
        return

    qs = []
    for _, load_q, _, _, _, _ in items:
        q_raw = load_q()
        hs = (0.5 * HEAD_DIM ** -0.5) * q_raw
        qs.append(hs + hs * jnp.tanh(0.5 * q_raw))
    scores = [jnp.zeros((CHUNK, HEAD_DIM), F32) for _ in items]
    zero_blk = jnp.zeros((CHUNK, HEAD_DIM), BF16)
    for pi, pair in enumerate(HG_PAIRS):
        gps = []
        for i, (_, _, _, _, _, rev) in enumerate(items):
            a_mats = []
            for lv in pair:
                m = HG_LEVELS[lv]
                if m >= 8:
                    a_mats.append(_level_operand(bs[i], b_scr.at[i], qs[i], ks[i], m, rev))
                    continue
                sign = cst_ref[lv]
                if m == 1:
                    x = jnp.exp2(jnp.where(even, gs[i] if rev else pltpu.roll(gs[i], CHUNK - 1, 0), 0.0))
                else:
                    d = bs[i] - _level_ref(b_scr.at[i], m, sub)
                    x = jnp.exp2((-d if rev else d) * sign)
                t_is_q = (sign < 0.0) if rev else (sign > 0.0)
                a_mats.append((jnp.where(t_is_q, qs[i], ks[i]) * x).astype(BF16))
            lhs = jnp.concatenate(a_mats, axis=1)
            rhs = jnp.concatenate([jnp.concatenate([a_mats[0], zero_blk], axis=1),
                                   jnp.concatenate([zero_blk, a_mats[1]], axis=1)], axis=0)
            gps.append(_dot_nt(lhs, rhs))
        for i, (_, _, _, _, _, rev) in enumerate(items):
            scores[i] = scores[i] + gps[i] * cst_ref[6 + (3 if rev else 0) + pi]
    for i, (slot, _, _, _, _, _) in enumerate(items):
        diag = jnp.sum(qs[i] * ks[i], axis=-1, keepdims=True) * cst_ref[12]
        sc_scr[slot, :, :HEAD_DIM] = (scores[i] + diag).astype(BF16)
        sc_scr[slot, :, HEAD_DIM:] = (qs[i] * jnp.exp2(bs[i])).astype(BF16)


def _hg_recur(steps, dirs, readout, sc_scr, vv_scr, u_scr, dec_scr):
    states = [s_scr[...] for s_scr, _, _ in dirs]
    for c in steps:
        for i, (_, slot_of, store_o) in enumerate(dirs):
            slot = slot_of(c)
            if readout:
                rhs = jnp.concatenate([vv_scr[slot], states[i].astype(BF16)], axis=1)
                store_o(c, _dot_nt(sc_scr[slot], rhs))
            states[i] = states[i] * dec_scr[slot] + u_scr[slot]
    for (s_scr, _, _), st in zip(dirs, states):
        s_scr[...] = st


def _hg_body(cst_ref, tri_ref, qf_ref, ff_ref, vf_ref, qb_ref, fb_ref, vb_ref,
             cff_ref, cfb_ref, cv_ref, lb_ref, of_ref, ob_ref,
             sf_scr, sb_scr, b_scr, sc_scr, vv_scr, u_scr, dec_scr, *, n_chunks, n_ctx_chunks):
    lb_f = lb_ref[0, 0:1, :]
    lb_b = lb_ref[0, 1:2, :]
    scr = (sc_scr, vv_scr, u_scr, dec_scr)
    load = lambda ref, rows: (lambda: ref[0, 0, rows, :].astype(F32))

    @pl.when(pl.program_id(2) == 0)
    def _():
        sf_scr[...] = jnp.zeros_like(sf_scr)
        sb_scr[...] = jnp.zeros_like(sb_scr)
        n = n_ctx_chunks
        for c0 in range(0, n, HG_GROUP):
            items = []
            for c in range(c0, min(c0 + HG_GROUP, n)):
                sl = slice(c * CHUNK, (c + 1) * CHUNK)
                items.append((c, None, load(cff_ref, sl), load(cv_ref, sl), lb_f, False))
                items.append((n + c, None, load(cfb_ref, sl), load(cv_ref, sl), lb_b, True))
            _hg_prepare(items, False, cst_ref, tri_ref, b_scr, *scr)
        _hg_recur(range(n), [(sf_scr, lambda c: c, None), (sb_scr, lambda c: 2 * n - 1 - c, None)],
                  False, *scr)

    n = n_chunks
    rows = lambda c: slice(c * CHUNK, (c + 1) * CHUNK)

    def prepare(g):
        items = []
        for p in range(g * HG_GROUP, (g + 1) * HG_GROUP):
            rf, rb = rows(p), rows(n - 1 - p)
            items.append((p, load(qf_ref, rf), load(ff_ref, rf), load(vf_ref, rf), lb_f, False))
            items.append((2 * n - 1 - p, load(qb_ref, rb), load(fb_ref, rb), load(vb_ref, rb), lb_b, True))
        _hg_prepare(items, True, cst_ref, tri_ref, b_scr, *scr)

    def store_f(c, o):
        of_ref[0, 0, rows(c), :] = o.astype(of_ref.dtype)

    def store_b(c, o):
        ob_ref[0, 0, rows(n - 1 - c), :] = o.astype(ob_ref.dtype)

    def recur(g):
        _hg_recur(range(g * HG_GROUP, (g + 1) * HG_GROUP),
                  [(sf_scr, lambda c: c, store_f), (sb_scr, lambda c: 2 * n - 1 - c, store_b)], True, *scr)

    groups = n // HG_GROUP
    prepare(0)
    for g in range(1, groups):
        recur(g - 1)
        prepare(g)
    recur(groups - 1)


def _hgrn2_scan(yh, yh_ctx, lb, tb):
    bsz, _, length, _ = yh.shape
    lc = yh_ctx.shape[2]
    nt = length // tb
    nc, ncc = tb // CHUNK, lc // CHUNK
    assert nc % HG_GROUP == 0
    slots = 2 * max(nc, ncc)
    blk = (1, 1, tb, HEAD_DIM)
    cblk = (1, 1, lc, HEAD_DIM)
    fwd = lambda off: pl.BlockSpec(blk, lambda b, h, t: (b, off + h, t, 0))
    bwd = lambda off: pl.BlockSpec(blk, lambda b, h, t: (b, off + h, nt - 1 - t, 0))
    ctx = lambda off: pl.BlockSpec(cblk, lambda b, h, t: (b, off + h, 0, 0))
    consts = _hg_constants()
    tri3 = _hg_triangles()
    out_shape = jax.ShapeDtypeStruct((bsz, N_HEADS, length, HEAD_DIM), BF16)
    return pl.pallas_call(
        functools.partial(_hg_body, n_chunks=nc, n_ctx_chunks=ncc),
        grid=(bsz, N_HEADS, nt),
        in_specs=[pl.BlockSpec(consts.shape, lambda b, h, t: (0, 0, 0)),
                  pl.BlockSpec(tri3.shape, lambda b, h, t: (0, 0, 0)),
                  fwd(OFF_AQ), fwd(OFF_AFF), fwd(OFF_AI), bwd(OFF_AQ), bwd(OFF_AFB), bwd(OFF_AI),
                  ctx(OFF_AFF), ctx(OFF_AFB), ctx(OFF_AI),
                  pl.BlockSpec((1, 2, HEAD_DIM), lambda b, h, t: (h, 0, 0))],
        out_specs=[pl.BlockSpec(blk, lambda b, h, t: (b, h, t, 0)),
                   pl.BlockSpec(blk, lambda b, h, t: (b, h, nt - 1 - t, 0))],
        out_shape=[out_shape, out_shape],
        scratch_shapes=[pltpu.VMEM((HEAD_DIM, HEAD_DIM), F32), pltpu.VMEM((HEAD_DIM, HEAD_DIM), F32),
                        pltpu.VMEM((2 * HG_GROUP, CHUNK, HEAD_DIM), F32),
                        pltpu.VMEM((slots, CHUNK, 2 * HEAD_DIM), BF16),
                        pltpu.VMEM((slots, HEAD_DIM, HEAD_DIM), BF16),
                        pltpu.VMEM((slots, HEAD_DIM, HEAD_DIM), F32),
                        pltpu.VMEM((slots, 1, HEAD_DIM), F32)],
        compiler_params=pltpu.CompilerParams(
            dimension_semantics=("parallel", "parallel", "arbitrary"),
            vmem_limit_bytes=VMEM_LIMIT),
    )(consts, tri3, yh, yh, yh, yh, yh, yh, yh_ctx, yh_ctx, yh_ctx, lb)


def _seg_cumsum_lanes(x, rev, lane):
    total = x.shape[1]
    seg = lane & (CHUNK - 1)
    for sh in (1, 2, 4, 8, 16, 32):
        if rev:
            x = x + jnp.where(seg < CHUNK - sh, pltpu.roll(x, total - sh, 1), 0.0)
        else:
            x = x + jnp.where(seg >= sh, pltpu.roll(x, sh, 1), 0.0)
    return x


def _gate_rows_body(g_ref, p_ref, o_ref):
    nrow = 2 * N_HEADS
    total = g_ref.shape[2]
    lane = lax.broadcasted_iota(jnp.int32, (nrow, total), 1)
    row = lax.broadcasted_iota(jnp.int32, (nrow, total), 0)
    z = g_ref[0, :nrow, :] + p_ref[:, 1:2]
    softplus = jnp.maximum(z, 0.0) + jnp.log1p(jnp.exp(-jnp.abs(z)))
    g = -jnp.exp(p_ref[:, 0:1]) * softplus
    prefix = _seg_cumsum_lanes(g, False, lane)
    suffix = _seg_cumsum_lanes(g, True, lane)
    o_ref[0, 0] = jnp.where(row < N_HEADS, prefix, suffix)
    o_ref[0, 1] = _sigmoid(g_ref[0, nrow:, :])
    o_ref[0, 2] = prefix + suffix - g


def _gate_rows(gates, params):
    bsz, _, length = gates.shape
    tg = min(2048, length)
    nrow = 2 * N_HEADS
    return pl.pallas_call(
        _gate_rows_body,
        grid=(bsz, length // tg),
        in_specs=[pl.BlockSpec((1, N_GATE, tg), lambda b, t: (b, 0, t)),
                  pl.BlockSpec(params.shape, lambda b, t: (0, 0))],
        out_specs=pl.BlockSpec((1, 3, nrow, tg), lambda b, t: (b, 0, 0, t)),
        out_shape=jax.ShapeDtypeStruct((bsz, 3, nrow, length), F32),
        compiler_params=pltpu.CompilerParams(dimension_semantics=("parallel", "parallel"),
                                             vmem_limit_bytes=VMEM_LIMIT),
    )(gates, params)


def _gdn_gates(g_ref, d, head):
    r = pl.ds(d * N_HEADS + head, 1)
    b = g_ref[0, 0, r, :]
    rows = jnp.concatenate([b, g_ref[0, 1, r, :], g_ref[0, 2, r, :]], axis=0)
    return rows, jnp.broadcast_to(b, (HEAD_DIM, b.shape[1])).T


GDN_QUAD = 4
GDN_PREP_HEADS = 4


def _gdn_prepare(quads, readout, qkg_scr, tb_scr, pm_scr, kdt_scr, dec_scr):
    wide = GDN_QUAD * CHUNK
    t_i = lax.broadcasted_iota(jnp.int32, (CHUNK, wide), 0)
    lane = lax.broadcasted_iota(jnp.int32, (CHUNK, wide), 1)
    s_i = lane & (CHUNK - 1)
    blk = [(lane >= j * CHUNK) & (lane < (j + 1) * CHUNK) for j in range(GDN_QUAD)]
    eye = (t_i == s_i).astype(F32)
    tri = {False: (s_i <= t_i, s_i < t_i), True: (s_i >= t_i, s_i > t_i)}
    rows_of = lambda j: slice(j * CHUNK, (j + 1) * CHUNK)

    def diag_blocks(m, off):
        out = jnp.where(blk[0], m[off:off + CHUNK], 0.0)
        for j in range(1, GDN_QUAD):
            out = jnp.where(blk[j], m[off + j * CHUNK:off + (j + 1) * CHUNK], out)
        return out

    def block_diag(y):
        return jnp.concatenate([jnp.where(blk[j], y, 0.0).astype(BF16) for j in range(GDN_QUAD)], axis=0)

    dmats, qks, xs, ys = [], [], [], []
    for _, load_q, load_k, rows, bcol, rev in quads:
        incl, strict = tri[rev]
        bc = jnp.concatenate([bcol, bcol], axis=1)
        dmat = jnp.exp(jnp.where(incl, diag_blocks(bc, 0) - rows[0:1, :], NEG_BIG))
        kb = load_k()
        if readout:
            qkk = _dot_nt(jnp.concatenate([load_q(), kb], axis=0), kb)
            qks.append(diag_blocks(qkk, 0))
            kk = diag_blocks(qkk, wide)
        else:
            kk = diag_blocks(_dot_nt(kb, kb), 0)
        n_mat = jnp.where(strict, kk * dmat, 0.0) * rows[1:2, :]
        dmats.append(dmat)
        xs.append(eye - n_mat)
        ys.append(n_mat)
    ys = [_dot(y.astype(BF16), block_diag(y)) for y in ys]
    for _ in range(4):
        xy = [_dot(jnp.concatenate([x, y], axis=0).astype(BF16), block_diag(y)) for x, y in zip(xs, ys)]
        xs = [x + p[:CHUNK] for x, p in zip(xs, xy)]
        ys = [p[CHUNK:] for p in xy]
    xs = [x + _dot(x.astype(BF16), block_diag(y)) for x, y in zip(xs, ys)]

    for i, (slot0, load_q, load_k, rows, bcol, rev) in enumerate(quads):
        k_all = load_k().astype(F32)
        e_col = jnp.exp(bcol)
        qkg_all_k = (k_all * e_col).astype(BF16)
        if readout:
            qkg_all_q = (load_q().astype(F32) * e_col).astype(BF16)
            pm_all = qks[i] * dmats[i] * rows[1:2, :]
        k_t = k_all.T
        kdt_all = (k_t * (jnp.exp(rows[2:3, :] - rows[0:1, :]) * rows[1:2, :])).astype(BF16)
        for j in range(GDN_QUAD):
            slot = slot0 + j
            shift = (wide - j * CHUNK) % wide
            take = lambda m: (m if shift == 0 else pltpu.roll(m, shift, 1))[:, :CHUNK]
            tb_scr[slot] = take(xs[i]).astype(BF16)
            qkg_scr[slot, CHUNK:, :] = qkg_all_k[rows_of(j)]
            if readout:
                qkg_scr[slot, :CHUNK, :] = qkg_all_q[rows_of(j)]
                pm_scr[slot] = take(pm_all).astype(BF16)
            kdt_scr[slot] = kdt_all[:, j * CHUNK:(j + 1) * CHUNK]
            tot = rows[2:3, j * CHUNK:(j + 1) * CHUNK]
            dec_scr[slot] = jnp.exp(jnp.concatenate([tot, tot], axis=1))


def _gdn_recur(n, dirs, readout, qkg_scr, tb_scr, pm_scr, kdt_scr, dec_scr):
    def step(c, carry):
        slots = [slot_of(c) for _, slot_of, _, _ in dirs]
        states = [s_ref[...] for s_ref, _, _, _ in dirs]
        if readout:
            qks = [_dot(qkg_scr[slot], s.astype(BF16)) for slot, s in zip(slots, states)]
            kss = [p[CHUNK:] for p in qks]
        else:
            kss = [_dot(qkg_scr[slot, CHUNK:, :], s.astype(BF16)) for slot, s in zip(slots, states)]
        ws = [_dot(tb_scr[slot], (load_v(c) - ks).astype(BF16)).astype(BF16)
              for slot, ks, (_, _, load_v, _) in zip(slots, kss, dirs)]
        for slot, s, w, (s_ref, _, _, _) in zip(slots, states, ws, dirs):
            s_ref[...] = s * dec_scr[slot] + _dot(kdt_scr[slot], w)
        if readout:
            for slot, p, w, (_, _, _, store_o) in zip(slots, qks, ws, dirs):
                store_o(c, p[:CHUNK] + _dot(pm_scr[slot], w))
        return carry

    lax.fori_loop(0, n, step, 0)


def _gdn_body(qf_ref, kf_ref, vf_ref, gf_ref, qb_ref, kb_ref, vb_ref, gb_ref,
              ck_ref, cv_ref, cg_ref, of_ref, ob_ref,
              s_scr, qkg_scr, tb_scr, pm_scr, kdt_scr, dec_scr,
              *, n_chunks, n_ctx_chunks, heads):
    h0 = pl.program_id(1) * heads
    scr = (qkg_scr, tb_scr, pm_scr, kdt_scr, dec_scr)
    ds = lambda c: pl.ds(pl.multiple_of(c * CHUNK, CHUNK), CHUNK)

    def chunk_list(hh, n, q_refs, k_refs, g_refs):
        head = h0 + hh
        wide = GDN_QUAD * CHUNK
        out = []
        for d in (0, 1):
            rows, bcol = _gdn_gates(g_refs[d], d, head)
            for c0 in range(0, n, GDN_QUAD):
                sl = slice(c0 * CHUNK, c0 * CHUNK + wide)
                load_q = None if q_refs is None else (lambda r=q_refs[d], sl=sl: r[0, hh, sl, :])
                load_k = lambda r=k_refs[d], sl=sl: r[0, hh, sl, :]
                out.append(((2 * hh + d) * n + c0, load_q, load_k, rows[:, sl], bcol[sl, :], d == 1))
        return out

    def chains(n, vf, vb, of, ob):
        out = []
        for hh in range(heads):
            out.append((s_scr.at[2 * hh], lambda c, hh=hh: 2 * hh * n + c,
                        lambda c, hh=hh: vf[0, hh, ds(c), :].astype(F32),
                        None if of is None else (lambda c, o, hh=hh: of.__setitem__(
                            (0, hh, ds(c), slice(None)), o.astype(of.dtype)))))
            out.append((s_scr.at[2 * hh + 1], lambda c, hh=hh: (2 * hh + 1) * n + (n - 1 - c),
                        lambda c, hh=hh: vb[0, hh, ds(n - 1 - c), :].astype(F32),
                        None if ob is None else (lambda c, o, hh=hh: ob.__setitem__(
                            (0, hh, ds(n - 1 - c), slice(None)), o.astype(ob.dtype)))))
        return out

    @pl.when(pl.program_id(2) == 0)
    def _():
        s_scr[...] = jnp.zeros_like(s_scr)
        n = n_ctx_chunks

        def ctx_head(hp, carry):
            quads = []
            for i in range(GDN_PREP_HEADS):
                quads += chunk_list(hp * GDN_PREP_HEADS + i, n, None, (ck_ref, ck_ref), (cg_ref, cg_ref))
            _gdn_prepare(quads, False, *scr)
            return carry

        lax.fori_loop(0, heads // GDN_PREP_HEADS, ctx_head, 0)
        _gdn_recur(n, chains(n, cv_ref, cv_ref, None, None), False, *scr)

    n = n_chunks

    def main_head(hp, carry):
        quads = []
        for i in range(GDN_PREP_HEADS):
            quads += chunk_list(hp * GDN_PREP_HEADS + i, n, (qf_ref, qb_ref), (kf_ref, kb_ref), (gf_ref, gb_ref))
        _gdn_prepare(quads, True, *scr)
        return carry

    lax.fori_loop(0, heads // GDN_PREP_HEADS, main_head, 0)
    _gdn_recur(n, chains(n, vf_ref, vb_ref, of_ref, ob_ref), True, *scr)


GDN_HEADS = 8


def _gdn_scan(qkv, gates, qkv_ctx, gates_ctx, tb):
    bsz, _, length, _ = qkv.shape
    lc = qkv_ctx.shape[2]
    nt = length // tb
    nc, ncc = tb // CHUNK, lc // CHUNK
    hb = GDN_HEADS
    assert nc % GDN_QUAD == 0 and ncc % GDN_QUAD == 0 and hb % GDN_PREP_HEADS == 0
    slots = 2 * hb * max(nc, ncc)
    blk = (1, hb, tb, HEAD_DIM)
    cblk = (1, hb, lc, HEAD_DIM)
    fwd = lambda off: pl.BlockSpec(blk, lambda b, h, t: (b, off // hb + h, t, 0))
    bwd = lambda off: pl.BlockSpec(blk, lambda b, h, t: (b, off // hb + h, nt - 1 - t, 0))
    ctx = lambda off: pl.BlockSpec(cblk, lambda b, h, t: (b, off // hb + h, 0, 0))
    out_shape = jax.ShapeDtypeStruct((bsz, N_HEADS, length, HEAD_DIM), BF16)
    return pl.pallas_call(
        functools.partial(_gdn_body, n_chunks=nc, n_ctx_chunks=ncc, heads=hb),
        grid=(bsz, N_HEADS // hb, nt),
        in_specs=[fwd(0), fwd(N_HEADS), fwd(2 * N_HEADS),
                  pl.BlockSpec((1, 3, 2 * N_HEADS, tb), lambda b, h, t: (b, 0, 0, t)),
                  bwd(0), bwd(N_HEADS), bwd(2 * N_HEADS),
                  pl.BlockSpec((1, 3, 2 * N_HEADS, tb), lambda b, h, t: (b, 0, 0, nt - 1 - t)),
                  ctx(N_HEADS), ctx(2 * N_HEADS),
                  pl.BlockSpec((1, 3, 2 * N_HEADS, lc), lambda b, h, t: (b, 0, 0, 0))],
        out_specs=[pl.BlockSpec(blk, lambda b, h, t: (b, h, t, 0)),
                   pl.BlockSpec(blk, lambda b, h, t: (b, h, nt - 1 - t, 0))],
        out_shape=[out_shape, out_shape],
        scratch_shapes=[pltpu.VMEM((2 * hb, HEAD_DIM, HEAD_DIM), F32),
                        pltpu.VMEM((slots, 2 * CHUNK, HEAD_DIM), BF16),
                        pltpu.VMEM((slots, CHUNK, CHUNK), BF16),
                        pltpu.VMEM((slots, CHUNK, CHUNK), BF16),
                        pltpu.VMEM((slots, HEAD_DIM, CHUNK), BF16),
                        pltpu.VMEM((slots, 1, HEAD_DIM), F32)],
        compiler_params=pltpu.CompilerParams(
            dimension_semantics=("parallel", "parallel", "arbitrary"),
            vmem_limit_bytes=VMEM_LIMIT),
    )(qkv, qkv, qkv, gates, qkv, qkv, qkv, gates, qkv_ctx, qkv_ctx, gates_ctx)


def _out_body(oaf_ref, oab_ref, obf_ref, obb_ref, za_ref, zb_ref, naw_ref, nbw_ref,
              w_ref, x_ref, gate_ref, fw_ref, o_ref):
    parts = []
    for of_ref, ob_ref, z_ref, nw_ref in ((oaf_ref, oab_ref, za_ref, naw_ref),
                                          (obf_ref, obb_ref, zb_ref, nbw_ref)):
        for h in range(N_HEADS):
            o = of_ref[0, h].astype(F32) + ob_ref[0, h].astype(F32)
            o = o * lax.rsqrt(jnp.mean(o * o, axis=-1, keepdims=True) + NORM_EPS) * nw_ref[h:h + 1, :]
            hz = 0.5 * z_ref[0, h].astype(F32)
            parts.append(((hz + hz * jnp.tanh(hz)) * o).astype(BF16))
    y = jnp.concatenate(parts, axis=1)
    xo = x_ref[0] + gate_ref[0] * _dot(y, w_ref[...])
    ms = jnp.mean(xo * xo, axis=-1, keepdims=True)
    o_ref[0] = xo * lax.rsqrt(ms + NORM_EPS) * fw_ref[...]


def _out_stage(oa_f, oa_b, ob_f, ob_b, yh, na_w, nb_w, w_out, x, gate, final_w, tm):
    bsz, length, d = x.shape
    hblk = (1, N_HEADS, tm, HEAD_DIM)
    ospec = pl.BlockSpec(hblk, lambda b, m: (b, 0, m, 0))
    full2 = lambda a: pl.BlockSpec(a.shape, lambda b, m: (0, 0))
    return pl.pallas_call(
        _out_body,
        grid=(bsz, length // tm),
        in_specs=[ospec, ospec, ospec, ospec,
                  pl.BlockSpec(hblk, lambda b, m: (b, OFF_AZ // N_HEADS, m, 0)),
                  pl.BlockSpec(hblk, lambda b, m: (b, OFF_BZ // N_HEADS, m, 0)),
                  full2(na_w), full2(nb_w), full2(w_out),
                  pl.BlockSpec((1, tm, d), lambda b, m: (b, m, 0)),
                  pl.BlockSpec((1, 1, d), lambda b, m: (b, 0, 0)),
                  full2(final_w)],
        out_specs=pl.BlockSpec((1, tm, d), lambda b, m: (b, m, 0)),
        out_shape=jax.ShapeDtypeStruct((bsz, length, d), F32),
        compiler_params=pltpu.CompilerParams(dimension_semantics=("parallel", "parallel"),
                                             vmem_limit_bytes=VMEM_LIMIT),
    )(oa_f, oa_b, ob_f, ob_b, yh, yh, na_w, nb_w, w_out, x, gate, final_w)


def kernel(x, c, ctx, c_ctx, norm_w, ada_w, ada_b, w_in, conv_w, hg_lb_logits, gdn_a_log,
           gdn_dt_bias, ha_norm_w, hb_norm_w, w_out, final_norm_w):
    bsz, length, d = x.shape
    lc = ctx.shape[1]
    assert d == D_MODEL and length % 512 == 0 and length % GRID_W == 0 and lc % CHUNK == 0
    assert w_in.shape[0] == 1, "single-layer block"

    n_cond = bsz + 1
    cond = jnp.concatenate([c, c_ctx[None, :], jnp.zeros((-n_cond % 8, d), F32)], axis=0)
    mod = _adaln(cond, ada_w[0], ada_b[0])
    shift, scale, gate = mod[:, :d], mod[:, d:2 * d], mod[:, 2 * d:]
    lat = lambda m: m[:bsz, None, :]
    rep_ctx = lambda m: jnp.broadcast_to(m[bsz:bsz + 1, None, :], (bsz, 1, d))

    w_main = w_in[0].astype(BF16)
    w_gate_t = w_main[:, N_MAIN:].T
    nw = norm_w[0].reshape(1, d)
    yh, gates = _inproj(x, nw, lat(scale), lat(shift), w_main, w_gate_t, tm=min(2048, length))
    yh_c, gates_c = _inproj(ctx, nw, rep_ctx(scale), rep_ctx(shift), w_main, w_gate_t, tm=lc)

    lb = jax.nn.softmax(hg_lb_logits.astype(F32), axis=0)[0]
    lb = lb.reshape(2, N_HEADS, HEAD_DIM).transpose(1, 0, 2)
    tb = min(512, length)
    oa_f, oa_b = _hgrn2_scan(yh, yh_c, lb, tb)

    cw = conv_w[0].reshape(9, 3 * N_HEADS, HEAD_DIM).transpose(1, 0, 2)
    qkv = _gdn_conv(yh, cw, two_d=True)
    qkv_c = _gdn_conv(yh_c, cw, two_d=False)
    params = jnp.stack([gdn_a_log[0].reshape(-1), gdn_dt_bias[0].reshape(-1)], axis=1).astype(F32)
    ob_f, ob_b = _gdn_scan(qkv, _gate_rows(gates, params), qkv_c, _gate_rows(gates_c, params), tb)

    return _out_stage(oa_f, oa_b, ob_f, ob_b, yh, ha_norm_w[0], hb_norm_w[0],
                      w_out[0].astype(BF16), x, lat(gate), final_norm_w.reshape(1, d),
                      tm=min(512, length))
```

```python
import functools

import jax
import jax.numpy as jnp
import numpy as np
from jax import lax
from jax.experimental import pallas as pl
from jax.experimental.pallas import tpu as pltpu

F32 = jnp.float32
BF16 = jnp.bfloat16

D_MODEL = 1024
N_HEADS = 8
HEAD_DIM = 128
CHUNK = 64
GRID_W = 64
NORM_EPS = 1e-6
N_MAIN = 72 * HEAD_DIM
N_GATE = 4 * N_HEADS
OFF_AQ, OFF_AFF, OFF_AFB, OFF_AI, OFF_AZ, OFF_BQ, OFF_BZ = 0, 8, 16, 24, 32, 40, 64
NEG_BIG = -1e30
VMEM_LIMIT = 56 * 1024 * 1024


def _dot(a, b):
    return jnp.dot(a, b, preferred_element_type=F32)


def _dot_nt(a, b):
    return lax.dot_general(a, b, (((1,), (1,)), ((), ())), preferred_element_type=F32)


def _sigmoid(x):
    return 1.0 / (1.0 + jnp.exp(-x))


def _silu(x):
    return x * _sigmoid(x)


def _adaln_body(c_ref, w_ref, b_ref, o_ref):
    o_ref[...] = _dot(_silu(c_ref[...]), w_ref[...]) + b_ref[...]


def _adaln(cond, ada_w, ada_b):
    rows, d = cond.shape
    n = ada_w.shape[1]
    tn = 1024
    return pl.pallas_call(
        _adaln_body,
        grid=(n // tn,),
        in_specs=[pl.BlockSpec((rows, d), lambda j: (0, 0)),
                  pl.BlockSpec((d, tn), lambda j: (0, j)),
                  pl.BlockSpec((1, tn), lambda j: (0, j))],
        out_specs=pl.BlockSpec((rows, tn), lambda j: (0, j)),
        out_shape=jax.ShapeDtypeStruct((rows, n), F32),
        compiler_params=pltpu.CompilerParams(dimension_semantics=("arbitrary",),
                                             vmem_limit_bytes=VMEM_LIMIT),
    )(cond, ada_w, ada_b.reshape(1, n))


def _inproj_body(x_ref, nw_ref, sc_ref, sh_ref, w_ref, wg_ref, y_ref, yg_ref, h_scr, *, tn):
    @pl.when(pl.program_id(2) == 0)
    def _():
        x = x_ref[0]
        ms = jnp.mean(x * x, axis=-1, keepdims=True)
        h = x * lax.rsqrt(ms + NORM_EPS) * nw_ref[...]
        h = (h * (1.0 + sc_ref[0]) + sh_ref[0]).astype(BF16)
        h_scr[...] = h
        yg_ref[0] = _dot_nt(wg_ref[...], h)

    acc = _dot(h_scr[...], w_ref[...])
    for j in range(tn // HEAD_DIM):
        y_ref[0, j] = acc[:, j * HEAD_DIM:(j + 1) * HEAD_DIM].astype(y_ref.dtype)


def _inproj(x, norm_w, scale, shift, w_main, w_gate_t, tm):
    bsz, length, d = x.shape
    tn = 1024
    grid = (bsz, length // tm, N_MAIN // tn)
    return pl.pallas_call(
        functools.partial(_inproj_body, tn=tn),
        grid=grid,
        in_specs=[pl.BlockSpec((1, tm, d), lambda b, m, n: (b, m, 0)),
                  pl.BlockSpec((1, d), lambda b, m, n: (0, 0)),
                  pl.BlockSpec((1, 1, d), lambda b, m, n: (b, 0, 0)),
                  pl.BlockSpec((1, 1, d), lambda b, m, n: (b, 0, 0)),
                  pl.BlockSpec((d, tn), lambda b, m, n: (0, n)),
                  pl.BlockSpec((N_GATE, d), lambda b, m, n: (0, 0))],
        out_specs=[pl.BlockSpec((1, tn // HEAD_DIM, tm, HEAD_DIM), lambda b, m, n: (b, n, m, 0)),
                   pl.BlockSpec((1, N_GATE, tm), lambda b, m, n: (b, 0, m))],
        out_shape=[jax.ShapeDtypeStruct((bsz, N_MAIN // HEAD_DIM, length, HEAD_DIM), BF16),
                   jax.ShapeDtypeStruct((bsz, N_GATE, length), F32)],
        scratch_shapes=[pltpu.VMEM((tm, d), BF16)],
        compiler_params=pltpu.CompilerParams(
            dimension_semantics=("parallel", "parallel", "arbitrary"),
            vmem_limit_bytes=VMEM_LIMIT),
    )(x, norm_w, scale, shift, w_main, w_gate_t)


CONV_PAD = 72
CONV_HALO = 8


def _conv_body(x_ref, w_ref, o_ref, pad_scr, *, length, width, two_d, rt):
    blk = pl.program_id(1)
    zeros = jnp.zeros((CONV_PAD, HEAD_DIM), F32)
    pad_scr[0:CONV_PAD, :] = zeros
    pad_scr[CONV_PAD + length:CONV_PAD + length + CONV_PAD, :] = zeros
    pad_scr[CONV_PAD:CONV_PAD + length, :] = x_ref[0, 0].astype(F32)
    w = 0.5 * w_ref[0]
    win_rows = rt + 2 * CONV_HALO
    is_q = blk < N_HEADS
    is_qk = blk < 2 * N_HEADS
    col = lax.broadcasted_iota(jnp.int32, (rt, HEAD_DIM), 0) & (width - 1)

    def tile(i, carry):
        s = pl.multiple_of(i * rt, rt)
        sums = [None, None, None]
        for dr in ((-1, 0, 1) if two_d else (0,)):
            base = pl.multiple_of(s + (CONV_PAD + dr * width - CONV_HALO), 8)
            win = pad_scr[pl.ds(base, win_rows), :]
            for j in range(3):
                tap = (dr + 1) * 3 + j
                term = win * w[tap:tap + 1, :]
                sums[j] = term if sums[j] is None else sums[j] + term
        inner = slice(CONV_HALO, CONV_HALO + rt)
        left = pltpu.roll(sums[0], 1, 0)[inner]
        right = pltpu.roll(sums[2], win_rows - 1, 0)[inner]
        hs = sums[1][inner] + jnp.where(col >= 1, left, 0.0) + jnp.where(col <= width - 2, right, 0.0)
        a = hs + hs * jnp.tanh(hs)
        nrm = lax.rsqrt(jnp.sum(a * a, axis=-1, keepdims=True) + NORM_EPS)
        f = jnp.where(is_q, nrm * HEAD_DIM ** -0.5, jnp.where(is_qk, nrm, 1.0))
        o_ref[0, 0, pl.ds(s, rt), :] = (a * f).astype(o_ref.dtype)
        return carry

    lax.fori_loop(0, length // rt, tile, 0)


def _gdn_conv(yh, conv_w, two_d, width=GRID_W):
    bsz, _, length, _ = yh.shape
    assert width & (width - 1) == 0, "grid width must be a power of two (column index by bit mask)"
    rt = min(512, length)
    assert rt % width == 0
    nblk = 3 * N_HEADS
    return pl.pallas_call(
        functools.partial(_conv_body, length=length, width=width, two_d=two_d, rt=rt),
        grid=(bsz, nblk),
        in_specs=[pl.BlockSpec((1, 1, length, HEAD_DIM), lambda b, j: (b, OFF_BQ + j, 0, 0)),
                  pl.BlockSpec((1, 9, HEAD_DIM), lambda b, j: (j, 0, 0))],
        out_specs=pl.BlockSpec((1, 1, length, HEAD_DIM), lambda b, j: (b, j, 0, 0)),
        out_shape=jax.ShapeDtypeStruct((bsz, nblk, length, HEAD_DIM), BF16),
        scratch_shapes=[pltpu.VMEM((length + 2 * CONV_PAD, HEAD_DIM), F32)],
        compiler_params=pltpu.CompilerParams(dimension_semantics=("parallel", "parallel"),
                                             vmem_limit_bytes=VMEM_LIMIT),
    )(yh, conv_w)


HG_LEVELS = (32, 16, 8, 4, 2, 1)
HG_PAIRS = ((0, 1), (2, 3), (4, 5))
HG_GROUP = 4
LOG2E = 1.4426950408889634


def _hg_constants():
    t = np.arange(CHUNK)[:, None]
    col = np.arange(HEAD_DIM)[None, :]
    s = col % CHUNK
    out = [np.broadcast_to(np.where((t & m) != 0, 1.0, -1.0), (CHUNK, HEAD_DIM)) for m in HG_LEVELS]
    for rev in (False, True):
        for pa, pb in HG_PAIRS:
            keep = np.zeros((CHUNK, HEAD_DIM), bool)
            for half, lv in ((col < CHUNK, pa), (col >= CHUNK, pb)):
                m = HG_LEVELS[lv]
                split = ((t ^ s) >> (m.bit_length() - 1)) == 1
                t_hi = (t & m) != 0
                keep |= half & split & (~t_hi if rev else t_hi)
            out.append(keep)
    out.append((col == t) & (col < CHUNK))
    return jnp.asarray(np.stack([np.asarray(o, np.float32) for o in out]))


def _hg_triangles():
    t = np.arange(CHUNK)
    lower = (t[None, :] <= t[:, None]).astype(np.float32)
    return jnp.asarray(np.stack([np.tile(lower, (1, 3)), np.tile(lower.T, (1, 3))]), BF16)


def _cumsum_rows(g, tri3):
    hi = g.astype(BF16)
    r1 = g - hi.astype(F32)
    mid = r1.astype(BF16)
    lo = (r1 - mid.astype(F32)).astype(BF16)
    return _dot(tri3, jnp.concatenate([hi, mid, lo], axis=0))


def _level_operand(b, b_ref, q, k, m, rev):
    parts = []
    for blk in range(CHUNK // m):
        rows = slice(blk * m, (blk + 1) * m)
        r = (blk // 2) * 2 * m + m
        ref = jnp.broadcast_to(b_ref[r:r + 1, :], (m, HEAD_DIM))
        q_side = (blk % 2 == 1) != rev
        e = (b[rows] - ref) if q_side else (ref - b[rows])
        parts.append(((q if q_side else k)[rows] * jnp.exp2(e)).astype(BF16))
    return jnp.concatenate(parts, axis=0)


def _level_ref(b_ref, m, sub):
    bc = lambda r, n: jnp.broadcast_to(b_ref[r:r + 1, :], (n, HEAD_DIM))
    if m >= 4:
        return jnp.concatenate([bc(blk * 2 * m + m, 2 * m) for blk in range(CHUNK // (2 * m))], axis=0)
    lo = jnp.concatenate([bc(v * 8 + 2, 8) for v in range(CHUNK // 8)], axis=0)
    hi = jnp.concatenate([bc(v * 8 + 6, 8) for v in range(CHUNK // 8)], axis=0)
    return jnp.where(sub < 4, lo, hi)


def _hg_prepare(items, readout, cst_ref, tri_ref, b_scr, sc_scr, vv_scr, u_scr, dec_scr):
    row = lax.broadcasted_iota(jnp.int32, (CHUNK, HEAD_DIM), 0)
    sub = row & 7
    even = (row & 1) == 0
    gs, ks, bs = [], [], []
    for i, (_, _, load_f, _, lb, rev) in enumerate(items):
        c1 = 0.5 - 0.5 * lb
        p = c1 * jnp.tanh(0.5 * load_f())
        g = jnp.log2((0.5 + 0.5 * lb) + p)
        gs.append(g)
        ks.append(c1 - p)
        b = _cumsum_rows(g, tri_ref[1 if rev else 0])
        bs.append(b)
        b_scr[i] = b
    for i, (slot, _, _, load_v, _, rev) in enumerate(items):
        b = bs[i]
        btot = b[0:1, :] if rev else b[CHUNK - 1:CHUNK, :]
        v = load_v()
        vv_t = jnp.concatenate([v, v], axis=0).T.astype(BF16)
        k_dec = (ks[i] * jnp.exp2(btot - b)).astype(BF16)
        u_scr[slot] = _dot(vv_t[:, :CHUNK], k_dec)
        dec_scr[slot] = jnp.exp2(btot)
        vv_scr[slot] = vv_t
    if not readout:
        return

    qs = []
    for _, load_q, _, _, _, _ in items:
        q_raw = load_q()
        hs = (0.5 * HEAD_DIM ** -0.5) * q_raw
        qs.append(hs + hs * jnp.tanh(0.5 * q_raw))
    scores = [jnp.zeros((CHUNK, HEAD_DIM), F32) for _ in items]
    zero_blk = jnp.zeros((CHUNK, HEAD_DIM), BF16)
    for pi, pair in enumerate(HG_PAIRS):
        gps = []
        for i, (_, _, _, _, _, rev) in enumerate(items):
            a_mats = []
            for lv in pair:
                m = HG_LEVELS[lv]
                if m >= 8:
                    a_mats.append(_level_operand(bs[i], b_scr.at[i], qs[i], ks[i], m, rev))
                    continue
                sign = cst_ref[lv]
                if m == 1:
                    x = jnp.exp2(jnp.where(even, gs[i] if rev else pltpu.roll(gs[i], CHUNK - 1, 0), 0.0))
                else:
                    d = bs[i] - _level_ref(b_scr.at[i], m, sub)
                    x = jnp.exp2((-d if rev else d) * sign)
                t_is_q = (sign < 0.0) if rev else (sign > 0.0)
                a_mats.append((jnp.where(t_is_q, qs[i], ks[i]) * x).astype(BF16))
            lhs = jnp.concatenate(a_mats, axis=1)
            rhs = jnp.concatenate([jnp.concatenate([a_mats[0], zero_blk], axis=1),
                                   jnp.concatenate([zero_blk, a_mats[1]], axis=1)], axis=0)
            gps.append(_dot_nt(lhs, rhs))
        for i, (_, _, _, _, _, rev) in enumerate(items):
            scores[i] = scores[i] + gps[i] * cst_ref[6 + (3 if rev else 0) + pi]
    for i, (slot, _, _, _, _, _) in enumerate(items):
        diag = jnp.sum(qs[i] * ks[i], axis=-1, keepdims=True) * cst_ref[12]
        sc_scr[slot, :, :HEAD_DIM] = (scores[i] + diag).astype(BF16)
        sc_scr[slot, :, HEAD_DIM:] = (qs[i] * jnp.exp2(bs[i])).astype(BF16)


def _hg_recur(steps, dirs, readout, sc_scr, vv_scr, u_scr, dec_scr):
    states = [s_scr[...] for s_scr, _, _ in dirs]
    for c in steps:
        for i, (_, slot_of, store_o) in enumerate(dirs):
            slot = slot_of(c)
            if readout:
                rhs = jnp.concatenate([vv_scr[slot], states[i].astype(BF16)], axis=1)
                store_o(c, _dot_nt(sc_scr[slot], rhs))
            states[i] = states[i] * dec_scr[slot] + u_scr[slot]
    for (s_scr, _, _), st in zip(dirs, states):
        s_scr[...] = st


def _hg_body(cst_ref, tri_ref, qf_ref, ff_ref, vf_ref, qb_ref, fb_ref, vb_ref,
             cff_ref, cfb_ref, cv_ref, lb_ref, of_ref, ob_ref,
             sf_scr, sb_scr, b_scr, sc_scr, vv_scr, u_scr, dec_scr, *, n_chunks, n_ctx_chunks):
    lb_f = lb_ref[0, 0:1, :]
    lb_b = lb_ref[0, 1:2, :]
    scr = (sc_scr, vv_scr, u_scr, dec_scr)
    load = lambda ref, rows: (lambda: ref[0, 0, rows, :].astype(F32))

    @pl.when(pl.program_id(2) == 0)
    def _():
        sf_scr[...] = jnp.zeros_like(sf_scr)
        sb_scr[...] = jnp.zeros_like(sb_scr)
        n = n_ctx_chunks
        for c0 in range(0, n, HG_GROUP):
            items = []
            for c in range(c0, min(c0 + HG_GROUP, n)):
                sl = slice(c * CHUNK, (c + 1) * CHUNK)
                items.append((c, None, load(cff_ref, sl), load(cv_ref, sl), lb_f, False))
                items.append((n + c, None, load(cfb_ref, sl), load(cv_ref, sl), lb_b, True))
            _hg_prepare(items, False, cst_ref, tri_ref, b_scr, *scr)
        _hg_recur(range(n), [(sf_scr, lambda c: c, None), (sb_scr, lambda c: 2 * n - 1 - c, None)],
                  False, *scr)

    n = n_chunks
    rows = lambda c: slice(c * CHUNK, (c + 1) * CHUNK)

    def prepare(g):
        items = []
        for p in range(g * HG_GROUP, (g + 1) * HG_GROUP):
            rf, rb = rows(p), rows(n - 1 - p)
            items.append((p, load(qf_ref, rf), load(ff_ref, rf), load(vf_ref, rf), lb_f, False))
            items.append((2 * n - 1 - p, load(qb_ref, rb), load(fb_ref, rb), load(vb_ref, rb), lb_b, True))
        _hg_prepare(items, True, cst_ref, tri_ref, b_scr, *scr)

    def store_f(c, o):
        of_ref[0, 0, rows(c), :] = o.astype(of_ref.dtype)

    def store_b(c, o):
        ob_ref[0, 0, rows(n - 1 - c), :] = o.astype(ob_ref.dtype)

    def recur(g):
        _hg_recur(range(g * HG_GROUP, (g + 1) * HG_GROUP),
                  [(sf_scr, lambda c: c, store_f), (sb_scr, lambda c: 2 * n - 1 - c, store_b)], True, *scr)

    groups = n // HG_GROUP
    prepare(0)
    for g in range(1, groups):
        recur(g - 1)
        prepare(g)
    recur(groups - 1)


def _hgrn2_scan(yh, yh_ctx, lb, tb):
    bsz, _, length, _ = yh.shape
    lc = yh_ctx.shape[2] // bsz
    nt = length // tb
    nc, ncc = tb // CHUNK, lc // CHUNK
    assert nc % HG_GROUP == 0
    slots = 2 * max(nc, ncc)
    blk = (1, 1, tb, HEAD_DIM)
    cblk = (1, 1, lc, HEAD_DIM)
    fwd = lambda off: pl.BlockSpec(blk, lambda b, h, t: (b, off + h, t, 0))
    bwd = lambda off: pl.BlockSpec(blk, lambda b, h, t: (b, off + h, nt - 1 - t, 0))
    ctx = lambda off: pl.BlockSpec(cblk, lambda b, h, t: (0, off + h, b, 0))
    consts = _hg_constants()
    tri3 = _hg_triangles()
    out_shape = jax.ShapeDtypeStruct((bsz, N_HEADS, length, HEAD_DIM), BF16)
    return pl.pallas_call(
        functools.partial(_hg_body, n_chunks=nc, n_ctx_chunks=ncc),
        grid=(bsz, N_HEADS, nt),
        in_specs=[pl.BlockSpec(consts.shape, lambda b, h, t: (0, 0, 0)),
                  pl.BlockSpec(tri3.shape, lambda b, h, t: (0, 0, 0)),
                  fwd(OFF_AQ), fwd(OFF_AFF), fwd(OFF_AI), bwd(OFF_AQ), bwd(OFF_AFB), bwd(OFF_AI),
                  ctx(OFF_AFF), ctx(OFF_AFB), ctx(OFF_AI),
                  pl.BlockSpec((1, 2, HEAD_DIM), lambda b, h, t: (h, 0, 0))],
        out_specs=[pl.BlockSpec(blk, lambda b, h, t: (b, h, t, 0)),
                   pl.BlockSpec(blk, lambda b, h, t: (b, h, nt - 1 - t, 0))],
        out_shape=[out_shape, out_shape],
        scratch_shapes=[pltpu.VMEM((HEAD_DIM, HEAD_DIM), F32), pltpu.VMEM((HEAD_DIM, HEAD_DIM), F32),
                        pltpu.VMEM((2 * HG_GROUP, CHUNK, HEAD_DIM), F32),
                        pltpu.VMEM((slots, CHUNK, 2 * HEAD_DIM), BF16),
                        pltpu.VMEM((slots, HEAD_DIM, HEAD_DIM), BF16),
                        pltpu.VMEM((slots, HEAD_DIM, HEAD_DIM), F32),
                        pltpu.VMEM((slots, 1, HEAD_DIM), F32)],
        compiler_params=pltpu.CompilerParams(
            dimension_semantics=("parallel", "parallel", "arbitrary"),
            vmem_limit_bytes=VMEM_LIMIT),
    )(consts, tri3, yh, yh, yh, yh, yh, yh, yh_ctx, yh_ctx, yh_ctx, lb)


def _seg_cumsum_lanes(x, rev, lane):
    total = x.shape[1]
    seg = lane & (CHUNK - 1)
    for sh in (1, 2, 4, 8, 16, 32):
        if rev:
            x = x + jnp.where(seg < CHUNK - sh, pltpu.roll(x, total - sh, 1), 0.0)
        else:
            x = x + jnp.where(seg >= sh, pltpu.roll(x, sh, 1), 0.0)
    return x


def _gate_rows_body(g_ref, p_ref, o_ref):
    nrow = 2 * N_HEADS
    total = g_ref.shape[2]
    lane = lax.broadcasted_iota(jnp.int32, (nrow, total), 1)
    row = lax.broadcasted_iota(jnp.int32, (nrow, total), 0)
    z = g_ref[0, :nrow, :] + p_ref[:, 1:2]
    softplus = jnp.maximum(z, 0.0) + jnp.log1p(jnp.exp(-jnp.abs(z)))
    g = -jnp.exp(p_ref[:, 0:1]) * softplus
    prefix = _seg_cumsum_lanes(g, False, lane)
    suffix = _seg_cumsum_lanes(g, True, lane)
    o_ref[0, 0] = jnp.where(row < N_HEADS, prefix, suffix)
    o_ref[0, 1] = _sigmoid(g_ref[0, nrow:, :])
    o_ref[0, 2] = prefix + suffix - g


def _gate_rows(gates, params):
    bsz, _, length = gates.shape
    tg = min(2048, length)
    nrow = 2 * N_HEADS
    return pl.pallas_call(
        _gate_rows_body,
        grid=(bsz, length // tg),
        in_specs=[pl.BlockSpec((1, N_GATE, tg), lambda b, t: (b, 0, t)),
                  pl.BlockSpec(params.shape, lambda b, t: (0, 0))],
        out_specs=pl.BlockSpec((1, 3, nrow, tg), lambda b, t: (b, 0, 0, t)),
        out_shape=jax.ShapeDtypeStruct((bsz, 3, nrow, length), F32),
        compiler_params=pltpu.CompilerParams(dimension_semantics=("parallel", "parallel"),
                                             vmem_limit_bytes=VMEM_LIMIT),
    )(gates, params)


def _gdn_gates(g_ref, d, head):
    r = pl.ds(d * N_HEADS + head, 1)
    b = g_ref[0, 0, r, :]
    rows = jnp.concatenate([b, g_ref[0, 1, r, :], g_ref[0, 2, r, :]], axis=0)
    return rows, jnp.broadcast_to(b, (HEAD_DIM, b.shape[1])).T


GDN_QUAD = 4
GDN_PREP_HEADS = 4


def _gdn_prepare(quads, readout, qkg_scr, tb_scr, kp_scr, dec_scr):
    wide = GDN_QUAD * CHUNK
    t_i = lax.broadcasted_iota(jnp.int32, (CHUNK, wide), 0)
    lane = lax.broadcasted_iota(jnp.int32, (CHUNK, wide), 1)
    s_i = lane & (CHUNK - 1)
    blk = [(lane >= j * CHUNK) & (lane < (j + 1) * CHUNK) for j in range(GDN_QUAD)]
    eye = (t_i == s_i).astype(F32)
    tri = {False: (s_i <= t_i, s_i < t_i), True: (s_i >= t_i, s_i > t_i)}
    rows_of = lambda j: slice(j * CHUNK, (j + 1) * CHUNK)

    def diag_blocks(m, off):
        out = jnp.where(blk[0], m[off:off + CHUNK], 0.0)
        for j in range(1, GDN_QUAD):
            out = jnp.where(blk[j], m[off + j * CHUNK:off + (j + 1) * CHUNK], out)
        return out

    def block_diag(y):
        return jnp.concatenate([jnp.where(blk[j], y, 0.0).astype(BF16) for j in range(GDN_QUAD)], axis=0)

    dmats, qks, xs, ys = [], [], [], []
    for _, load_q, load_k, rows, bcol, rev in quads:
        incl, strict = tri[rev]
        bc = jnp.concatenate([bcol, bcol], axis=1)
        dmat = jnp.exp(jnp.where(incl, diag_blocks(bc, 0) - rows[0:1, :], NEG_BIG))
        kb = load_k()
        if readout:
            qkk = _dot_nt(jnp.concatenate([load_q(), kb], axis=0), kb)
            qks.append(diag_blocks(qkk, 0))
            kk = diag_blocks(qkk, wide)
        else:
            kk = diag_blocks(_dot_nt(kb, kb), 0)
        n_mat = jnp.where(strict, kk * dmat, 0.0) * rows[1:2, :]
        dmats.append(dmat)
        xs.append(eye - n_mat)
        ys.append(n_mat)
    ys = [_dot(y.astype(BF16), block_diag(y)) for y in ys]
    for _ in range(4):
        xy = [_dot(jnp.concatenate([x, y], axis=0).astype(BF16), block_diag(y)) for x, y in zip(xs, ys)]
        xs = [x + p[:CHUNK] for x, p in zip(xs, xy)]
        ys = [p[CHUNK:] for p in xy]
    xs = [x + _dot(x.astype(BF16), block_diag(y)) for x, y in zip(xs, ys)]

    for i, (slot0, load_q, load_k, rows, bcol, rev) in enumerate(quads):
        k_all = load_k().astype(F32)
        e_col = jnp.exp(bcol)
        qkg_all_k = (k_all * e_col).astype(BF16)
        if readout:
            qkg_all_q = (load_q().astype(F32) * e_col).astype(BF16)
            pm_all = qks[i] * dmats[i] * rows[1:2, :]
        k_t = k_all.T
        kdt_all = (k_t * (jnp.exp(rows[2:3, :] - rows[0:1, :]) * rows[1:2, :])).astype(BF16)
        for j in range(GDN_QUAD):
            slot = slot0 + j
            shift = (wide - j * CHUNK) % wide
            take = lambda m: (m if shift == 0 else pltpu.roll(m, shift, 1))[:, :CHUNK]
            tb_scr[slot] = take(xs[i]).astype(BF16)
            qkg_scr[slot, CHUNK:, :] = qkg_all_k[rows_of(j)]
            if readout:
                qkg_scr[slot, :CHUNK, :] = qkg_all_q[rows_of(j)]
                kp_scr[slot, HEAD_DIM:, :] = take(pm_all).astype(BF16)
            kp_scr[slot, :HEAD_DIM, :] = kdt_all[:, j * CHUNK:(j + 1) * CHUNK]
            tot = rows[2:3, j * CHUNK:(j + 1) * CHUNK]
            dec_scr[slot] = jnp.exp(jnp.concatenate([tot, tot], axis=1))


def _gdn_recur(n, dirs, readout, qkg_scr, tb_scr, kp_scr, dec_scr):
    def step(c, carry):
        slots = [slot_of(c) for _, slot_of, _, _ in dirs]
        states = [s_ref[...] for s_ref, _, _, _ in dirs]
        if readout:
            qks = [_dot(qkg_scr[slot], s.astype(BF16)) for slot, s in zip(slots, states)]
            kss = [p[CHUNK:] for p in qks]
        else:
            kss = [_dot(qkg_scr[slot, CHUNK:, :], s.astype(BF16)) for slot, s in zip(slots, states)]
        ws = [_dot(tb_scr[slot], (load_v(c) - ks).astype(BF16)).astype(BF16)
              for slot, ks, (_, _, load_v, _) in zip(slots, kss, dirs)]
        if readout:
            res = [_dot(kp_scr[slot], w) for slot, w in zip(slots, ws)]
            for slot, s, r, p, (s_ref, _, _, store_o) in zip(slots, states, res, qks, dirs):
                s_ref[...] = s * dec_scr[slot] + r[:HEAD_DIM]
                store_o(c, p[:CHUNK] + r[HEAD_DIM:])
        else:
            for slot, s, w, (s_ref, _, _, _) in zip(slots, states, ws, dirs):
                s_ref[...] = s * dec_scr[slot] + _dot(kp_scr[slot, :HEAD_DIM, :], w)
        return carry

    lax.fori_loop(0, n, step, 0, unroll=4)


def _gdn_body(qf_ref, kf_ref, vf_ref, gf_ref, qb_ref, kb_ref, vb_ref, gb_ref,
              ck_ref, cv_ref, cg_ref, of_ref, ob_ref,
              s_scr, qkg_scr, tb_scr, kp_scr, dec_scr,
              *, n_chunks, n_ctx_chunks, heads):
    h0 = pl.program_id(1) * heads
    scr = (qkg_scr, tb_scr, kp_scr, dec_scr)
    ds = lambda c: pl.ds(pl.multiple_of(c * CHUNK, CHUNK), CHUNK)

    def chunk_list(hh, n, q_refs, k_refs, g_refs):
        head = h0 + hh
        wide = GDN_QUAD * CHUNK
        out = []
        for d in (0, 1):
            rows, bcol = _gdn_gates(g_refs[d], d, head)
            for c0 in range(0, n, GDN_QUAD):
                sl = slice(c0 * CHUNK, c0 * CHUNK + wide)
                load_q = None if q_refs is None else (lambda r=q_refs[d], sl=sl: r[0, hh, sl, :])
                load_k = lambda r=k_refs[d], sl=sl: r[0, hh, sl, :]
                out.append(((2 * hh + d) * n + c0, load_q, load_k, rows[:, sl], bcol[sl, :], d == 1))
        return out

    def chains(n, vf, vb, of, ob):
        out = []
        for hh in range(heads):
            out.append((s_scr.at[2 * hh], lambda c, hh=hh: 2 * hh * n + c,
                        lambda c, hh=hh: vf[0, hh, ds(c), :].astype(F32),
                        None if of is None else (lambda c, o, hh=hh: of.__setitem__(
                            (0, hh, ds(c), slice(None)), o.astype(of.dtype)))))
            out.append((s_scr.at[2 * hh + 1], lambda c, hh=hh: (2 * hh + 1) * n + (n - 1 - c),
                        lambda c, hh=hh: vb[0, hh, ds(n - 1 - c), :].astype(F32),
                        None if ob is None else (lambda c, o, hh=hh: ob.__setitem__(
                            (0, hh, ds(n - 1 - c), slice(None)), o.astype(ob.dtype)))))
        return out

    @pl.when(pl.program_id(2) == 0)
    def _():
        s_scr[...] = jnp.zeros_like(s_scr)
        n = n_ctx_chunks

        def ctx_head(hp, carry):
            quads = []
            for i in range(GDN_PREP_HEADS):
                quads += chunk_list(hp * GDN_PREP_HEADS + i, n, None, (ck_ref, ck_ref), (cg_ref, cg_ref))
            _gdn_prepare(quads, False, *scr)
            return carry

        lax.fori_loop(0, heads // GDN_PREP_HEADS, ctx_head, 0)
        _gdn_recur(n, chains(n, cv_ref, cv_ref, None, None), False, *scr)

    n = n_chunks

    def main_head(hp, carry):
        quads = []
        for i in range(GDN_PREP_HEADS):
            quads += chunk_list(hp * GDN_PREP_HEADS + i, n, (qf_ref, qb_ref), (kf_ref, kb_ref), (gf_ref, gb_ref))
        _gdn_prepare(quads, True, *scr)
        return carry

    lax.fori_loop(0, heads // GDN_PREP_HEADS, main_head, 0)
    _gdn_recur(n, chains(n, vf_ref, vb_ref, of_ref, ob_ref), True, *scr)


GDN_HEADS = 8


def _gdn_scan(qkv, gates, qkv_ctx, gates_ctx, tb):
    bsz, _, length, _ = qkv.shape
    lc = qkv_ctx.shape[2] // bsz
    nt = length // tb
    nc, ncc = tb // CHUNK, lc // CHUNK
    hb = GDN_HEADS
    assert nc % GDN_QUAD == 0 and ncc % GDN_QUAD == 0 and hb % GDN_PREP_HEADS == 0
    slots = 2 * hb * max(nc, ncc)
    blk = (1, hb, tb, HEAD_DIM)
    cblk = (1, hb, lc, HEAD_DIM)
    fwd = lambda off: pl.BlockSpec(blk, lambda b, h, t: (b, off // hb + h, t, 0))
    bwd = lambda off: pl.BlockSpec(blk, lambda b, h, t: (b, off // hb + h, nt - 1 - t, 0))
    ctx = lambda off: pl.BlockSpec(cblk, lambda b, h, t: (0, off // hb + h, b, 0))
    out_shape = jax.ShapeDtypeStruct((bsz, N_HEADS, length, HEAD_DIM), BF16)
    return pl.pallas_call(
        functools.partial(_gdn_body, n_chunks=nc, n_ctx_chunks=ncc, heads=hb),
        grid=(bsz, N_HEADS // hb, nt),
        in_specs=[fwd(0), fwd(N_HEADS), fwd(2 * N_HEADS),
                  pl.BlockSpec((1, 3, 2 * N_HEADS, tb), lambda b, h, t: (b, 0, 0, t)),
                  bwd(0), bwd(N_HEADS), bwd(2 * N_HEADS),
                  pl.BlockSpec((1, 3, 2 * N_HEADS, tb), lambda b, h, t: (b, 0, 0, nt - 1 - t)),
                  ctx(N_HEADS), ctx(2 * N_HEADS),
                  pl.BlockSpec((1, 3, 2 * N_HEADS, lc), lambda b, h, t: (0, 0, 0, b))],
        out_specs=[pl.BlockSpec(blk, lambda b, h, t: (b, h, t, 0)),
                   pl.BlockSpec(blk, lambda b, h, t: (b, h, nt - 1 - t, 0))],
        out_shape=[out_shape, out_shape],
        scratch_shapes=[pltpu.VMEM((2 * hb, HEAD_DIM, HEAD_DIM), F32),
                        pltpu.VMEM((slots, 2 * CHUNK, HEAD_DIM), BF16),
                        pltpu.VMEM((slots, CHUNK, CHUNK), BF16),
                        pltpu.VMEM((slots, HEAD_DIM + CHUNK, CHUNK), BF16),
                        pltpu.VMEM((slots, 1, HEAD_DIM), F32)],
        compiler_params=pltpu.CompilerParams(
            dimension_semantics=("parallel", "parallel", "arbitrary"),
            vmem_limit_bytes=VMEM_LIMIT),
    )(qkv, qkv, qkv, gates, qkv, qkv, qkv, gates, qkv_ctx, qkv_ctx, gates_ctx)


def _out_body(oaf_ref, oab_ref, obf_ref, obb_ref, za_ref, zb_ref, naw_ref, nbw_ref,
              w_ref, x_ref, gate_ref, fw_ref, o_ref):
    parts = []
    for of_ref, ob_ref, z_ref, nw_ref in ((oaf_ref, oab_ref, za_ref, naw_ref),
                                          (obf_ref, obb_ref, zb_ref, nbw_ref)):
        for h in range(N_HEADS):
            o = of_ref[0, h].astype(F32) + ob_ref[0, h].astype(F32)
            o = o * lax.rsqrt(jnp.mean(o * o, axis=-1, keepdims=True) + NORM_EPS) * nw_ref[h:h + 1, :]
            hz = 0.5 * z_ref[0, h].astype(F32)
            parts.append(((hz + hz * jnp.tanh(hz)) * o).astype(BF16))
    y = jnp.concatenate(parts, axis=1)
    xo = x_ref[0] + gate_ref[0] * _dot(y, w_ref[...])
    ms = jnp.mean(xo * xo, axis=-1, keepdims=True)
    o_ref[0] = xo * lax.rsqrt(ms + NORM_EPS) * fw_ref[...]


def _out_stage(oa_f, oa_b, ob_f, ob_b, yh, na_w, nb_w, w_out, x, gate, final_w, tm):
    bsz, length, d = x.shape
    hblk = (1, N_HEADS, tm, HEAD_DIM)
    ospec = pl.BlockSpec(hblk, lambda b, m: (b, 0, m, 0))
    full2 = lambda a: pl.BlockSpec(a.shape, lambda b, m: (0, 0))
    return pl.pallas_call(
        _out_body,
        grid=(bsz, length // tm),
        in_specs=[ospec, ospec, ospec, ospec,
                  pl.BlockSpec(hblk, lambda b, m: (b, OFF_AZ // N_HEADS, m, 0)),
                  pl.BlockSpec(hblk, lambda b, m: (b, OFF_BZ // N_HEADS, m, 0)),
                  full2(na_w), full2(nb_w), full2(w_out),
                  pl.BlockSpec((1, tm, d), lambda b, m: (b, m, 0)),
                  pl.BlockSpec((1, 1, d), lambda b, m: (b, 0, 0)),
                  full2(final_w)],
        out_specs=pl.BlockSpec((1, tm, d), lambda b, m: (b, m, 0)),
        out_shape=jax.ShapeDtypeStruct((bsz, length, d), F32),
        compiler_params=pltpu.CompilerParams(dimension_semantics=("parallel", "parallel"),
                                             vmem_limit_bytes=VMEM_LIMIT),
    )(oa_f, oa_b, ob_f, ob_b, yh, yh, na_w, nb_w, w_out, x, gate, final_w)


def kernel(x, c, ctx, c_ctx, norm_w, ada_w, ada_b, w_in, conv_w, hg_lb_logits, gdn_a_log,
           gdn_dt_bias, ha_norm_w, hb_norm_w, w_out, final_norm_w):
    bsz, length, d = x.shape
    lc = ctx.shape[1]
    assert d == D_MODEL and length % 512 == 0 and length % GRID_W == 0 and lc % CHUNK == 0
    assert w_in.shape[0] == 1, "single-layer block"

    n_cond = bsz + 1
    cond = jnp.concatenate([c, c_ctx[None, :], jnp.zeros((-n_cond % 8, d), F32)], axis=0)
    mod = _adaln(cond, ada_w[0], ada_b[0])
    shift, scale, gate = mod[:, :d], mod[:, d:2 * d], mod[:, 2 * d:]
    lat = lambda m: m[:bsz, None, :]
    of_ctx = lambda m: m[bsz:bsz + 1, None, :]

    w_main = w_in[0].astype(BF16)
    w_gate_t = w_main[:, N_MAIN:].T
    nw = norm_w[0].reshape(1, d)
    yh, gates = _inproj(x, nw, lat(scale), lat(shift), w_main, w_gate_t, tm=min(2048, length))
    yh_c, gates_c = _inproj(ctx.reshape(1, bsz * lc, d), nw, of_ctx(scale), of_ctx(shift), w_main, w_gate_t,
                            tm=bsz * lc)

    lb = jax.nn.softmax(hg_lb_logits.astype(F32), axis=0)[0]
    lb = lb.reshape(2, N_HEADS, HEAD_DIM).transpose(1, 0, 2)
    tb = min(512, length)
    oa_f, oa_b = _hgrn2_scan(yh, yh_c, lb, min(2048, length))

    cw = conv_w[0].reshape(9, 3 * N_HEADS, HEAD_DIM).transpose(1, 0, 2)
    qkv = _gdn_conv(yh, cw, two_d=True)
    qkv_c = _gdn_conv(yh_c, cw, two_d=False, width=lc)
    params = jnp.stack([gdn_a_log[0].reshape(-1), gdn_dt_bias[0].reshape(-1)], axis=1).astype(F32)
    ob_f, ob_b = _gdn_scan(qkv, _gate_rows(gates, params), qkv_c, _gate_rows(gates_c, params), tb)

    return _out_stage(oa_f, oa_b, ob_f, ob_b, yh, ha_norm_w[0], hb_norm_w[0],
                      w_out[0].astype(BF16), x, lat(gate), final_norm_w.reshape(1, d),
                      tm=min(512, length))
```

```python
import functools

import jax
import jax.numpy as jnp
import numpy as np
from jax import lax
from jax.experimental import pallas as pl
from jax.experimental.pallas import tpu as pltpu

F32 = jnp.float32
BF16 = jnp.bfloat16

D_MODEL = 1024
N_HEADS = 8
HEAD_DIM = 128
CHUNK = 64
GRID_W = 64
NORM_EPS = 1e-6
N_MAIN = 72 * HEAD_DIM
N_GATE = 4 * N_HEADS
OFF_AQ, OFF_AFF, OFF_AFB, OFF_AI, OFF_AZ, OFF_BQ, OFF_BZ = 0, 8, 16, 24, 32, 40, 64
NEG_BIG = -1e30
VMEM_LIMIT = 56 * 1024 * 1024


def _dot(a, b):
    return jnp.dot(a, b, preferred_element_type=F32)


def _dot_nt(a, b):
    return lax.dot_general(a, b, (((1,), (1,)), ((), ())), preferred_element_type=F32)


def _sigmoid(x):
    return 1.0 / (1.0 + jnp.exp(-x))


def _silu(x):
    return x * _sigmoid(x)


def _adaln_body(c_ref, w_ref, b_ref, o_ref):
    o_ref[...] = _dot(_silu(c_ref[...]), w_ref[...]) + b_ref[...]


def _adaln(cond, ada_w, ada_b):
    rows, d = cond.shape
    n = ada_w.shape[1]
    tn = 1024
    return pl.pallas_call(
        _adaln_body,
        grid=(n // tn,),
        in_specs=[pl.BlockSpec((rows, d), lambda j: (0, 0)),
                  pl.BlockSpec((d, tn), lambda j: (0, j)),
                  pl.BlockSpec((1, tn), lambda j: (0, j))],
        out_specs=pl.BlockSpec((rows, tn), lambda j: (0, j)),
        out_shape=jax.ShapeDtypeStruct((rows, n), F32),
        compiler_params=pltpu.CompilerParams(dimension_semantics=("arbitrary",),
                                             vmem_limit_bytes=VMEM_LIMIT),
    )(cond, ada_w, ada_b.reshape(1, n))


def _inproj_body(x_ref, nw_ref, sc_ref, sh_ref, w_ref, wg_ref, y_ref, yg_ref, h_scr, *, tn):
    @pl.when(pl.program_id(2) == 0)
    def _():
        x = x_ref[0]
        ms = jnp.mean(x * x, axis=-1, keepdims=True)
        h = x * lax.rsqrt(ms + NORM_EPS) * nw_ref[...]
        h = (h * (1.0 + sc_ref[0]) + sh_ref[0]).astype(BF16)
        h_scr[...] = h
        yg_ref[0] = _dot_nt(wg_ref[...], h)

    acc = _dot(h_scr[...], w_ref[...])
    for j in range(tn // HEAD_DIM):
        y_ref[0, j] = acc[:, j * HEAD_DIM:(j + 1) * HEAD_DIM].astype(y_ref.dtype)


def _inproj(x, norm_w, scale, shift, w_main, w_gate_t, tm):
    bsz, length, d = x.shape
    tn = 1024
    grid = (bsz, length // tm, N_MAIN // tn)
    return pl.pallas_call(
        functools.partial(_inproj_body, tn=tn),
        grid=grid,
        in_specs=[pl.BlockSpec((1, tm, d), lambda b, m, n: (b, m, 0)),
                  pl.BlockSpec((1, d), lambda b, m, n: (0, 0)),
                  pl.BlockSpec((1, 1, d), lambda b, m, n: (b, 0, 0)),
                  pl.BlockSpec((1, 1, d), lambda b, m, n: (b, 0, 0)),
                  pl.BlockSpec((d, tn), lambda b, m, n: (0, n)),
                  pl.BlockSpec((N_GATE, d), lambda b, m, n: (0, 0))],
        out_specs=[pl.BlockSpec((1, tn // HEAD_DIM, tm, HEAD_DIM), lambda b, m, n: (b, n, m, 0)),
                   pl.BlockSpec((1, N_GATE, tm), lambda b, m, n: (b, 0, m))],
        out_shape=[jax.ShapeDtypeStruct((bsz, N_MAIN // HEAD_DIM, length, HEAD_DIM), BF16),
                   jax.ShapeDtypeStruct((bsz, N_GATE, length), F32)],
        scratch_shapes=[pltpu.VMEM((tm, d), BF16)],
        compiler_params=pltpu.CompilerParams(
            dimension_semantics=("parallel", "parallel", "arbitrary"),
            vmem_limit_bytes=VMEM_LIMIT),
    )(x, norm_w, scale, shift, w_main, w_gate_t)


CONV_PAD = 72
CONV_HALO = 8


def _conv_body(x_ref, w_ref, o_ref, pad_scr, *, length, width, two_d, rt):
    blk = pl.program_id(1)
    zeros = jnp.zeros((CONV_PAD, HEAD_DIM), F32)
    pad_scr[0:CONV_PAD, :] = zeros
    pad_scr[CONV_PAD + length:CONV_PAD + length + CONV_PAD, :] = zeros
    pad_scr[CONV_PAD:CONV_PAD + length, :] = x_ref[0, 0].astype(F32)
    w = 0.5 * w_ref[0]
    win_rows = rt + 2 * CONV_HALO
    is_q = blk < N_HEADS
    is_qk = blk < 2 * N_HEADS
    col = lax.broadcasted_iota(jnp.int32, (rt, HEAD_DIM), 0) & (width - 1)

    def tile(i, carry):
        s = pl.multiple_of(i * rt, rt)
        sums = [None, None, None]
        for dr in ((-1, 0, 1) if two_d else (0,)):
            base = pl.multiple_of(s + (CONV_PAD + dr * width - CONV_HALO), 8)
            win = pad_scr[pl.ds(base, win_rows), :]
            for j in range(3):
                tap = (dr + 1) * 3 + j
                term = win * w[tap:tap + 1, :]
                sums[j] = term if sums[j] is None else sums[j] + term
        inner = slice(CONV_HALO, CONV_HALO + rt)
        left = pltpu.roll(sums[0], 1, 0)[inner]
        right = pltpu.roll(sums[2], win_rows - 1, 0)[inner]
        hs = sums[1][inner] + jnp.where(col >= 1, left, 0.0) + jnp.where(col <= width - 2, right, 0.0)
        a = hs + hs * jnp.tanh(hs)
        nrm = lax.rsqrt(jnp.sum(a * a, axis=-1, keepdims=True) + NORM_EPS)
        f = jnp.where(is_q, nrm * HEAD_DIM ** -0.5, jnp.where(is_qk, nrm, 1.0))
        o_ref[0, 0, pl.ds(s, rt), :] = (a * f).astype(o_ref.dtype)
        return carry

    lax.fori_loop(0, length // rt, tile, 0)


def _gdn_conv(yh, conv_w, two_d, width=GRID_W):
    bsz, _, length, _ = yh.shape
    assert width & (width - 1) == 0, "grid width must be a power of two (column index by bit mask)"
    rt = min(512, length)
    assert rt % width == 0
    nblk = 3 * N_HEADS
    return pl.pallas_call(
        functools.partial(_conv_body, length=length, width=width, two_d=two_d, rt=rt),
        grid=(bsz, nblk),
        in_specs=[pl.BlockSpec((1, 1, length, HEAD_DIM), lambda b, j: (b, OFF_BQ + j, 0, 0)),
                  pl.BlockSpec((1, 9, HEAD_DIM), lambda b, j: (j, 0, 0))],
        out_specs=pl.BlockSpec((1, 1, length, HEAD_DIM), lambda b, j: (b, j, 0, 0)),
        out_shape=jax.ShapeDtypeStruct((bsz, nblk, length, HEAD_DIM), BF16),
        scratch_shapes=[pltpu.VMEM((length + 2 * CONV_PAD, HEAD_DIM), F32)],
        compiler_params=pltpu.CompilerParams(dimension_semantics=("parallel", "parallel"),
                                             vmem_limit_bytes=VMEM_LIMIT),
    )(yh, conv_w)


HG_LEVELS = (32, 16, 8, 4, 2, 1)
HG_PAIRS = ((0, 1), (2, 3), (4, 5))
HG_GROUP = 4
LOG2E = 1.4426950408889634


def _hg_constants():
    t = np.arange(CHUNK)[:, None]
    col = np.arange(HEAD_DIM)[None, :]
    s = col % CHUNK
    out = [np.broadcast_to(np.where((t & m) != 0, 1.0, -1.0), (CHUNK, HEAD_DIM)) for m in HG_LEVELS]
    for rev in (False, True):
        for pa, pb in HG_PAIRS:
            keep = np.zeros((CHUNK, HEAD_DIM), bool)
            for half, lv in ((col < CHUNK, pa), (col >= CHUNK, pb)):
                m = HG_LEVELS[lv]
                split = ((t ^ s) >> (m.bit_length() - 1)) == 1
                t_hi = (t & m) != 0
                keep |= half & split & (~t_hi if rev else t_hi)
            out.append(keep)
    out.append((col == t) & (col < CHUNK))
    return jnp.asarray(np.stack([np.asarray(o, np.float32) for o in out]))


def _hg_triangles():
    t = np.arange(CHUNK)
    lower = (t[None, :] <= t[:, None]).astype(np.float32)
    return jnp.asarray(np.stack([np.tile(lower, (1, 3)), np.tile(lower.T, (1, 3))]), BF16)


def _cumsum_rows(g, tri3):
    hi = g.astype(BF16)
    r1 = g - hi.astype(F32)
    mid = r1.astype(BF16)
    lo = (r1 - mid.astype(F32)).astype(BF16)
    return _dot(tri3, jnp.concatenate([hi, mid, lo], axis=0))


def _level_operand(b, b_ref, q, k, m, rev):
    parts = []
    for blk in range(CHUNK // m):
        rows = slice(blk * m, (blk + 1) * m)
        r = (blk // 2) * 2 * m + m
        ref = jnp.broadcast_to(b_ref[r:r + 1, :], (m, HEAD_DIM))
        q_side = (blk % 2 == 1) != rev
        e = (b[rows] - ref) if q_side else (ref - b[rows])
        parts.append(((q if q_side else k)[rows] * jnp.exp2(e)).astype(BF16))
    return jnp.concatenate(parts, axis=0)


def _level_ref(b_ref, m, sub):
    bc = lambda r, n: jnp.broadcast_to(b_ref[r:r + 1, :], (n, HEAD_DIM))
    if m >= 4:
        return jnp.concatenate([bc(blk * 2 * m + m, 2 * m) for blk in range(CHUNK // (2 * m))], axis=0)
    lo = jnp.concatenate([bc(v * 8 + 2, 8) for v in range(CHUNK // 8)], axis=0)
    hi = jnp.concatenate([bc(v * 8 + 6, 8) for v in range(CHUNK // 8)], axis=0)
    return jnp.where(sub < 4, lo, hi)


def _hg_prepare(items, readout, cst_ref, tri_ref, b_scr, sc_scr, vv_scr, u_scr, dec_scr):
    row = lax.broadcasted_iota(jnp.int32, (CHUNK, HEAD_DIM), 0)
    sub = row & 7
    even = (row & 1) == 0
    gs, ks, bs = [], [], []
    for i, (_, _, load_f, _, lb, rev) in enumerate(items):
        c1 = 0.5 - 0.5 * lb
        p = c1 * jnp.tanh(0.5 * load_f())
        g = jnp.log2((0.5 + 0.5 * lb) + p)
        gs.append(g)
        ks.append(c1 - p)
        b = _cumsum_rows(g, tri_ref[1 if rev else 0])
        bs.append(b)
        b_scr[i] = b
    for i, (slot, _, _, load_v, _, rev) in enumerate(items):
        b = bs[i]
        btot = b[0:1, :] if rev else b[CHUNK - 1:CHUNK, :]
        v = load_v()
        vv_t = jnp.concatenate([v, v], axis=0).T.astype(BF16)
        k_dec = (ks[i] * jnp.exp2(btot - b)).astype(BF16)
        u_scr[slot] = _dot(vv_t[:, :CHUNK], k_dec)
        dec_scr[slot] = jnp.exp2(btot)
        vv_scr[slot] = vv_t
    if not readout:
        return

    qs = []
    for _, load_q, _, _, _, _ in items:
        q_raw = load_q()
        hs = (0.5 * HEAD_DIM ** -0.5) * q_raw
        qs.append(hs + hs * jnp.tanh(0.5 * q_raw))
    scores = [jnp.zeros((CHUNK, HEAD_DIM), F32) for _ in items]
    zero_blk = jnp.zeros((CHUNK, HEAD_DIM), BF16)
    for pi, pair in enumerate(HG_PAIRS):
        gps = []
        for i, (_, _, _, _, _, rev) in enumerate(items):
            a_mats = []
            for lv in pair:
                m = HG_LEVELS[lv]
                if m >= 8:
                    a_mats.append(_level_operand(bs[i], b_scr.at[i], qs[i], ks[i], m, rev))
                    continue
                sign = cst_ref[lv]
                if m == 1:
                    x = jnp.exp2(jnp.where(even, gs[i] if rev else pltpu.roll(gs[i], CHUNK - 1, 0), 0.0))
                else:
                    d = bs[i] - _level_ref(b_scr.at[i], m, sub)
                    x = jnp.exp2((-d if rev else d) * sign)
                t_is_q = (sign < 0.0) if rev else (sign > 0.0)
                a_mats.append((jnp.where(t_is_q, qs[i], ks[i]) * x).astype(BF16))
            lhs = jnp.concatenate(a_mats, axis=1)
            rhs = jnp.concatenate([jnp.concatenate([a_mats[0], zero_blk], axis=1),
                                   jnp.concatenate([zero_blk, a_mats[1]], axis=1)], axis=0)
            gps.append(_dot_nt(lhs, rhs))
        for i, (_, _, _, _, _, rev) in enumerate(items):
            scores[i] = scores[i] + gps[i] * cst_ref[6 + (3 if rev else 0) + pi]
    for i, (slot, _, _, _, _, _) in enumerate(items):
        diag = jnp.sum(qs[i] * ks[i], axis=-1, keepdims=True) * cst_ref[12]
        sc_scr[slot, :, :HEAD_DIM] = (scores[i] + diag).astype(BF16)
        sc_scr[slot, :, HEAD_DIM:] = (qs[i] * jnp.exp2(bs[i])).astype(BF16)


def _hg_recur(steps, dirs, readout, sc_scr, vv_scr, u_scr, dec_scr):
    states = [s_scr[...] for s_scr, _, _ in dirs]
    for c in steps:
        for i, (_, slot_of, store_o) in enumerate(dirs):
            slot = slot_of(c)
            if readout:
                rhs = jnp.concatenate([vv_scr[slot], states[i].astype(BF16)], axis=1)
                store_o(c, _dot_nt(sc_scr[slot], rhs))
            states[i] = states[i] * dec_scr[slot] + u_scr[slot]
    for (s_scr, _, _), st in zip(dirs, states):
        s_scr[...] = st


def _hg_body(cst_ref, tri_ref, qf_ref, ff_ref, vf_ref, qb_ref, fb_ref, vb_ref,
             cff_ref, cfb_ref, cv_ref, lb_ref, of_ref, ob_ref,
             sf_scr, sb_scr, b_scr, sc_scr, vv_scr, u_scr, dec_scr, *, n_chunks, n_ctx_chunks):
    lb_f = lb_ref[0, 0:1, :]
    lb_b = lb_ref[0, 1:2, :]
    scr = (sc_scr, vv_scr, u_scr, dec_scr)
    load = lambda ref, rows: (lambda: ref[0, 0, rows, :].astype(F32))

    @pl.when(pl.program_id(2) == 0)
    def _():
        sf_scr[...] = jnp.zeros_like(sf_scr)
        sb_scr[...] = jnp.zeros_like(sb_scr)
        n = n_ctx_chunks
        for c0 in range(0, n, HG_GROUP):
            items = []
            for c in range(c0, min(c0 + HG_GROUP, n)):
                sl = slice(c * CHUNK, (c + 1) * CHUNK)
                items.append((c, None, load(cff_ref, sl), load(cv_ref, sl), lb_f, False))
                items.append((n + c, None, load(cfb_ref, sl), load(cv_ref, sl), lb_b, True))
            _hg_prepare(items, False, cst_ref, tri_ref, b_scr, *scr)
        _hg_recur(range(n), [(sf_scr, lambda c: c, None), (sb_scr, lambda c: 2 * n - 1 - c, None)],
                  False, *scr)

    n = n_chunks
    rows = lambda c: slice(c * CHUNK, (c + 1) * CHUNK)

    def prepare(g):
        items = []
        for p in range(g * HG_GROUP, (g + 1) * HG_GROUP):
            rf, rb = rows(p), rows(n - 1 - p)
            items.append((p, load(qf_ref, rf), load(ff_ref, rf), load(vf_ref, rf), lb_f, False))
            items.append((2 * n - 1 - p, load(qb_ref, rb), load(fb_ref, rb), load(vb_ref, rb), lb_b, True))
        _hg_prepare(items, True, cst_ref, tri_ref, b_scr, *scr)

    def store_f(c, o):
        of_ref[0, 0, rows(c), :] = o.astype(of_ref.dtype)

    def store_b(c, o):
        ob_ref[0, 0, rows(n - 1 - c), :] = o.astype(ob_ref.dtype)

    def recur(g):
        _hg_recur(range(g * HG_GROUP, (g + 1) * HG_GROUP),
                  [(sf_scr, lambda c: c, store_f), (sb_scr, lambda c: 2 * n - 1 - c, store_b)], True, *scr)

    groups = n // HG_GROUP
    prepare(0)
    for g in range(1, groups):
        recur(g - 1)
        prepare(g)
    recur(groups - 1)


def _hgrn2_scan(yh, yh_ctx, lb, tb):
    bsz, _, length, _ = yh.shape
    lc = yh_ctx.shape[2] // bsz
    nt = length // tb
    nc, ncc = tb // CHUNK, lc // CHUNK
    assert nc % HG_GROUP == 0
    slots = 2 * max(nc, ncc)
    blk = (1, 1, tb, HEAD_DIM)
    cblk = (1, 1, lc, HEAD_DIM)
    fwd = lambda off: pl.BlockSpec(blk, lambda b, h, t: (b, off + h, t, 0))
    bwd = lambda off: pl.BlockSpec(blk, lambda b, h, t: (b, off + h, nt - 1 - t, 0))
    ctx = lambda off: pl.BlockSpec(cblk, lambda b, h, t: (0, off + h, b, 0))
    consts = _hg_constants()
    tri3 = _hg_triangles()
    out_shape = jax.ShapeDtypeStruct((bsz, N_HEADS, length, HEAD_DIM), BF16)
    return pl.pallas_call(
        functools.partial(_hg_body, n_chunks=nc, n_ctx_chunks=ncc),
        grid=(bsz, N_HEADS, nt),
        in_specs=[pl.BlockSpec(consts.shape, lambda b, h, t: (0, 0, 0)),
                  pl.BlockSpec(tri3.shape, lambda b, h, t: (0, 0, 0)),
                  fwd(OFF_AQ), fwd(OFF_AFF), fwd(OFF_AI), bwd(OFF_AQ), bwd(OFF_AFB), bwd(OFF_AI),
                  ctx(OFF_AFF), ctx(OFF_AFB), ctx(OFF_AI),
                  pl.BlockSpec((1, 2, HEAD_DIM), lambda b, h, t: (h, 0, 0))],
        out_specs=[pl.BlockSpec(blk, lambda b, h, t: (b, h, t, 0)),
                   pl.BlockSpec(blk, lambda b, h, t: (b, h, nt - 1 - t, 0))],
        out_shape=[out_shape, out_shape],
        scratch_shapes=[pltpu.VMEM((HEAD_DIM, HEAD_DIM), F32), pltpu.VMEM((HEAD_DIM, HEAD_DIM), F32),
                        pltpu.VMEM((2 * HG_GROUP, CHUNK, HEAD_DIM), F32),
                        pltpu.VMEM((slots, CHUNK, 2 * HEAD_DIM), BF16),
                        pltpu.VMEM((slots, HEAD_DIM, HEAD_DIM), BF16),
                        pltpu.VMEM((slots, HEAD_DIM, HEAD_DIM), F32),
                        pltpu.VMEM((slots, 1, HEAD_DIM), F32)],
        compiler_params=pltpu.CompilerParams(
            dimension_semantics=("parallel", "parallel", "arbitrary"),
            vmem_limit_bytes=VMEM_LIMIT),
    )(consts, tri3, yh, yh, yh, yh, yh, yh, yh_ctx, yh_ctx, yh_ctx, lb)


def _seg_cumsum_lanes(x, rev, lane):
    total = x.shape[1]
    seg = lane & (CHUNK - 1)
    for sh in (1, 2, 4, 8, 16, 32):
        if rev:
            x = x + jnp.where(seg < CHUNK - sh, pltpu.roll(x, total - sh, 1), 0.0)
        else:
            x = x + jnp.where(seg >= sh, pltpu.roll(x, sh, 1), 0.0)
    return x


def _gate_rows_body(g_ref, p_ref, o_ref):
    nrow = 2 * N_HEADS
    total = g_ref.shape[2]
    lane = lax.broadcasted_iota(jnp.int32, (nrow, total), 1)
    row = lax.broadcasted_iota(jnp.int32, (nrow, total), 0)
    z = g_ref[0, :nrow, :] + p_ref[:, 1:2]
    softplus = jnp.maximum(z, 0.0) + jnp.log1p(jnp.exp(-jnp.abs(z)))
    g = -jnp.exp(p_ref[:, 0:1]) * softplus
    prefix = _seg_cumsum_lanes(g, False, lane)
    suffix = _seg_cumsum_lanes(g, True, lane)
    o_ref[0, 0] = jnp.where(row < N_HEADS, prefix, suffix)
    o_ref[0, 1] = _sigmoid(g_ref[0, nrow:, :])
    o_ref[0, 2] = prefix + suffix - g


def _gate_rows(gates, params):
    bsz, _, length = gates.shape
    tg = min(2048, length)
    nrow = 2 * N_HEADS
    return pl.pallas_call(
        _gate_rows_body,
        grid=(bsz, length // tg),
        in_specs=[pl.BlockSpec((1, N_GATE, tg), lambda b, t: (b, 0, t)),
                  pl.BlockSpec(params.shape, lambda b, t: (0, 0))],
        out_specs=pl.BlockSpec((1, 3, nrow, tg), lambda b, t: (b, 0, 0, t)),
        out_shape=jax.ShapeDtypeStruct((bsz, 3, nrow, length), F32),
        compiler_params=pltpu.CompilerParams(dimension_semantics=("parallel", "parallel"),
                                             vmem_limit_bytes=VMEM_LIMIT),
    )(gates, params)


def _gdn_gates(g_ref, d, head):
    r = pl.ds(d * N_HEADS + head, 1)
    b = g_ref[0, 0, r, :]
    rows = jnp.concatenate([b, g_ref[0, 1, r, :], g_ref[0, 2, r, :]], axis=0)
    return rows, jnp.broadcast_to(b, (HEAD_DIM, b.shape[1])).T


GDN_QUAD = 4
GDN_PREP_HEADS = 4


def _gdn_prepare(quads, readout, qkg_scr, tb_scr, kp_scr, dec_scr):
    wide = GDN_QUAD * CHUNK
    t_i = lax.broadcasted_iota(jnp.int32, (CHUNK, wide), 0)
    lane = lax.broadcasted_iota(jnp.int32, (CHUNK, wide), 1)
    s_i = lane & (CHUNK - 1)
    blk = [(lane >= j * CHUNK) & (lane < (j + 1) * CHUNK) for j in range(GDN_QUAD)]
    eye = (t_i == s_i).astype(F32)
    tri = {False: (s_i <= t_i, s_i < t_i), True: (s_i >= t_i, s_i > t_i)}
    rows_of = lambda j: slice(j * CHUNK, (j + 1) * CHUNK)

    def diag_blocks(m, off):
        out = jnp.where(blk[0], m[off:off + CHUNK], 0.0)
        for j in range(1, GDN_QUAD):
            out = jnp.where(blk[j], m[off + j * CHUNK:off + (j + 1) * CHUNK], out)
        return out

    def block_diag(y):
        return jnp.concatenate([jnp.where(blk[j], y, 0.0).astype(BF16) for j in range(GDN_QUAD)], axis=0)

    dmats, qks, xs, ys = [], [], [], []
    for _, load_q, load_k, rows, bcol, rev in quads:
        incl, strict = tri[rev]
        bc = jnp.concatenate([bcol, bcol], axis=1)
        dmat = jnp.exp(jnp.where(incl, diag_blocks(bc, 0) - rows[0:1, :], NEG_BIG))
        kb = load_k()
        if readout:
            qkk = _dot_nt(jnp.concatenate([load_q(), kb], axis=0), kb)
            qks.append(diag_blocks(qkk, 0))
            kk = diag_blocks(qkk, wide)
        else:
            kk = diag_blocks(_dot_nt(kb, kb), 0)
        n_mat = jnp.where(strict, kk * dmat, 0.0) * rows[1:2, :]
        dmats.append(dmat)
        xs.append(eye - n_mat)
        ys.append(n_mat)
    ys = [_dot(y.astype(BF16), block_diag(y)) for y in ys]
    for _ in range(4):
        xy = [_dot(jnp.concatenate([x, y], axis=0).astype(BF16), block_diag(y)) for x, y in zip(xs, ys)]
        xs = [x + p[:CHUNK] for x, p in zip(xs, xy)]
        ys = [p[CHUNK:] for p in xy]
    xs = [x + _dot(x.astype(BF16), block_diag(y)) for x, y in zip(xs, ys)]

    for i, (slot0, load_q, load_k, rows, bcol, rev) in enumerate(quads):
        k_all = load_k().astype(F32)
        e_col = jnp.exp(bcol)
        qkg_all_k = (k_all * e_col).astype(BF16)
        if readout:
            qkg_all_q = (load_q().astype(F32) * e_col).astype(BF16)
            pm_all = qks[i] * dmats[i] * rows[1:2, :]
        k_t = k_all.T
        kdt_all = (k_t * (jnp.exp(rows[2:3, :] - rows[0:1, :]) * rows[1:2, :])).astype(BF16)
        for j in range(GDN_QUAD):
            slot = slot0 + j
            shift = (wide - j * CHUNK) % wide
            take = lambda m: (m if shift == 0 else pltpu.roll(m, shift, 1))[:, :CHUNK]
            tb_scr[slot] = take(xs[i]).astype(BF16)
            qkg_scr[slot, CHUNK:, :] = qkg_all_k[rows_of(j)]
            if readout:
                qkg_scr[slot, :CHUNK, :] = qkg_all_q[rows_of(j)]
                kp_scr[slot, HEAD_DIM:, :] = take(pm_all).astype(BF16)
            kp_scr[slot, :HEAD_DIM, :] = kdt_all[:, j * CHUNK:(j + 1) * CHUNK]
            tot = rows[2:3, j * CHUNK:(j + 1) * CHUNK]
            dec_scr[slot] = jnp.exp(jnp.concatenate([tot, tot], axis=1))


def _gdn_recur(steps, dirs, readout, qkg_scr, tb_scr, kp_scr, dec_scr):
    def step(c, carry):
        slots = [slot_of(c) for _, slot_of, _, _ in dirs]
        states = [s_ref[...] for s_ref, _, _, _ in dirs]
        if readout:
            qks = [_dot(qkg_scr[slot], s.astype(BF16)) for slot, s in zip(slots, states)]
            kss = [p[CHUNK:] for p in qks]
        else:
            kss = [_dot(qkg_scr[slot, CHUNK:, :], s.astype(BF16)) for slot, s in zip(slots, states)]
        ws = [_dot(tb_scr[slot], (load_v(c) - ks).astype(BF16)).astype(BF16)
              for slot, ks, (_, _, load_v, _) in zip(slots, kss, dirs)]
        if readout:
            res = [_dot(kp_scr[slot], w) for slot, w in zip(slots, ws)]
            for slot, s, r, p, (s_ref, _, _, store_o) in zip(slots, states, res, qks, dirs):
                s_ref[...] = s * dec_scr[slot] + r[:HEAD_DIM]
                store_o(c, p[:CHUNK] + r[HEAD_DIM:])
        else:
            for slot, s, w, (s_ref, _, _, _) in zip(slots, states, ws, dirs):
                s_ref[...] = s * dec_scr[slot] + _dot(kp_scr[slot, :HEAD_DIM, :], w)
        return carry

    if isinstance(steps, int):
        lax.fori_loop(0, steps, step, 0, unroll=4)
    else:
        for c in steps:
            step(c, 0)


def _gdn_body(qf_ref, kf_ref, vf_ref, gf_ref, qb_ref, kb_ref, vb_ref, gb_ref,
              ck_ref, cv_ref, cg_ref, of_ref, ob_ref,
              s_scr, qkg_scr, tb_scr, kp_scr, dec_scr,
              *, n_chunks, n_ctx_chunks, heads, n_blocks):
    t = pl.program_id(2)
    h0 = pl.program_id(1) * heads
    scr = (qkg_scr, tb_scr, kp_scr, dec_scr)
    set_slots = 2 * heads * n_chunks

    def rows(c):
        return slice(c * CHUNK, (c + 1) * CHUNK) if isinstance(c, int) else pl.ds(pl.multiple_of(c * CHUNK, CHUNK), CHUNK)

    def chunk_list(hh, n, base, q_refs, k_refs, g_refs):
        head = h0 + hh
        wide = GDN_QUAD * CHUNK
        out = []
        for d in (0, 1):
            gate_rows, bcol = _gdn_gates(g_refs[d], d, head)
            for c0 in range(0, n, GDN_QUAD):
                sl = slice(c0 * CHUNK, c0 * CHUNK + wide)
                load_q = None if q_refs is None else (lambda r=q_refs[d], sl=sl: r[0, hh, sl, :])
                load_k = lambda r=k_refs[d], sl=sl: r[0, hh, sl, :]
                out.append((base + (2 * hh + d) * n + c0, load_q, load_k, gate_rows[:, sl], bcol[sl, :], d == 1))
        return out

    def prepare(hp, n, base, q_refs, k_refs, g_refs, readout):
        quads = []
        for i in range(GDN_PREP_HEADS):
            quads += chunk_list(hp * GDN_PREP_HEADS + i, n, base, q_refs, k_refs, g_refs)
        _gdn_prepare(quads, readout, *scr)

    def chains(n, base, vf, vb, of, ob):
        out = []
        for hh in range(heads):
            out.append((s_scr.at[2 * hh], lambda c, hh=hh: base + 2 * hh * n + c,
                        lambda c, hh=hh: vf[0, hh, rows(c), :].astype(F32),
                        None if of is None else (lambda c, o, hh=hh: of.__setitem__(
                            (0, hh, rows(c), slice(None)), o.astype(of.dtype)))))
            out.append((s_scr.at[2 * hh + 1], lambda c, hh=hh: base + (2 * hh + 1) * n + (n - 1 - c),
                        lambda c, hh=hh: vb[0, hh, rows(n - 1 - c), :].astype(F32),
                        None if ob is None else (lambda c, o, hh=hh: ob.__setitem__(
                            (0, hh, rows(n - 1 - c), slice(None)), o.astype(ob.dtype)))))
        return out

    main_refs = ((qf_ref, qb_ref), (kf_ref, kb_ref), (gf_ref, gb_ref))
    groups = heads // GDN_PREP_HEADS
    n = n_chunks

    @pl.when(t == 0)
    def _():
        s_scr[...] = jnp.zeros_like(s_scr)
        nc = n_ctx_chunks

        def ctx_head(hp, carry):
            prepare(hp, nc, 0, None, (ck_ref, ck_ref), (cg_ref, cg_ref), False)
            return carry

        lax.fori_loop(0, groups, ctx_head, 0)
        _gdn_recur(nc, chains(nc, 0, cv_ref, cv_ref, None, None), False, *scr)

        def first_head(hp, carry):
            prepare(hp, n, 0, *main_refs, True)
            return carry

        lax.fori_loop(0, groups, first_head, 0)

    @pl.when((t > 0) & (t < n_blocks))
    def _():
        base_p = (t & 1) * set_slots
        base_r = set_slots - base_p
        per = n // groups
        for g in range(groups):
            _gdn_recur(range(g * per, (g + 1) * per), chains(n, base_r, vf_ref, vb_ref, of_ref, ob_ref), True, *scr)
            prepare(g, n, base_p, *main_refs, True)

    @pl.when(t == n_blocks)
    def _():
        base_r = ((n_blocks - 1) & 1) * set_slots
        _gdn_recur(n, chains(n, base_r, vf_ref, vb_ref, of_ref, ob_ref), True, *scr)


GDN_HEADS = 8


def _gdn_scan(qkv, gates, qkv_ctx, gates_ctx, tb):
    bsz, _, length, _ = qkv.shape
    lc = qkv_ctx.shape[2] // bsz
    nt = length // tb
    nc, ncc = tb // CHUNK, lc // CHUNK
    hb = GDN_HEADS
    assert nc % GDN_QUAD == 0 and ncc % GDN_QUAD == 0 and hb % GDN_PREP_HEADS == 0
    assert nc % (hb // GDN_PREP_HEADS) == 0 and ncc <= 2 * nc
    slots = 2 * 2 * hb * nc
    blk = (1, hb, tb, HEAD_DIM)
    cblk = (1, hb, lc, HEAD_DIM)
    gblk = (1, 3, 2 * N_HEADS, tb)
    prep = lambda t: jnp.minimum(t, nt - 1)
    scan = lambda t: jnp.maximum(t - 1, 0)
    fwd = lambda off, tt: pl.BlockSpec(blk, lambda b, h, t: (b, off // hb + h, tt(t), 0))
    bwd = lambda off, tt: pl.BlockSpec(blk, lambda b, h, t: (b, off // hb + h, nt - 1 - tt(t), 0))
    ctx = lambda off: pl.BlockSpec(cblk, lambda b, h, t: (0, off // hb + h, b, 0))
    out_shape = jax.ShapeDtypeStruct((bsz, N_HEADS, length, HEAD_DIM), BF16)
    return pl.pallas_call(
        functools.partial(_gdn_body, n_chunks=nc, n_ctx_chunks=ncc, heads=hb, n_blocks=nt),
        grid=(bsz, N_HEADS // hb, nt + 1),
        in_specs=[fwd(0, prep), fwd(N_HEADS, prep), fwd(2 * N_HEADS, scan),
                  pl.BlockSpec(gblk, lambda b, h, t: (b, 0, 0, prep(t))),
                  bwd(0, prep), bwd(N_HEADS, prep), bwd(2 * N_HEADS, scan),
                  pl.BlockSpec(gblk, lambda b, h, t: (b, 0, 0, nt - 1 - prep(t))),
                  ctx(N_HEADS), ctx(2 * N_HEADS),
                  pl.BlockSpec((1, 3, 2 * N_HEADS, lc), lambda b, h, t: (0, 0, 0, b))],
        out_specs=[pl.BlockSpec(blk, lambda b, h, t: (b, h, scan(t), 0)),
                   pl.BlockSpec(blk, lambda b, h, t: (b, h, nt - 1 - scan(t), 0))],
        out_shape=[out_shape, out_shape],
        scratch_shapes=[pltpu.VMEM((2 * hb, HEAD_DIM, HEAD_DIM), F32),
                        pltpu.VMEM((slots, 2 * CHUNK, HEAD_DIM), BF16),
                        pltpu.VMEM((slots, CHUNK, CHUNK), BF16),
                        pltpu.VMEM((slots, HEAD_DIM + CHUNK, CHUNK), BF16),
                        pltpu.VMEM((slots, 1, HEAD_DIM), F32)],
        compiler_params=pltpu.CompilerParams(
            dimension_semantics=("parallel", "parallel", "arbitrary"),
            vmem_limit_bytes=VMEM_LIMIT),
    )(qkv, qkv, qkv, gates, qkv, qkv, qkv, gates, qkv_ctx, qkv_ctx, gates_ctx)


def _out_body(oaf_ref, oab_ref, obf_ref, obb_ref, za_ref, zb_ref, naw_ref, nbw_ref,
              w_ref, x_ref, gate_ref, fw_ref, o_ref):
    parts = []
    for of_ref, ob_ref, z_ref, nw_ref in ((oaf_ref, oab_ref, za_ref, naw_ref),
                                          (obf_ref, obb_ref, zb_ref, nbw_ref)):
        for h in range(N_HEADS):
            o = of_ref[0, h].astype(F32) + ob_ref[0, h].astype(F32)
            o = o * lax.rsqrt(jnp.mean(o * o, axis=-1, keepdims=True) + NORM_EPS) * nw_ref[h:h + 1, :]
            hz = 0.5 * z_ref[0, h].astype(F32)
            parts.append(((hz + hz * jnp.tanh(hz)) * o).astype(BF16))
    y = jnp.concatenate(parts, axis=1)
    xo = x_ref[0] + gate_ref[0] * _dot(y, w_ref[...])
    ms = jnp.mean(xo * xo, axis=-1, keepdims=True)
    o_ref[0] = xo * lax.rsqrt(ms + NORM_EPS) * fw_ref[...]


def _out_stage(oa_f, oa_b, ob_f, ob_b, yh, na_w, nb_w, w_out, x, gate, final_w, tm):
    bsz, length, d = x.shape
    hblk = (1, N_HEADS, tm, HEAD_DIM)
    ospec = pl.BlockSpec(hblk, lambda b, m: (b, 0, m, 0))
    full2 = lambda a: pl.BlockSpec(a.shape, lambda b, m: (0, 0))
    return pl.pallas_call(
        _out_body,
        grid=(bsz, length // tm),
        in_specs=[ospec, ospec, ospec, ospec,
                  pl.BlockSpec(hblk, lambda b, m: (b, OFF_AZ // N_HEADS, m, 0)),
                  pl.BlockSpec(hblk, lambda b, m: (b, OFF_BZ // N_HEADS, m, 0)),
                  full2(na_w), full2(nb_w), full2(w_out),
                  pl.BlockSpec((1, tm, d), lambda b, m: (b, m, 0)),
                  pl.BlockSpec((1, 1, d), lambda b, m: (b, 0, 0)),
                  full2(final_w)],
        out_specs=pl.BlockSpec((1, tm, d), lambda b, m: (b, m, 0)),
        out_shape=jax.ShapeDtypeStruct((bsz, length, d), F32),
        compiler_params=pltpu.CompilerParams(dimension_semantics=("parallel", "parallel"),
                                             vmem_limit_bytes=VMEM_LIMIT),
    )(oa_f, oa_b, ob_f, ob_b, yh, yh, na_w, nb_w, w_out, x, gate, final_w)


def kernel(x, c, ctx, c_ctx, norm_w, ada_w, ada_b, w_in, conv_w, hg_lb_logits, gdn_a_log,
           gdn_dt_bias, ha_norm_w, hb_norm_w, w_out, final_norm_w):
    bsz, length, d = x.shape
    lc = ctx.shape[1]
    assert d == D_MODEL and length % 512 == 0 and length % GRID_W == 0 and lc % CHUNK == 0
    assert w_in.shape[0] == 1, "single-layer block"

    n_cond = bsz + 1
    cond = jnp.concatenate([c, c_ctx[None, :], jnp.zeros((-n_cond % 8, d), F32)], axis=0)
    mod = _adaln(cond, ada_w[0], ada_b[0])
    shift, scale, gate = mod[:, :d], mod[:, d:2 * d], mod[:, 2 * d:]
    lat = lambda m: m[:bsz, None, :]
    of_ctx = lambda m: m[bsz:bsz + 1, None, :]

    w_main = w_in[0].astype(BF16)
    w_gate_t = w_main[:, N_MAIN:].T
    nw = norm_w[0].reshape(1, d)
    yh, gates = _inproj(x, nw, lat(scale), lat(shift), w_main, w_gate_t, tm=min(2048, length))
    yh_c, gates_c = _inproj(ctx.reshape(1, bsz * lc, d), nw, of_ctx(scale), of_ctx(shift), w_main, w_gate_t,
                            tm=bsz * lc)

    lb = jax.nn.softmax(hg_lb_logits.astype(F32), axis=0)[0]
    lb = lb.reshape(2, N_HEADS, HEAD_DIM).transpose(1, 0, 2)
    tb = min(512, length)
    oa_f, oa_b = _hgrn2_scan(yh, yh_c, lb, min(2048, length))

    cw = conv_w[0].reshape(9, 3 * N_HEADS, HEAD_DIM).transpose(1, 0, 2)
    qkv = _gdn_conv(yh, cw, two_d=True)
    qkv_c = _gdn_conv(yh_c, cw, two_d=False, width=lc)
    params = jnp.stack([gdn_a_log[0].reshape(-1), gdn_dt_bias[0].reshape(-1)], axis=1).astype(F32)
    ob_f, ob_b = _gdn_scan(qkv, _gate_rows(gates, params), qkv_c, _gate_rows(gates_c, params), tb)

    return _out_stage(oa_f, oa_b, ob_f, ob_b, yh, ha_norm_w[0], hb_norm_w[0],
                      w_out[0].astype(BF16), x, lat(gate), final_norm_w.reshape(1, d),
                      tm=min(512, length))
```

```python
import functools

import jax
import jax.numpy as jnp
import numpy as np
from jax import lax
from jax.experimental import pallas as pl
from jax.experimental.pallas import tpu as pltpu

F32 = jnp.float32
BF16 = jnp.bfloat16

D_MODEL = 1024
N_HEADS = 8
HEAD_DIM = 128
CHUNK = 64
GRID_W = 64
NORM_EPS = 1e-6
N_MAIN = 72 * HEAD_DIM
N_GATE = 4 * N_HEADS
OFF_AQ, OFF_AFF, OFF_AFB, OFF_AI, OFF_AZ, OFF_BQ, OFF_BZ = 0, 8, 16, 24, 32, 40, 64
NEG_BIG = -1e30
F32_TINY = float(np.finfo(np.float32).tiny)
VMEM_LIMIT = 56 * 1024 * 1024
IN_TM, IN_TN = 2048, 1024
HG_TB, GDN_TB = 4096, 512
CONV_RT = 512
OUT_TM = 512
GATE_TG = 2048


def _dot(a, b):
    return jnp.dot(a, b, preferred_element_type=F32)


def _dot_nt(a, b):
    return lax.dot_general(a, b, (((1,), (1,)), ((), ())), preferred_element_type=F32)


def _sigmoid(x):
    return 1.0 / (1.0 + jnp.exp(-x))


def _silu(x):
    return x * _sigmoid(x)


def _adaln_body(c_ref, w_ref, b_ref, o_ref):
    o_ref[...] = _dot(_silu(c_ref[...]), w_ref[...]) + b_ref[...]


def _adaln(cond, ada_w, ada_b):
    rows, d = cond.shape
    n = ada_w.shape[1]
    tn = 1024
    return pl.pallas_call(
        _adaln_body,
        grid=(n // tn,),
        in_specs=[pl.BlockSpec((rows, d), lambda j: (0, 0)),
                  pl.BlockSpec((d, tn), lambda j: (0, j)),
                  pl.BlockSpec((1, tn), lambda j: (0, j))],
        out_specs=pl.BlockSpec((rows, tn), lambda j: (0, j)),
        out_shape=jax.ShapeDtypeStruct((rows, n), F32),
        compiler_params=pltpu.CompilerParams(dimension_semantics=("arbitrary",),
                                             vmem_limit_bytes=VMEM_LIMIT),
    )(cond, ada_w, ada_b.reshape(1, n))


def _inproj_body(x_ref, nw_ref, sc_ref, sh_ref, w_ref, wg_ref, y_ref, yg_ref, h_scr, *, tn):
    @pl.when(pl.program_id(2) == 0)
    def _():
        x = x_ref[0]
        ms = jnp.mean(x * x, axis=-1, keepdims=True)
        h = x * lax.rsqrt(ms + NORM_EPS) * nw_ref[...]
        h = (h * (1.0 + sc_ref[0]) + sh_ref[0]).astype(BF16)
        h_scr[...] = h
        yg_ref[0] = _dot_nt(wg_ref[...], h)

    acc = _dot(h_scr[...], w_ref[...])
    for j in range(tn // HEAD_DIM):
        y_ref[0, j] = acc[:, j * HEAD_DIM:(j + 1) * HEAD_DIM].astype(y_ref.dtype)


def _inproj(x, norm_w, scale, shift, w_main, w_gate_t, tm):
    bsz, length, d = x.shape
    tn = IN_TN
    grid = (bsz, length // tm, N_MAIN // tn)
    return pl.pallas_call(
        functools.partial(_inproj_body, tn=tn),
        grid=grid,
        in_specs=[pl.BlockSpec((1, tm, d), lambda b, m, n: (b, m, 0)),
                  pl.BlockSpec((1, d), lambda b, m, n: (0, 0)),
                  pl.BlockSpec((1, 1, d), lambda b, m, n: (b, 0, 0)),
                  pl.BlockSpec((1, 1, d), lambda b, m, n: (b, 0, 0)),
                  pl.BlockSpec((d, tn), lambda b, m, n: (0, n)),
                  pl.BlockSpec((N_GATE, d), lambda b, m, n: (0, 0))],
        out_specs=[pl.BlockSpec((1, tn // HEAD_DIM, tm, HEAD_DIM), lambda b, m, n: (b, n, m, 0)),
                   pl.BlockSpec((1, N_GATE, tm), lambda b, m, n: (b, 0, m))],
        out_shape=[jax.ShapeDtypeStruct((bsz, N_MAIN // HEAD_DIM, length, HEAD_DIM), BF16),
                   jax.ShapeDtypeStruct((bsz, N_GATE, length), F32)],
        scratch_shapes=[pltpu.VMEM((tm, d), BF16)],
        compiler_params=pltpu.CompilerParams(
            dimension_semantics=("parallel", "parallel", "arbitrary"),
            vmem_limit_bytes=VMEM_LIMIT),
    )(x, norm_w, scale, shift, w_main, w_gate_t)


CONV_PAD = 72
CONV_HALO = 8


def _conv_body(x_ref, w_ref, o_ref, pad_scr, *, length, width, two_d, rt):
    blk = pl.program_id(1)
    zeros = jnp.zeros((CONV_PAD, HEAD_DIM), F32)
    pad_scr[0:CONV_PAD, :] = zeros
    pad_scr[CONV_PAD + length:CONV_PAD + length + CONV_PAD, :] = zeros
    pad_scr[CONV_PAD:CONV_PAD + length, :] = x_ref[0, 0].astype(F32)
    w = 0.5 * w_ref[0]
    win_rows = rt + 2 * CONV_HALO
    is_q = blk < N_HEADS
    is_qk = blk < 2 * N_HEADS
    col = lax.broadcasted_iota(jnp.int32, (rt, HEAD_DIM), 0) & (width - 1)

    def tile(i, carry):
        s = pl.multiple_of(i * rt, rt)
        sums = [None, None, None]
        for dr in ((-1, 0, 1) if two_d else (0,)):
            base = pl.multiple_of(s + (CONV_PAD + dr * width - CONV_HALO), 8)
            win = pad_scr[pl.ds(base, win_rows), :]
            for j in range(3):
                tap = (dr + 1) * 3 + j
                term = win * w[tap:tap + 1, :]
                sums[j] = term if sums[j] is None else sums[j] + term
        inner = slice(CONV_HALO, CONV_HALO + rt)
        left = pltpu.roll(sums[0], 1, 0)[inner]
        right = pltpu.roll(sums[2], win_rows - 1, 0)[inner]
        hs = sums[1][inner] + jnp.where(col >= 1, left, 0.0) + jnp.where(col <= width - 2, right, 0.0)
        a = hs + hs * jnp.tanh(hs)
        nrm = lax.rsqrt(jnp.sum(a * a, axis=-1, keepdims=True) + NORM_EPS)
        f = jnp.where(is_q, nrm * HEAD_DIM ** -0.5, jnp.where(is_qk, nrm, 1.0))
        o_ref[0, 0, pl.ds(s, rt), :] = (a * f).astype(o_ref.dtype)
        return carry

    lax.fori_loop(0, length // rt, tile, 0)


def _gdn_conv(yh, conv_w, two_d, width=GRID_W):
    bsz, _, length, _ = yh.shape
    assert width & (width - 1) == 0, "grid width must be a power of two (column index by bit mask)"
    rt = min(CONV_RT, length)
    assert rt % width == 0
    nblk = 3 * N_HEADS
    return pl.pallas_call(
        functools.partial(_conv_body, length=length, width=width, two_d=two_d, rt=rt),
        grid=(bsz, nblk),
        in_specs=[pl.BlockSpec((1, 1, length, HEAD_DIM), lambda b, j: (b, OFF_BQ + j, 0, 0)),
                  pl.BlockSpec((1, 9, HEAD_DIM), lambda b, j: (j, 0, 0))],
        out_specs=pl.BlockSpec((1, 1, length, HEAD_DIM), lambda b, j: (b, j, 0, 0)),
        out_shape=jax.ShapeDtypeStruct((bsz, nblk, length, HEAD_DIM), BF16),
        scratch_shapes=[pltpu.VMEM((length + 2 * CONV_PAD, HEAD_DIM), F32)],
        compiler_params=pltpu.CompilerParams(dimension_semantics=("parallel", "parallel"),
                                             vmem_limit_bytes=VMEM_LIMIT),
    )(yh, conv_w)


HG_LEVELS = (32, 16, 8, 4, 2, 1)
HG_PAIRS = ((0, 1), (2, 3), (4, 5))
HG_GROUP = 4


def _hg_constants():
    t = np.arange(CHUNK)[:, None]
    col = np.arange(HEAD_DIM)[None, :]
    s = col % CHUNK
    out = [np.broadcast_to(np.where((t & m) != 0, 1.0, -1.0), (CHUNK, HEAD_DIM)) for m in HG_LEVELS]
    for rev in (False, True):
        for pa, pb in HG_PAIRS:
            keep = np.zeros((CHUNK, HEAD_DIM), bool)
            for half, lv in ((col < CHUNK, pa), (col >= CHUNK, pb)):
                m = HG_LEVELS[lv]
                split = ((t ^ s) >> (m.bit_length() - 1)) == 1
                t_hi = (t & m) != 0
                keep |= half & split & (~t_hi if rev else t_hi)
            out.append(keep)
    out.append((col == t) & (col < CHUNK))
    return jnp.asarray(np.stack([np.asarray(o, np.float32) for o in out]))


def _hg_triangles():
    t = np.arange(CHUNK)
    lower = (t[None, :] <= t[:, None]).astype(np.float32)
    return jnp.asarray(np.stack([np.tile(lower, (1, 3)), np.tile(lower.T, (1, 3))]), BF16)


def _cumsum_rows(g, tri3):
    hi = g.astype(BF16)
    r1 = g - hi.astype(F32)
    mid = r1.astype(BF16)
    lo = (r1 - mid.astype(F32)).astype(BF16)
    return _dot(tri3, jnp.concatenate([hi, mid, lo], axis=0))


def _level_operand(b, b_ref, q, k, m, rev):
    parts = []
    for blk in range(CHUNK // m):
        rows = slice(blk * m, (blk + 1) * m)
        r = (blk // 2) * 2 * m + m
        ref = jnp.broadcast_to(b_ref[r:r + 1, :], (m, HEAD_DIM))
        q_side = (blk % 2 == 1) != rev
        e = (b[rows] - ref) if q_side else (ref - b[rows])
        parts.append(((q if q_side else k)[rows] * jnp.exp2(e)).astype(BF16))
    return jnp.concatenate(parts, axis=0)


def _level_ref(b_ref, m, sub):
    bc = lambda r, n: jnp.broadcast_to(b_ref[r:r + 1, :], (n, HEAD_DIM))
    if m >= 4:
        return jnp.concatenate([bc(blk * 2 * m + m, 2 * m) for blk in range(CHUNK // (2 * m))], axis=0)
    lo = jnp.concatenate([bc(v * 8 + 2, 8) for v in range(CHUNK // 8)], axis=0)
    hi = jnp.concatenate([bc(v * 8 + 6, 8) for v in range(CHUNK // 8)], axis=0)
    return jnp.where(sub < 4, lo, hi)


def _hg_prepare(items, readout, cst_ref, tri_ref, b_scr, sc_scr, vv_scr, u_scr, dec_scr):
    row = lax.broadcasted_iota(jnp.int32, (CHUNK, HEAD_DIM), 0)
    sub = row & 7
    even = (row & 1) == 0
    gs, ks, bs = [], [], []
    for i, (_, _, load_f, _, lb, rev) in enumerate(items):
        gate = (1.0 - lb) * (0.5 + 0.5 * jnp.tanh(0.5 * load_f()))
        g = jnp.log2(jnp.maximum(lb + gate, F32_TINY))
        gs.append(g)
        ks.append((1.0 - lb) - gate)
        b = _cumsum_rows(g, tri_ref[1 if rev else 0])
        bs.append(b)
        b_scr[i] = b
    for i, (slot, _, _, load_v, _, rev) in enumerate(items):
        b = bs[i]
        btot = b[0:1, :] if rev else b[CHUNK - 1:CHUNK, :]
        v = load_v()
        vv_t = jnp.concatenate([v, v], axis=0).T.astype(BF16)
        k_dec = (ks[i] * jnp.exp2(btot - b)).astype(BF16)
        u_scr[slot] = _dot(vv_t[:, :CHUNK], k_dec)
        dec_scr[slot] = jnp.exp2(btot)
        vv_scr[slot] = vv_t
    if not readout:
        return

    qs = []
    for _, load_q, _, _, _, _ in items:
        q_raw = load_q()
        hs = (0.5 * HEAD_DIM ** -0.5) * q_raw
        qs.append(hs + hs * jnp.tanh(0.5 * q_raw))
    scores = [jnp.zeros((CHUNK, HEAD_DIM), F32) for _ in items]
    zero_blk = jnp.zeros((CHUNK, HEAD_DIM), BF16)
    for pi, pair in enumerate(HG_PAIRS):
        gps = []
        for i, (_, _, _, _, _, rev) in enumerate(items):
            a_mats = []
            for lv in pair:
                m = HG_LEVELS[lv]
                if m >= 8:
                    a_mats.append(_level_operand(bs[i], b_scr.at[i], qs[i], ks[i], m, rev))
                    continue
                sign = cst_ref[lv]
                if m == 1:
                    x = jnp.exp2(jnp.where(even, gs[i] if rev else pltpu.roll(gs[i], CHUNK - 1, 0), 0.0))
                else:
                    d = bs[i] - _level_ref(b_scr.at[i], m, sub)
                    x = jnp.exp2((-d if rev else d) * sign)
                t_is_q = (sign < 0.0) if rev else (sign > 0.0)
                a_mats.append((jnp.where(t_is_q, qs[i], ks[i]) * x).astype(BF16))
            lhs = jnp.concatenate(a_mats, axis=1)
            rhs = jnp.concatenate([jnp.concatenate([a_mats[0], zero_blk], axis=1),
                                   jnp.concatenate([zero_blk, a_mats[1]], axis=1)], axis=0)
            gps.append(_dot_nt(lhs, rhs))
        for i, (_, _, _, _, _, rev) in enumerate(items):
            scores[i] = scores[i] + gps[i] * cst_ref[6 + (3 if rev else 0) + pi]
    for i, (slot, _, _, _, _, _) in enumerate(items):
        diag = jnp.sum(qs[i] * ks[i], axis=-1, keepdims=True) * cst_ref[12]
        sc_scr[slot, :, :HEAD_DIM] = (scores[i] + diag).astype(BF16)
        sc_scr[slot, :, HEAD_DIM:] = (qs[i] * jnp.exp2(bs[i])).astype(BF16)


def _hg_recur(steps, dirs, readout, sc_scr, vv_scr, u_scr, dec_scr):
    states = [s_scr[...] for s_scr, _, _ in dirs]
    for c in steps:
        for i, (_, slot_of, store_o) in enumerate(dirs):
            slot = slot_of(c)
            if readout:
                rhs = jnp.concatenate([vv_scr[slot], states[i].astype(BF16)], axis=1)
                store_o(c, _dot_nt(sc_scr[slot], rhs))
            states[i] = states[i] * dec_scr[slot] + u_scr[slot]
    for (s_scr, _, _), st in zip(dirs, states):
        s_scr[...] = st


def _hg_body(cst_ref, tri_ref, qf_ref, ff_ref, vf_ref, qb_ref, fb_ref, vb_ref,
             cff_ref, cfb_ref, cv_ref, lb_ref, of_ref, ob_ref,
             sf_scr, sb_scr, b_scr, sc_scr, vv_scr, u_scr, dec_scr, *, n_chunks, n_ctx_chunks):
    lb_f = lb_ref[0, 0:1, :]
    lb_b = lb_ref[0, 1:2, :]
    scr = (sc_scr, vv_scr, u_scr, dec_scr)
    load = lambda ref, rows: (lambda: ref[0, 0, rows, :].astype(F32))

    @pl.when(pl.program_id(2) == 0)
    def _():
        sf_scr[...] = jnp.zeros_like(sf_scr)
        sb_scr[...] = jnp.zeros_like(sb_scr)
        n = n_ctx_chunks
        for c0 in range(0, n, HG_GROUP):
            items = []
            for c in range(c0, min(c0 + HG_GROUP, n)):
                sl = slice(c * CHUNK, (c + 1) * CHUNK)
                items.append((c, None, load(cff_ref, sl), load(cv_ref, sl), lb_f, False))
                items.append((n + c, None, load(cfb_ref, sl), load(cv_ref, sl), lb_b, True))
            _hg_prepare(items, False, cst_ref, tri_ref, b_scr, *scr)
        _hg_recur(range(n), [(sf_scr, lambda c: c, None), (sb_scr, lambda c: 2 * n - 1 - c, None)],
                  False, *scr)

    n = n_chunks
    rows = lambda c: slice(c * CHUNK, (c + 1) * CHUNK)

    def prepare(g):
        items = []
        for p in range(g * HG_GROUP, (g + 1) * HG_GROUP):
            rf, rb = rows(p), rows(n - 1 - p)
            items.append((p, load(qf_ref, rf), load(ff_ref, rf), load(vf_ref, rf), lb_f, False))
            items.append((2 * n - 1 - p, load(qb_ref, rb), load(fb_ref, rb), load(vb_ref, rb), lb_b, True))
        _hg_prepare(items, True, cst_ref, tri_ref, b_scr, *scr)

    def store_f(c, o):
        of_ref[0, 0, rows(c), :] = o.astype(of_ref.dtype)

    def store_b(c, o):
        ob_ref[0, 0, rows(n - 1 - c), :] = o.astype(ob_ref.dtype)

    def recur(g):
        _hg_recur(range(g * HG_GROUP, (g + 1) * HG_GROUP),
                  [(sf_scr, lambda c: c, store_f), (sb_scr, lambda c: 2 * n - 1 - c, store_b)], True, *scr)

    groups = n // HG_GROUP
    prepare(0)
    for g in range(1, groups):
        recur(g - 1)
        prepare(g)
    recur(groups - 1)


def _hgrn2_scan(yh, yh_ctx, lb, tb):
    bsz, _, length, _ = yh.shape
    lc = yh_ctx.shape[2] // bsz
    nt = length // tb
    nc, ncc = tb // CHUNK, lc // CHUNK
    assert nc % HG_GROUP == 0
    slots = 2 * max(nc, ncc)
    blk = (1, 1, tb, HEAD_DIM)
    cblk = (1, 1, lc, HEAD_DIM)
    fwd = lambda off: pl.BlockSpec(blk, lambda b, h, t: (b, off + h, t, 0))
    bwd = lambda off: pl.BlockSpec(blk, lambda b, h, t: (b, off + h, nt - 1 - t, 0))
    ctx = lambda off: pl.BlockSpec(cblk, lambda b, h, t: (0, off + h, b, 0))
    consts = _hg_constants()
    tri3 = _hg_triangles()
    out_shape = jax.ShapeDtypeStruct((bsz, N_HEADS, length, HEAD_DIM), BF16)
    return pl.pallas_call(
        functools.partial(_hg_body, n_chunks=nc, n_ctx_chunks=ncc),
        grid=(bsz, N_HEADS, nt),
        in_specs=[pl.BlockSpec(consts.shape, lambda b, h, t: (0, 0, 0)),
                  pl.BlockSpec(tri3.shape, lambda b, h, t: (0, 0, 0)),
                  fwd(OFF_AQ), fwd(OFF_AFF), fwd(OFF_AI), bwd(OFF_AQ), bwd(OFF_AFB), bwd(OFF_AI),
                  ctx(OFF_AFF), ctx(OFF_AFB), ctx(OFF_AI),
                  pl.BlockSpec((1, 2, HEAD_DIM), lambda b, h, t: (h, 0, 0))],
        out_specs=[pl.BlockSpec(blk, lambda b, h, t: (b, h, t, 0)),
                   pl.BlockSpec(blk, lambda b, h, t: (b, h, nt - 1 - t, 0))],
        out_shape=[out_shape, out_shape],
        scratch_shapes=[pltpu.VMEM((HEAD_DIM, HEAD_DIM), F32), pltpu.VMEM((HEAD_DIM, HEAD_DIM), F32),
                        pltpu.VMEM((2 * HG_GROUP, CHUNK, HEAD_DIM), F32),
                        pltpu.VMEM((slots, CHUNK, 2 * HEAD_DIM), BF16),
                        pltpu.VMEM((slots, HEAD_DIM, HEAD_DIM), BF16),
                        pltpu.VMEM((slots, HEAD_DIM, HEAD_DIM), F32),
                        pltpu.VMEM((slots, 1, HEAD_DIM), F32)],
        compiler_params=pltpu.CompilerParams(
            dimension_semantics=("parallel", "parallel", "arbitrary"),
            vmem_limit_bytes=VMEM_LIMIT),
    )(consts, tri3, yh, yh, yh, yh, yh, yh, yh_ctx, yh_ctx, yh_ctx, lb)


def _seg_cumsum_lanes(x, rev, lane):
    total = x.shape[1]
    seg = lane & (CHUNK - 1)
    for sh in (1, 2, 4, 8, 16, 32):
        if rev:
            x = x + jnp.where(seg < CHUNK - sh, pltpu.roll(x, total - sh, 1), 0.0)
        else:
            x = x + jnp.where(seg >= sh, pltpu.roll(x, sh, 1), 0.0)
    return x


def _gate_rows_body(g_ref, p_ref, o_ref):
    nrow = 2 * N_HEADS
    total = g_ref.shape[2]
    lane = lax.broadcasted_iota(jnp.int32, (nrow, total), 1)
    row = lax.broadcasted_iota(jnp.int32, (nrow, total), 0)
    z = g_ref[0, :nrow, :] + p_ref[:, 1:2]
    softplus = jnp.maximum(z, 0.0) + jnp.log1p(jnp.exp(-jnp.abs(z)))
    g = -jnp.exp(p_ref[:, 0:1]) * softplus
    prefix = _seg_cumsum_lanes(g, False, lane)
    suffix = _seg_cumsum_lanes(g, True, lane)
    o_ref[0, 0] = jnp.where(row < N_HEADS, prefix, suffix)
    o_ref[0, 1] = _sigmoid(g_ref[0, nrow:, :])
    o_ref[0, 2] = prefix + suffix - g


def _gate_rows(gates, params):
    bsz, _, length = gates.shape
    tg = min(GATE_TG, length)
    nrow = 2 * N_HEADS
    return pl.pallas_call(
        _gate_rows_body,
        grid=(bsz, length // tg),
        in_specs=[pl.BlockSpec((1, N_GATE, tg), lambda b, t: (b, 0, t)),
                  pl.BlockSpec(params.shape, lambda b, t: (0, 0))],
        out_specs=pl.BlockSpec((1, 3, nrow, tg), lambda b, t: (b, 0, 0, t)),
        out_shape=jax.ShapeDtypeStruct((bsz, 3, nrow, length), F32),
        compiler_params=pltpu.CompilerParams(dimension_semantics=("parallel", "parallel"),
                                             vmem_limit_bytes=VMEM_LIMIT),
    )(gates, params)


def _gdn_gates(g_ref, d, head):
    r = pl.ds(d * N_HEADS + head, 1)
    b = g_ref[0, 0, r, :]
    rows = jnp.concatenate([b, g_ref[0, 1, r, :], g_ref[0, 2, r, :]], axis=0)
    return rows, jnp.broadcast_to(b, (HEAD_DIM, b.shape[1])).T


GDN_QUAD = 4
GDN_PREP_HEADS = 4


def _gdn_prepare(quads, readout, qkg_scr, tb_scr, kp_scr, dec_scr):
    wide = GDN_QUAD * CHUNK
    t_i = lax.broadcasted_iota(jnp.int32, (CHUNK, wide), 0)
    lane = lax.broadcasted_iota(jnp.int32, (CHUNK, wide), 1)
    s_i = lane & (CHUNK - 1)
    blk = [(lane >= j * CHUNK) & (lane < (j + 1) * CHUNK) for j in range(GDN_QUAD)]
    eye = (t_i == s_i).astype(F32)
    tri = {False: (s_i <= t_i, s_i < t_i), True: (s_i >= t_i, s_i > t_i)}
    rows_of = lambda j: slice(j * CHUNK, (j + 1) * CHUNK)

    def diag_blocks(m, off):
        out = jnp.where(blk[0], m[off:off + CHUNK], 0.0)
        for j in range(1, GDN_QUAD):
            out = jnp.where(blk[j], m[off + j * CHUNK:off + (j + 1) * CHUNK], out)
        return out

    def block_diag(y):
        return jnp.concatenate([jnp.where(blk[j], y, 0.0).astype(BF16) for j in range(GDN_QUAD)], axis=0)

    dmats, qks, xs, ys = [], [], [], []
    for _, load_q, load_k, rows, bcol, rev in quads:
        incl, strict = tri[rev]
        bc = jnp.concatenate([bcol, bcol], axis=1)
        dmat = jnp.exp(jnp.where(incl, diag_blocks(bc, 0) - rows[0:1, :], NEG_BIG))
        kb = load_k()
        if readout:
            qkk = _dot_nt(jnp.concatenate([load_q(), kb], axis=0), kb)
            qks.append(diag_blocks(qkk, 0))
            kk = diag_blocks(qkk, wide)
        else:
            kk = diag_blocks(_dot_nt(kb, kb), 0)
        n_mat = jnp.where(strict, kk * dmat, 0.0) * rows[1:2, :]
        dmats.append(dmat)
        xs.append(eye - n_mat)
        ys.append(n_mat)
    ys = [_dot(y.astype(BF16), block_diag(y)) for y in ys]
    for _ in range(4):
        xy = [_dot(jnp.concatenate([x, y], axis=0).astype(BF16), block_diag(y)) for x, y in zip(xs, ys)]
        xs = [x + p[:CHUNK] for x, p in zip(xs, xy)]
        ys = [p[CHUNK:] for p in xy]
    xs = [x + _dot(x.astype(BF16), block_diag(y)) for x, y in zip(xs, ys)]

    for i, (slot0, load_q, load_k, rows, bcol, rev) in enumerate(quads):
        k_all = load_k().astype(F32)
        e_col = jnp.exp(bcol)
        qkg_all_k = (k_all * e_col).astype(BF16)
        if readout:
            qkg_all_q = (load_q().astype(F32) * e_col).astype(BF16)
            pm_all = qks[i] * dmats[i] * rows[1:2, :]
        k_t = k_all.T
        kdt_all = (k_t * (jnp.exp(rows[2:3, :] - rows[0:1, :]) * rows[1:2, :])).astype(BF16)
        for j in range(GDN_QUAD):
            slot = slot0 + j
            shift = (wide - j * CHUNK) % wide
            take = lambda m: (m if shift == 0 else pltpu.roll(m, shift, 1))[:, :CHUNK]
            tb_scr[slot] = take(xs[i]).astype(BF16)
            qkg_scr[slot, CHUNK:, :] = qkg_all_k[rows_of(j)]
            if readout:
                qkg_scr[slot, :CHUNK, :] = qkg_all_q[rows_of(j)]
                kp_scr[slot, HEAD_DIM:, :] = take(pm_all).astype(BF16)
            kp_scr[slot, :HEAD_DIM, :] = kdt_all[:, j * CHUNK:(j + 1) * CHUNK]
            tot = rows[2:3, j * CHUNK:(j + 1) * CHUNK]
            dec_scr[slot] = jnp.exp(jnp.concatenate([tot, tot], axis=1))


def _gdn_recur(steps, dirs, readout, qkg_scr, tb_scr, kp_scr, dec_scr):
    def step(c, carry):
        slots = [slot_of(c) for _, slot_of, _, _ in dirs]
        states = [s_ref[...] for s_ref, _, _, _ in dirs]
        if readout:
            qks = [_dot(qkg_scr[slot], s.astype(BF16)) for slot, s in zip(slots, states)]
            kss = [p[CHUNK:] for p in qks]
        else:
            kss = [_dot(qkg_scr[slot, CHUNK:, :], s.astype(BF16)) for slot, s in zip(slots, states)]
        ws = [_dot(tb_scr[slot], (load_v(c) - ks).astype(BF16)).astype(BF16)
              for slot, ks, (_, _, load_v, _) in zip(slots, kss, dirs)]
        if readout:
            res = [_dot(kp_scr[slot], w) for slot, w in zip(slots, ws)]
            for slot, s, r, p, (s_ref, _, _, store_o) in zip(slots, states, res, qks, dirs):
                s_ref[...] = s * dec_scr[slot] + r[:HEAD_DIM]
                store_o(c, p[:CHUNK] + r[HEAD_DIM:])
        else:
            for slot, s, w, (s_ref, _, _, _) in zip(slots, states, ws, dirs):
                s_ref[...] = s * dec_scr[slot] + _dot(kp_scr[slot, :HEAD_DIM, :], w)
        return carry

    if isinstance(steps, int):
        lax.fori_loop(0, steps, step, 0, unroll=4)
    else:
        for c in steps:
            step(c, 0)


def _gdn_body(qf_ref, kf_ref, vf_ref, gf_ref, qb_ref, kb_ref, vb_ref, gb_ref,
              ck_ref, cv_ref, cg_ref, of_ref, ob_ref,
              s_scr, qkg_scr, tb_scr, kp_scr, dec_scr,
              *, n_chunks, n_ctx_chunks, heads, n_blocks):
    t = pl.program_id(2)
    h0 = pl.program_id(1) * heads
    scr = (qkg_scr, tb_scr, kp_scr, dec_scr)
    set_slots = 2 * heads * n_chunks

    def rows(c):
        return slice(c * CHUNK, (c + 1) * CHUNK) if isinstance(c, int) else pl.ds(pl.multiple_of(c * CHUNK, CHUNK), CHUNK)

    def chunk_list(hh, n, base, q_refs, k_refs, g_refs):
        head = h0 + hh
        wide = GDN_QUAD * CHUNK
        out = []
        for d in (0, 1):
            gate_rows, bcol = _gdn_gates(g_refs[d], d, head)
            for c0 in range(0, n, GDN_QUAD):
                sl = slice(c0 * CHUNK, c0 * CHUNK + wide)
                load_q = None if q_refs is None else (lambda r=q_refs[d], sl=sl: r[0, hh, sl, :])
                load_k = lambda r=k_refs[d], sl=sl: r[0, hh, sl, :]
                out.append((base + (2 * hh + d) * n + c0, load_q, load_k, gate_rows[:, sl], bcol[sl, :], d == 1))
        return out

    def prepare(hp, n, base, q_refs, k_refs, g_refs, readout):
        quads = []
        for i in range(GDN_PREP_HEADS):
            quads += chunk_list(hp * GDN_PREP_HEADS + i, n, base, q_refs, k_refs, g_refs)
        _gdn_prepare(quads, readout, *scr)

    def chains(n, base, vf, vb, of, ob):
        out = []
        for hh in range(heads):
            out.append((s_scr.at[2 * hh], lambda c, hh=hh: base + 2 * hh * n + c,
                        lambda c, hh=hh: vf[0, hh, rows(c), :].astype(F32),
                        None if of is None else (lambda c, o, hh=hh: of.__setitem__(
                            (0, hh, rows(c), slice(None)), o.astype(of.dtype)))))
            out.append((s_scr.at[2 * hh + 1], lambda c, hh=hh: base + (2 * hh + 1) * n + (n - 1 - c),
                        lambda c, hh=hh: vb[0, hh, rows(n - 1 - c), :].astype(F32),
                        None if ob is None else (lambda c, o, hh=hh: ob.__setitem__(
                            (0, hh, rows(n - 1 - c), slice(None)), o.astype(ob.dtype)))))
        return out

    main_refs = ((qf_ref, qb_ref), (kf_ref, kb_ref), (gf_ref, gb_ref))
    groups = heads // GDN_PREP_HEADS
    n = n_chunks

    @pl.when(t == 0)
    def _():
        s_scr[...] = jnp.zeros_like(s_scr)
        nc = n_ctx_chunks

        def ctx_head(hp, carry):
            prepare(hp, nc, 0, None, (ck_ref, ck_ref), (cg_ref, cg_ref), False)
            return carry

        lax.fori_loop(0, groups, ctx_head, 0)
        _gdn_recur(nc, chains(nc, 0, cv_ref, cv_ref, None, None), False, *scr)

        def first_head(hp, carry):
            prepare(hp, n, 0, *main_refs, True)
            return carry

        lax.fori_loop(0, groups, first_head, 0)

    @pl.when((t > 0) & (t < n_blocks))
    def _():
        base_p = (t & 1) * set_slots
        base_r = set_slots - base_p
        per = n // groups
        for g in range(groups):
            _gdn_recur(range(g * per, (g + 1) * per), chains(n, base_r, vf_ref, vb_ref, of_ref, ob_ref), True, *scr)
            prepare(g, n, base_p, *main_refs, True)

    @pl.when(t == n_blocks)
    def _():
        base_r = ((n_blocks - 1) & 1) * set_slots
        _gdn_recur(n, chains(n, base_r, vf_ref, vb_ref, of_ref, ob_ref), True, *scr)


GDN_HEADS = 8


def _gdn_scan(qkv, gates, qkv_ctx, gates_ctx, tb):
    bsz, _, length, _ = qkv.shape
    lc = qkv_ctx.shape[2] // bsz
    nt = length // tb
    nc, ncc = tb // CHUNK, lc // CHUNK
    hb = GDN_HEADS
    assert nc % GDN_QUAD == 0 and ncc % GDN_QUAD == 0 and hb % GDN_PREP_HEADS == 0
    assert nc % (hb // GDN_PREP_HEADS) == 0 and ncc <= 2 * nc
    slots = 2 * 2 * hb * nc
    blk = (1, hb, tb, HEAD_DIM)
    cblk = (1, hb, lc, HEAD_DIM)
    gblk = (1, 3, 2 * N_HEADS, tb)
    prep = lambda t: jnp.minimum(t, nt - 1)
    scan = lambda t: jnp.maximum(t - 1, 0)
    fwd = lambda off, tt: pl.BlockSpec(blk, lambda b, h, t: (b, off // hb + h, tt(t), 0))
    bwd = lambda off, tt: pl.BlockSpec(blk, lambda b, h, t: (b, off // hb + h, nt - 1 - tt(t), 0))
    ctx = lambda off: pl.BlockSpec(cblk, lambda b, h, t: (0, off // hb + h, b, 0))
    out_shape = jax.ShapeDtypeStruct((bsz, N_HEADS, length, HEAD_DIM), BF16)
    return pl.pallas_call(
        functools.partial(_gdn_body, n_chunks=nc, n_ctx_chunks=ncc, heads=hb, n_blocks=nt),
        grid=(bsz, N_HEADS // hb, nt + 1),
        in_specs=[fwd(0, prep), fwd(N_HEADS, prep), fwd(2 * N_HEADS, scan),
                  pl.BlockSpec(gblk, lambda b, h, t: (b, 0, 0, prep(t))),
                  bwd(0, prep), bwd(N_HEADS, prep), bwd(2 * N_HEADS, scan),
                  pl.BlockSpec(gblk, lambda b, h, t: (b, 0, 0, nt - 1 - prep(t))),
                  ctx(N_HEADS), ctx(2 * N_HEADS),
                  pl.BlockSpec((1, 3, 2 * N_HEADS, lc), lambda b, h, t: (0, 0, 0, b))],
        out_specs=[pl.BlockSpec(blk, lambda b, h, t: (b, h, scan(t), 0)),
                   pl.BlockSpec(blk, lambda b, h, t: (b, h, nt - 1 - scan(t), 0))],
        out_shape=[out_shape, out_shape],
        scratch_shapes=[pltpu.VMEM((2 * hb, HEAD_DIM, HEAD_DIM), F32),
                        pltpu.VMEM((slots, 2 * CHUNK, HEAD_DIM), BF16),
                        pltpu.VMEM((slots, CHUNK, CHUNK), BF16),
                        pltpu.VMEM((slots, HEAD_DIM + CHUNK, CHUNK), BF16),
                        pltpu.VMEM((slots, 1, HEAD_DIM), F32)],
        compiler_params=pltpu.CompilerParams(
            dimension_semantics=("parallel", "parallel", "arbitrary"),
            vmem_limit_bytes=VMEM_LIMIT),
    )(qkv, qkv, qkv, gates, qkv, qkv, qkv, gates, qkv_ctx, qkv_ctx, gates_ctx)


def _out_body(oaf_ref, oab_ref, obf_ref, obb_ref, za_ref, zb_ref, naw_ref, nbw_ref,
              w_ref, x_ref, gate_ref, fw_ref, o_ref):
    parts = []
    for of_ref, ob_ref, z_ref, nw_ref in ((oaf_ref, oab_ref, za_ref, naw_ref),
                                          (obf_ref, obb_ref, zb_ref, nbw_ref)):
        for h in range(N_HEADS):
            o = of_ref[0, h].astype(F32) + ob_ref[0, h].astype(F32)
            o = o * lax.rsqrt(jnp.mean(o * o, axis=-1, keepdims=True) + NORM_EPS) * nw_ref[h:h + 1, :]
            hz = 0.5 * z_ref[0, h].astype(F32)
            parts.append(((hz + hz * jnp.tanh(hz)) * o).astype(BF16))
    y = jnp.concatenate(parts, axis=1)
    xo = x_ref[0] + gate_ref[0] * _dot(y, w_ref[...])
    ms = jnp.mean(xo * xo, axis=-1, keepdims=True)
    o_ref[0] = xo * lax.rsqrt(ms + NORM_EPS) * fw_ref[...]


def _out_stage(oa_f, oa_b, ob_f, ob_b, yh, na_w, nb_w, w_out, x, gate, final_w, tm):
    bsz, length, d = x.shape
    hblk = (1, N_HEADS, tm, HEAD_DIM)
    ospec = pl.BlockSpec(hblk, lambda b, m: (b, 0, m, 0))
    full2 = lambda a: pl.BlockSpec(a.shape, lambda b, m: (0, 0))
    return pl.pallas_call(
        _out_body,
        grid=(bsz, length // tm),
        in_specs=[ospec, ospec, ospec, ospec,
                  pl.BlockSpec(hblk, lambda b, m: (b, OFF_AZ // N_HEADS, m, 0)),
                  pl.BlockSpec(hblk, lambda b, m: (b, OFF_BZ // N_HEADS, m, 0)),
                  full2(na_w), full2(nb_w), full2(w_out),
                  pl.BlockSpec((1, tm, d), lambda b, m: (b, m, 0)),
                  pl.BlockSpec((1, 1, d), lambda b, m: (b, 0, 0)),
                  full2(final_w)],
        out_specs=pl.BlockSpec((1, tm, d), lambda b, m: (b, m, 0)),
        out_shape=jax.ShapeDtypeStruct((bsz, length, d), F32),
        compiler_params=pltpu.CompilerParams(dimension_semantics=("parallel", "parallel"),
                                             vmem_limit_bytes=VMEM_LIMIT),
    )(oa_f, oa_b, ob_f, ob_b, yh, yh, na_w, nb_w, w_out, x, gate, final_w)


def kernel(x, c, ctx, c_ctx, norm_w, ada_w, ada_b, w_in, conv_w, hg_lb_logits, gdn_a_log,
           gdn_dt_bias, ha_norm_w, hb_norm_w, w_out, final_norm_w):
    bsz, length, d = x.shape
    lc = ctx.shape[1]
    assert d == D_MODEL and length % 512 == 0 and length % GRID_W == 0 and lc % CHUNK == 0
    assert w_in.shape[0] == 1, "single-layer block"

    n_cond = bsz + 1
    cond = jnp.concatenate([c, c_ctx[None, :], jnp.zeros((-n_cond % 8, d), F32)], axis=0)
    mod = _adaln(cond, ada_w[0], ada_b[0])
    shift, scale, gate = mod[:, :d], mod[:, d:2 * d], mod[:, 2 * d:]
    lat = lambda m: m[:bsz, None, :]
    of_ctx = lambda m: m[bsz:bsz + 1, None, :]

    w_main = w_in[0].astype(BF16)
    w_gate_t = w_main[:, N_MAIN:].T
    nw = norm_w[0].reshape(1, d)
    yh, gates = _inproj(x, nw, lat(scale), lat(shift), w_main, w_gate_t, tm=min(IN_TM, length))
    yh_c, gates_c = _inproj(ctx.reshape(1, bsz * lc, d), nw, of_ctx(scale), of_ctx(shift), w_main, w_gate_t,
                            tm=bsz * lc)

    lb = jax.nn.softmax(hg_lb_logits.astype(F32), axis=0)[0]
    lb = lb.reshape(2, N_HEADS, HEAD_DIM).transpose(1, 0, 2)
    oa_f, oa_b = _hgrn2_scan(yh, yh_c, lb, min(HG_TB, length))

    cw = conv_w[0].reshape(9, 3 * N_HEADS, HEAD_DIM).transpose(1, 0, 2)
    qkv = _gdn_conv(yh, cw, two_d=True)
    qkv_c = _gdn_conv(yh_c, cw, two_d=False, width=lc)
    params = jnp.stack([gdn_a_log[0].reshape(-1), gdn_dt_bias[0].reshape(-1)], axis=1).astype(F32)
    ob_f, ob_b = _gdn_scan(qkv, _gate_rows(gates, params), qkv_c, _gate_rows(gates_c, params),
                           min(GDN_TB, length))

    return _out_stage(oa_f, oa_b, ob_f, ob_b, yh, ha_norm_w[0], hb_norm_w[0],
                      w_out[0].astype(BF16), x, lat(gate), final_norm_w.reshape(1, d),
                      tm=min(OUT_TM, length))
```

```python
import functools

import jax
import jax.numpy as jnp
import numpy as np
from jax import lax
from jax.experimental import pallas as pl
from jax.experimental.pallas import tpu as pltpu

F32 = jnp.float32
BF16 = jnp.bfloat16

D_MODEL = 1024
N_HEADS = 8
HEAD_DIM = 128
CHUNK = 64
GRID_W = 64
NORM_EPS = 1e-6
N_MAIN = 72 * HEAD_DIM
N_GATE = 4 * N_HEADS
OFF_AQ, OFF_AFF, OFF_AFB, OFF_AI, OFF_AZ, OFF_BQ, OFF_BZ = 0, 8, 16, 24, 32, 40, 64
NEG_BIG = -1e30
F32_TINY = float(np.finfo(np.float32).tiny)
VMEM_LIMIT = 56 * 1024 * 1024
IN_TM, IN_TN = 2048, 1024
HG_TB, GDN_TB = 4096, 512
CONV_RT = 512
OUT_TM = 512
GATE_TG = 2048


def _dot(a, b):
    return jnp.dot(a, b, preferred_element_type=F32)


def _dot_nt(a, b):
    return lax.dot_general(a, b, (((1,), (1,)), ((), ())), preferred_element_type=F32)


def _sigmoid(x):
    return 1.0 / (1.0 + jnp.exp(-x))


def _silu(x):
    return x * _sigmoid(x)


def _adaln_body(c_ref, w_ref, b_ref, o_ref):
    o_ref[...] = _dot(_silu(c_ref[...]), w_ref[...]) + b_ref[...]


def _adaln(cond, ada_w, ada_b):
    rows, d = cond.shape
    n = ada_w.shape[1]
    tn = 1024
    return pl.pallas_call(
        _adaln_body,
        grid=(n // tn,),
        in_specs=[pl.BlockSpec((rows, d), lambda j: (0, 0)),
                  pl.BlockSpec((d, tn), lambda j: (0, j)),
                  pl.BlockSpec((1, tn), lambda j: (0, j))],
        out_specs=pl.BlockSpec((rows, tn), lambda j: (0, j)),
        out_shape=jax.ShapeDtypeStruct((rows, n), F32),
        compiler_params=pltpu.CompilerParams(dimension_semantics=("arbitrary",),
                                             vmem_limit_bytes=VMEM_LIMIT),
    )(cond, ada_w, ada_b.reshape(1, n))


def _inproj_body(x_ref, nw_ref, sc_ref, sh_ref, w_ref, wg_ref, y_ref, yg_ref, h_scr, *, tn):
    @pl.when(pl.program_id(2) == 0)
    def _():
        x = x_ref[0]
        ms = jnp.mean(x * x, axis=-1, keepdims=True)
        h = x * lax.rsqrt(ms + NORM_EPS) * nw_ref[...]
        h = (h * (1.0 + sc_ref[0]) + sh_ref[0]).astype(BF16)
        h_scr[...] = h
        yg_ref[0] = _dot_nt(wg_ref[...], h)

    acc = _dot(h_scr[...], w_ref[...])
    for j in range(tn // HEAD_DIM):
        y_ref[0, j] = acc[:, j * HEAD_DIM:(j + 1) * HEAD_DIM].astype(y_ref.dtype)


def _inproj(x, norm_w, scale, shift, w_main, w_gate_t, tm):
    bsz, length, d = x.shape
    tn = IN_TN
    grid = (bsz, length // tm, N_MAIN // tn)
    return pl.pallas_call(
        functools.partial(_inproj_body, tn=tn),
        grid=grid,
        in_specs=[pl.BlockSpec((1, tm, d), lambda b, m, n: (b, m, 0)),
                  pl.BlockSpec((1, d), lambda b, m, n: (0, 0)),
                  pl.BlockSpec((1, 1, d), lambda b, m, n: (b, 0, 0)),
                  pl.BlockSpec((1, 1, d), lambda b, m, n: (b, 0, 0)),
                  pl.BlockSpec((d, tn), lambda b, m, n: (0, n)),
                  pl.BlockSpec((N_GATE, d), lambda b, m, n: (0, 0))],
        out_specs=[pl.BlockSpec((1, tn // HEAD_DIM, tm, HEAD_DIM), lambda b, m, n: (b, n, m, 0)),
                   pl.BlockSpec((1, N_GATE, tm), lambda b, m, n: (b, 0, m))],
        out_shape=[jax.ShapeDtypeStruct((bsz, N_MAIN // HEAD_DIM, length, HEAD_DIM), BF16),
                   jax.ShapeDtypeStruct((bsz, N_GATE, length), F32)],
        scratch_shapes=[pltpu.VMEM((tm, d), BF16)],
        compiler_params=pltpu.CompilerParams(
            dimension_semantics=("parallel", "parallel", "arbitrary"),
            vmem_limit_bytes=VMEM_LIMIT),
    )(x, norm_w, scale, shift, w_main, w_gate_t)


CONV_PAD = 72
CONV_HALO = 8


def _conv_body(x_ref, w_ref, o_ref, pad_scr, *, length, width, two_d, rt):
    blk = pl.program_id(1)
    zeros = jnp.zeros((CONV_PAD, HEAD_DIM), F32)
    pad_scr[0:CONV_PAD, :] = zeros
    pad_scr[CONV_PAD + length:CONV_PAD + length + CONV_PAD, :] = zeros
    pad_scr[CONV_PAD:CONV_PAD + length, :] = x_ref[0, 0].astype(F32)
    w = 0.5 * w_ref[0]
    win_rows = rt + 2 * CONV_HALO
    is_q = blk < N_HEADS
    is_qk = blk < 2 * N_HEADS
    col = lax.broadcasted_iota(jnp.int32, (rt, HEAD_DIM), 0) & (width - 1)

    def tile(i, carry):
        s = pl.multiple_of(i * rt, rt)
        sums = [None, None, None]
        for dr in ((-1, 0, 1) if two_d else (0,)):
            base = pl.multiple_of(s + (CONV_PAD + dr * width - CONV_HALO), 8)
            win = pad_scr[pl.ds(base, win_rows), :]
            for j in range(3):
                tap = (dr + 1) * 3 + j
                term = win * w[tap:tap + 1, :]
                sums[j] = term if sums[j] is None else sums[j] + term
        inner = slice(CONV_HALO, CONV_HALO + rt)
        left = pltpu.roll(sums[0], 1, 0)[inner]
        right = pltpu.roll(sums[2], win_rows - 1, 0)[inner]
        hs = sums[1][inner] + jnp.where(col >= 1, left, 0.0) + jnp.where(col <= width - 2, right, 0.0)
        a = hs + hs * jnp.tanh(hs)
        nrm = lax.rsqrt(jnp.sum(a * a, axis=-1, keepdims=True) + NORM_EPS)
        f = jnp.where(is_q, nrm * HEAD_DIM ** -0.5, jnp.where(is_qk, nrm, 1.0))
        o_ref[0, 0, pl.ds(s, rt), :] = (a * f).astype(o_ref.dtype)
        return carry

    lax.fori_loop(0, length // rt, tile, 0)


def _gdn_conv(yh, conv_w, two_d, width=GRID_W):
    bsz, _, length, _ = yh.shape
    assert width & (width - 1) == 0, "grid width must be a power of two (column index by bit mask)"
    rt = min(CONV_RT, length)
    assert rt % width == 0
    nblk = 3 * N_HEADS
    return pl.pallas_call(
        functools.partial(_conv_body, length=length, width=width, two_d=two_d, rt=rt),
        grid=(bsz, nblk),
        in_specs=[pl.BlockSpec((1, 1, length, HEAD_DIM), lambda b, j: (b, OFF_BQ + j, 0, 0)),
                  pl.BlockSpec((1, 9, HEAD_DIM), lambda b, j: (j, 0, 0))],
        out_specs=pl.BlockSpec((1, 1, length, HEAD_DIM), lambda b, j: (b, j, 0, 0)),
        out_shape=jax.ShapeDtypeStruct((bsz, nblk, length, HEAD_DIM), BF16),
        scratch_shapes=[pltpu.VMEM((length + 2 * CONV_PAD, HEAD_DIM), F32)],
        compiler_params=pltpu.CompilerParams(dimension_semantics=("parallel", "parallel"),
                                             vmem_limit_bytes=VMEM_LIMIT),
    )(yh, conv_w)


HG_LEVELS = (32, 16, 8, 4, 2, 1)
HG_PAIRS = ((0, 1), (2, 3), (4, 5))
HG_GROUP = 4


def _hg_constants():
    t = np.arange(CHUNK)[:, None]
    col = np.arange(HEAD_DIM)[None, :]
    s = col % CHUNK
    out = [np.broadcast_to(np.where((t & m) != 0, 1.0, -1.0), (CHUNK, HEAD_DIM)) for m in HG_LEVELS]
    for rev in (False, True):
        for pa, pb in HG_PAIRS:
            keep = np.zeros((CHUNK, HEAD_DIM), bool)
            for half, lv in ((col < CHUNK, pa), (col >= CHUNK, pb)):
                m = HG_LEVELS[lv]
                split = ((t ^ s) >> (m.bit_length() - 1)) == 1
                t_hi = (t & m) != 0
                keep |= half & split & (~t_hi if rev else t_hi)
            out.append(keep)
    out.append((col == t) & (col < CHUNK))
    return jnp.asarray(np.stack([np.asarray(o, np.float32) for o in out]))


def _hg_triangles():
    t = np.arange(CHUNK)
    lower = (t[None, :] <= t[:, None]).astype(np.float32)
    return jnp.asarray(np.stack([np.tile(lower, (1, 3)), np.tile(lower.T, (1, 3))]), BF16)


def _cumsum_rows(g, tri3):
    hi = g.astype(BF16)
    r1 = g - hi.astype(F32)
    mid = r1.astype(BF16)
    lo = (r1 - mid.astype(F32)).astype(BF16)
    return _dot(tri3, jnp.concatenate([hi, mid, lo], axis=0))


def _level_operand(b, b_ref, q, k, m, rev):
    parts = []
    for blk in range(CHUNK // m):
        rows = slice(blk * m, (blk + 1) * m)
        r = (blk // 2) * 2 * m + m
        ref = jnp.broadcast_to(b_ref[r:r + 1, :], (m, HEAD_DIM))
        q_side = (blk % 2 == 1) != rev
        e = (b[rows] - ref) if q_side else (ref - b[rows])
        parts.append(((q if q_side else k)[rows] * jnp.exp2(e)).astype(BF16))
    return jnp.concatenate(parts, axis=0)


def _level_ref(b_ref, m, sub):
    bc = lambda r, n: jnp.broadcast_to(b_ref[r:r + 1, :], (n, HEAD_DIM))
    if m >= 4:
        return jnp.concatenate([bc(blk * 2 * m + m, 2 * m) for blk in range(CHUNK // (2 * m))], axis=0)
    lo = jnp.concatenate([bc(v * 8 + 2, 8) for v in range(CHUNK // 8)], axis=0)
    hi = jnp.concatenate([bc(v * 8 + 6, 8) for v in range(CHUNK // 8)], axis=0)
    return jnp.where(sub < 4, lo, hi)


def _hg_prepare(items, readout, cst_ref, tri_ref, b_scr, sc_scr, vv_scr, u_scr, dec_scr):
    row = lax.broadcasted_iota(jnp.int32, (CHUNK, HEAD_DIM), 0)
    sub = row & 7
    even = (row & 1) == 0
    gs, ks, bs = [], [], []
    for i, (_, _, load_f, _, lb, rev) in enumerate(items):
        gate = (1.0 - lb) * (0.5 + 0.5 * jnp.tanh(0.5 * load_f()))
        g = jnp.log2(jnp.maximum(lb + gate, F32_TINY))
        gs.append(g)
        ks.append((1.0 - lb) - gate)
        b = _cumsum_rows(g, tri_ref[1 if rev else 0])
        bs.append(b)
        b_scr[i] = b
    for i, (slot, _, _, load_v, _, rev) in enumerate(items):
        b = bs[i]
        btot = b[0:1, :] if rev else b[CHUNK - 1:CHUNK, :]
        v = load_v()
        vv_t = jnp.concatenate([v, v], axis=0).T.astype(BF16)
        k_dec = (ks[i] * jnp.exp2(btot - b)).astype(BF16)
        u_scr[slot] = _dot(vv_t[:, :CHUNK], k_dec)
        dec_scr[slot] = jnp.exp2(btot)
        vv_scr[slot] = vv_t
    if not readout:
        return

    qs = []
    for _, load_q, _, _, _, _ in items:
        q_raw = load_q()
        hs = (0.5 * HEAD_DIM ** -0.5) * q_raw
        qs.append(hs + hs * jnp.tanh(0.5 * q_raw))
    scores = [jnp.zeros((CHUNK, HEAD_DIM), F32) for _ in items]
    zero_blk = jnp.zeros((CHUNK, HEAD_DIM), BF16)
    for pi, pair in enumerate(HG_PAIRS):
        gps = []
        for i, (_, _, _, _, _, rev) in enumerate(items):
            a_mats = []
            for lv in pair:
                m = HG_LEVELS[lv]
                if m >= 8:
                    a_mats.append(_level_operand(bs[i], b_scr.at[i], qs[i], ks[i], m, rev))
                    continue
                sign = cst_ref[lv]
                if m == 1:
                    x = jnp.exp2(jnp.where(even, gs[i] if rev else pltpu.roll(gs[i], CHUNK - 1, 0), 0.0))
                else:
                    d = bs[i] - _level_ref(b_scr.at[i], m, sub)
                    x = jnp.exp2((-d if rev else d) * sign)
                t_is_q = (sign < 0.0) if rev else (sign > 0.0)
                a_mats.append((jnp.where(t_is_q, qs[i], ks[i]) * x).astype(BF16))
            lhs = jnp.concatenate(a_mats, axis=1)
            rhs = jnp.concatenate([jnp.concatenate([a_mats[0], zero_blk], axis=1),
                                   jnp.concatenate([zero_blk, a_mats[1]], axis=1)], axis=0)
            gps.append(_dot_nt(lhs, rhs))
        for i, (_, _, _, _, _, rev) in enumerate(items):
            scores[i] = scores[i] + gps[i] * cst_ref[6 + (3 if rev else 0) + pi]
    for i, (slot, _, _, _, _, _) in enumerate(items):
        diag = jnp.sum(qs[i] * ks[i], axis=-1, keepdims=True) * cst_ref[12]
        sc_scr[slot, :, :HEAD_DIM] = (scores[i] + diag).astype(BF16)
        sc_scr[slot, :, HEAD_DIM:] = (qs[i] * jnp.exp2(bs[i])).astype(BF16)


def _hg_recur(steps, dirs, readout, sc_scr, vv_scr, u_scr, dec_scr):
    states = [s_scr[...] for s_scr, _, _ in dirs]
    for c in steps:
        for i, (_, slot_of, store_o) in enumerate(dirs):
            slot = slot_of(c)
            if readout:
                rhs = jnp.concatenate([vv_scr[slot], states[i].astype(BF16)], axis=1)
                store_o(c, _dot_nt(sc_scr[slot], rhs))
            states[i] = states[i] * dec_scr[slot] + u_scr[slot]
    for (s_scr, _, _), st in zip(dirs, states):
        s_scr[...] = st


def _hg_body(cst_ref, tri_ref, qf_ref, ff_ref, vf_ref, qb_ref, fb_ref, vb_ref,
             cff_ref, cfb_ref, cv_ref, lb_ref, of_ref, ob_ref,
             sf_scr, sb_scr, b_scr, sc_scr, vv_scr, u_scr, dec_scr, *, n_chunks, n_ctx_chunks):
    lb_f = lb_ref[0, 0:1, :]
    lb_b = lb_ref[0, 1:2, :]
    scr = (sc_scr, vv_scr, u_scr, dec_scr)
    load = lambda ref, rows: (lambda: ref[0, 0, rows, :].astype(F32))

    @pl.when(pl.program_id(2) == 0)
    def _():
        sf_scr[...] = jnp.zeros_like(sf_scr)
        sb_scr[...] = jnp.zeros_like(sb_scr)
        n = n_ctx_chunks
        for c0 in range(0, n, HG_GROUP):
            items = []
            for c in range(c0, min(c0 + HG_GROUP, n)):
                sl = slice(c * CHUNK, (c + 1) * CHUNK)
                items.append((c, None, load(cff_ref, sl), load(cv_ref, sl), lb_f, False))
                items.append((n + c, None, load(cfb_ref, sl), load(cv_ref, sl), lb_b, True))
            _hg_prepare(items, False, cst_ref, tri_ref, b_scr, *scr)
        _hg_recur(range(n), [(sf_scr, lambda c: c, None), (sb_scr, lambda c: 2 * n - 1 - c, None)],
                  False, *scr)

    n = n_chunks
    rows = lambda c: slice(c * CHUNK, (c + 1) * CHUNK)

    def prepare(g):
        items = []
        for p in range(g * HG_GROUP, (g + 1) * HG_GROUP):
            rf, rb = rows(p), rows(n - 1 - p)
            items.append((p, load(qf_ref, rf), load(ff_ref, rf), load(vf_ref, rf), lb_f, False))
            items.append((2 * n - 1 - p, load(qb_ref, rb), load(fb_ref, rb), load(vb_ref, rb), lb_b, True))
        _hg_prepare(items, True, cst_ref, tri_ref, b_scr, *scr)

    def store_f(c, o):
        of_ref[0, 0, rows(c), :] = o.astype(of_ref.dtype)

    def store_b(c, o):
        ob_ref[0, 0, rows(n - 1 - c), :] = o.astype(ob_ref.dtype)

    def recur(g):
        _hg_recur(range(g * HG_GROUP, (g + 1) * HG_GROUP),
                  [(sf_scr, lambda c: c, store_f), (sb_scr, lambda c: 2 * n - 1 - c, store_b)], True, *scr)

    groups = n // HG_GROUP
    prepare(0)
    for g in range(1, groups):
        recur(g - 1)
        prepare(g)
    recur(groups - 1)


def _hgrn2_scan(yh, yh_ctx, lb, tb):
    bsz, _, length, _ = yh.shape
    lc = yh_ctx.shape[2] // bsz
    nt = length // tb
    nc, ncc = tb // CHUNK, lc // CHUNK
    assert nc % HG_GROUP == 0
    slots = 2 * max(nc, ncc)
    blk = (1, 1, tb, HEAD_DIM)
    cblk = (1, 1, lc, HEAD_DIM)
    fwd = lambda off: pl.BlockSpec(blk, lambda b, h, t: (b, off + h, t, 0))
    bwd = lambda off: pl.BlockSpec(blk, lambda b, h, t: (b, off + h, nt - 1 - t, 0))
    ctx = lambda off: pl.BlockSpec(cblk, lambda b, h, t: (0, off + h, b, 0))
    consts = _hg_constants()
    tri3 = _hg_triangles()
    out_shape = jax.ShapeDtypeStruct((bsz, N_HEADS, length, HEAD_DIM), BF16)
    return pl.pallas_call(
        functools.partial(_hg_body, n_chunks=nc, n_ctx_chunks=ncc),
        grid=(bsz, N_HEADS, nt),
        in_specs=[pl.BlockSpec(consts.shape, lambda b, h, t: (0, 0, 0)),
                  pl.BlockSpec(tri3.shape, lambda b, h, t: (0, 0, 0)),
                  fwd(OFF_AQ), fwd(OFF_AFF), fwd(OFF_AI), bwd(OFF_AQ), bwd(OFF_AFB), bwd(OFF_AI),
                  ctx(OFF_AFF), ctx(OFF_AFB), ctx(OFF_AI),
                  pl.BlockSpec((1, 2, HEAD_DIM), lambda b, h, t: (h, 0, 0))],
        out_specs=[pl.BlockSpec(blk, lambda b, h, t: (b, h, t, 0)),
                   pl.BlockSpec(blk, lambda b, h, t: (b, h, nt - 1 - t, 0))],
        out_shape=[out_shape, out_shape],
        scratch_shapes=[pltpu.VMEM((HEAD_DIM, HEAD_DIM), F32), pltpu.VMEM((HEAD_DIM, HEAD_DIM), F32),
                        pltpu.VMEM((2 * HG_GROUP, CHUNK, HEAD_DIM), F32),
                        pltpu.VMEM((slots, CHUNK, 2 * HEAD_DIM), BF16),
                        pltpu.VMEM((slots, HEAD_DIM, HEAD_DIM), BF16),
                        pltpu.VMEM((slots, HEAD_DIM, HEAD_DIM), F32),
                        pltpu.VMEM((slots, 1, HEAD_DIM), F32)],
        compiler_params=pltpu.CompilerParams(
            dimension_semantics=("parallel", "parallel", "arbitrary"),
            vmem_limit_bytes=VMEM_LIMIT),
    )(consts, tri3, yh, yh, yh, yh, yh, yh, yh_ctx, yh_ctx, yh_ctx, lb)


def _seg_cumsum_lanes(x, rev, lane):
    total = x.shape[1]
    seg = lane & (CHUNK - 1)
    for sh in (1, 2, 4, 8, 16, 32):
        if rev:
            x = x + jnp.where(seg < CHUNK - sh, pltpu.roll(x, total - sh, 1), 0.0)
        else:
            x = x + jnp.where(seg >= sh, pltpu.roll(x, sh, 1), 0.0)
    return x


def _gate_rows_body(g_ref, p_ref, o_ref):
    nrow = 2 * N_HEADS
    total = g_ref.shape[2]
    lane = lax.broadcasted_iota(jnp.int32, (nrow, total), 1)
    row = lax.broadcasted_iota(jnp.int32, (nrow, total), 0)
    z = g_ref[0, :nrow, :] + p_ref[:, 1:2]
    softplus = jnp.maximum(z, 0.0) + jnp.log1p(jnp.exp(-jnp.abs(z)))
    g = -jnp.exp(p_ref[:, 0:1]) * softplus
    prefix = _seg_cumsum_lanes(g, False, lane)
    suffix = _seg_cumsum_lanes(g, True, lane)
    o_ref[0, 0] = jnp.where(row < N_HEADS, prefix, suffix)
    o_ref[0, 1] = _sigmoid(g_ref[0, nrow:, :])
    o_ref[0, 2] = prefix + suffix - g


def _gate_rows(gates, params):
    bsz, _, length = gates.shape
    tg = min(GATE_TG, length)
    nrow = 2 * N_HEADS
    return pl.pallas_call(
        _gate_rows_body,
        grid=(bsz, length // tg),
        in_specs=[pl.BlockSpec((1, N_GATE, tg), lambda b, t: (b, 0, t)),
                  pl.BlockSpec(params.shape, lambda b, t: (0, 0))],
        out_specs=pl.BlockSpec((1, 3, nrow, tg), lambda b, t: (b, 0, 0, t)),
        out_shape=jax.ShapeDtypeStruct((bsz, 3, nrow, length), F32),
        compiler_params=pltpu.CompilerParams(dimension_semantics=("parallel", "parallel"),
                                             vmem_limit_bytes=VMEM_LIMIT),
    )(gates, params)


def _gdn_gates(g_ref, d, head):
    r = pl.ds(d * N_HEADS + head, 1)
    b = g_ref[0, 0, r, :]
    rows = jnp.concatenate([b, g_ref[0, 1, r, :], g_ref[0, 2, r, :]], axis=0)
    return rows, jnp.broadcast_to(b, (HEAD_DIM, b.shape[1])).T


GDN_QUAD = 4
GDN_PREP_HEADS = 4


def _gdn_prepare(quads, readout, qkg_scr, tb_scr, kp_scr, dec_scr):
    wide = GDN_QUAD * CHUNK
    t_i = lax.broadcasted_iota(jnp.int32, (CHUNK, wide), 0)
    lane = lax.broadcasted_iota(jnp.int32, (CHUNK, wide), 1)
    s_i = lane & (CHUNK - 1)
    blk = [(lane >= j * CHUNK) & (lane < (j + 1) * CHUNK) for j in range(GDN_QUAD)]
    eye = (t_i == s_i).astype(F32)
    tri = {False: (s_i <= t_i, s_i < t_i), True: (s_i >= t_i, s_i > t_i)}
    rows_of = lambda j: slice(j * CHUNK, (j + 1) * CHUNK)

    def diag_blocks(m, off):
        out = jnp.where(blk[0], m[off:off + CHUNK], 0.0)
        for j in range(1, GDN_QUAD):
            out = jnp.where(blk[j], m[off + j * CHUNK:off + (j + 1) * CHUNK], out)
        return out

    def block_diag(y):
        return jnp.concatenate([jnp.where(blk[j], y, 0.0).astype(BF16) for j in range(GDN_QUAD)], axis=0)

    dmats, qks, xs, ys = [], [], [], []
    for _, load_q, load_k, rows, bcol, rev in quads:
        incl, strict = tri[rev]
        bc = jnp.concatenate([bcol, bcol], axis=1)
        dmat = jnp.exp(jnp.where(incl, diag_blocks(bc, 0) - rows[0:1, :], NEG_BIG))
        kb = load_k()
        qb = load_q() if readout else None
        pair = 2 * CHUNK
        lo_half = lax.broadcasted_iota(jnp.int32, (CHUNK, pair), 1) < CHUNK
        pair_diag = lambda m: jnp.where(lo_half, m[:CHUNK], m[CHUNK:])
        kk_parts, qk_parts = [], []
        for p in range(GDN_QUAD // 2):
            kp = kb[p * pair:(p + 1) * pair]
            if readout:
                qkk = _dot_nt(jnp.concatenate([qb[p * pair:(p + 1) * pair], kp], axis=0), kp)
                qk_parts.append(pair_diag(qkk[:pair]))
                kk_parts.append(pair_diag(qkk[pair:]))
            else:
                kk_parts.append(pair_diag(_dot_nt(kp, kp)))
        kk = jnp.concatenate(kk_parts, axis=1)
        if readout:
            qks.append(jnp.concatenate(qk_parts, axis=1))
        n_mat = jnp.where(strict, kk * dmat, 0.0) * rows[1:2, :]
        dmats.append(dmat)
        xs.append(eye - n_mat)
        ys.append(n_mat)
    ys = [_dot(y.astype(BF16), block_diag(y)) for y in ys]
    for _ in range(4):
        xy = [_dot(jnp.concatenate([x, y], axis=0).astype(BF16), block_diag(y)) for x, y in zip(xs, ys)]
        xs = [x + p[:CHUNK] for x, p in zip(xs, xy)]
        ys = [p[CHUNK:] for p in xy]
    xs = [x + _dot(x.astype(BF16), block_diag(y)) for x, y in zip(xs, ys)]

    for i, (slot0, load_q, load_k, rows, bcol, rev) in enumerate(quads):
        k_all = load_k().astype(F32)
        e_col = jnp.exp(bcol)
        qkg_all_k = (k_all * e_col).astype(BF16)
        if readout:
            qkg_all_q = (load_q().astype(F32) * e_col).astype(BF16)
            pm_all = qks[i] * dmats[i] * rows[1:2, :]
        k_t = k_all.T
        kdt_all = (k_t * (jnp.exp(rows[2:3, :] - rows[0:1, :]) * rows[1:2, :])).astype(BF16)
        for j in range(GDN_QUAD):
            slot = slot0 + j
            shift = (wide - j * CHUNK) % wide
            take = lambda m: (m if shift == 0 else pltpu.roll(m, shift, 1))[:, :CHUNK]
            tb_scr[slot] = take(xs[i]).astype(BF16)
            qkg_scr[slot, CHUNK:, :] = qkg_all_k[rows_of(j)]
            if readout:
                qkg_scr[slot, :CHUNK, :] = qkg_all_q[rows_of(j)]
                kp_scr[slot, HEAD_DIM:, :] = take(pm_all).astype(BF16)
            kp_scr[slot, :HEAD_DIM, :] = kdt_all[:, j * CHUNK:(j + 1) * CHUNK]
            tot = rows[2:3, j * CHUNK:(j + 1) * CHUNK]
            dec_scr[slot] = jnp.exp(jnp.concatenate([tot, tot], axis=1))


def _gdn_recur(steps, dirs, readout, qkg_scr, tb_scr, kp_scr, dec_scr):
    def step(c, carry):
        slots = [slot_of(c) for _, slot_of, _, _ in dirs]
        states = [s_ref[...] for s_ref, _, _, _ in dirs]
        if readout:
            qks = [_dot(qkg_scr[slot], s.astype(BF16)) for slot, s in zip(slots, states)]
            kss = [p[CHUNK:] for p in qks]
        else:
            kss = [_dot(qkg_scr[slot, CHUNK:, :], s.astype(BF16)) for slot, s in zip(slots, states)]
        ws = [_dot(tb_scr[slot], (load_v(c) - ks).astype(BF16)).astype(BF16)
              for slot, ks, (_, _, load_v, _) in zip(slots, kss, dirs)]
        if readout:
            res = [_dot(kp_scr[slot], w) for slot, w in zip(slots, ws)]
            for slot, s, r, p, (s_ref, _, _, store_o) in zip(slots, states, res, qks, dirs):
                s_ref[...] = s * dec_scr[slot] + r[:HEAD_DIM]
                store_o(c, p[:CHUNK] + r[HEAD_DIM:])
        else:
            for slot, s, w, (s_ref, _, _, _) in zip(slots, states, ws, dirs):
                s_ref[...] = s * dec_scr[slot] + _dot(kp_scr[slot, :HEAD_DIM, :], w)
        return carry

    if isinstance(steps, int):
        lax.fori_loop(0, steps, step, 0, unroll=4)
    else:
        for c in steps:
            step(c, 0)


def _gdn_body(qf_ref, kf_ref, vf_ref, gf_ref, qb_ref, kb_ref, vb_ref, gb_ref,
              ck_ref, cv_ref, cg_ref, of_ref, ob_ref,
              s_scr, qkg_scr, tb_scr, kp_scr, dec_scr,
              *, n_chunks, n_ctx_chunks, heads, n_blocks):
    t = pl.program_id(2)
    h0 = pl.program_id(1) * heads
    scr = (qkg_scr, tb_scr, kp_scr, dec_scr)
    set_slots = 2 * heads * n_chunks

    def rows(c):
        return slice(c * CHUNK, (c + 1) * CHUNK) if isinstance(c, int) else pl.ds(pl.multiple_of(c * CHUNK, CHUNK), CHUNK)

    def chunk_list(hh, n, base, q_refs, k_refs, g_refs):
        head = h0 + hh
        wide = GDN_QUAD * CHUNK
        out = []
        for d in (0, 1):
            gate_rows, bcol = _gdn_gates(g_refs[d], d, head)
            for c0 in range(0, n, GDN_QUAD):
                sl = slice(c0 * CHUNK, c0 * CHUNK + wide)
                load_q = None if q_refs is None else (lambda r=q_refs[d], sl=sl: r[0, hh, sl, :])
                load_k = lambda r=k_refs[d], sl=sl: r[0, hh, sl, :]
                out.append((base + (2 * hh + d) * n + c0, load_q, load_k, gate_rows[:, sl], bcol[sl, :], d == 1))
        return out

    def prepare(hp, n, base, q_refs, k_refs, g_refs, readout):
        quads = []
        for i in range(GDN_PREP_HEADS):
            quads += chunk_list(hp * GDN_PREP_HEADS + i, n, base, q_refs, k_refs, g_refs)
        _gdn_prepare(quads, readout, *scr)

    def chains(n, base, vf, vb, of, ob):
        out = []
        for hh in range(heads):
            out.append((s_scr.at[2 * hh], lambda c, hh=hh: base + 2 * hh * n + c,
                        lambda c, hh=hh: vf[0, hh, rows(c), :].astype(F32),
                        None if of is None else (lambda c, o, hh=hh: of.__setitem__(
                            (0, hh, rows(c), slice(None)), o.astype(of.dtype)))))
            out.append((s_scr.at[2 * hh + 1], lambda c, hh=hh: base + (2 * hh + 1) * n + (n - 1 - c),
                        lambda c, hh=hh: vb[0, hh, rows(n - 1 - c), :].astype(F32),
                        None if ob is None else (lambda c, o, hh=hh: ob.__setitem__(
                            (0, hh, rows(n - 1 - c), slice(None)), o.astype(ob.dtype)))))
        return out

    main_refs = ((qf_ref, qb_ref), (kf_ref, kb_ref), (gf_ref, gb_ref))
    groups = heads // GDN_PREP_HEADS
    n = n_chunks

    @pl.when(t == 0)
    def _():
        s_scr[...] = jnp.zeros_like(s_scr)
        nc = n_ctx_chunks

        def ctx_head(hp, carry):
            prepare(hp, nc, 0, None, (ck_ref, ck_ref), (cg_ref, cg_ref), False)
            return carry

        lax.fori_loop(0, groups, ctx_head, 0)
        _gdn_recur(nc, chains(nc, 0, cv_ref, cv_ref, None, None), False, *scr)

        def first_head(hp, carry):
            prepare(hp, n, 0, *main_refs, True)
            return carry

        lax.fori_loop(0, groups, first_head, 0)

    @pl.when((t > 0) & (t < n_blocks))
    def _():
        base_p = (t & 1) * set_slots
        base_r = set_slots - base_p
        per = n // groups
        for g in range(groups):
            _gdn_recur(range(g * per, (g + 1) * per), chains(n, base_r, vf_ref, vb_ref, of_ref, ob_ref), True, *scr)
            prepare(g, n, base_p, *main_refs, True)

    @pl.when(t == n_blocks)
    def _():
        base_r = ((n_blocks - 1) & 1) * set_slots
        _gdn_recur(n, chains(n, base_r, vf_ref, vb_ref, of_ref, ob_ref), True, *scr)


GDN_HEADS = 8


def _gdn_scan(qkv, gates, qkv_ctx, gates_ctx, tb):
    bsz, _, length, _ = qkv.shape
    lc = qkv_ctx.shape[2] // bsz
    nt = length // tb
    nc, ncc = tb // CHUNK, lc // CHUNK
    hb = GDN_HEADS
    assert nc % GDN_QUAD == 0 and ncc % GDN_QUAD == 0 and hb % GDN_PREP_HEADS == 0
    assert nc % (hb // GDN_PREP_HEADS) == 0 and ncc <= 2 * nc
    slots = 2 * 2 * hb * nc
    blk = (1, hb, tb, HEAD_DIM)
    cblk = (1, hb, lc, HEAD_DIM)
    gblk = (1, 3, 2 * N_HEADS, tb)
    prep = lambda t: jnp.minimum(t, nt - 1)
    scan = lambda t: jnp.maximum(t - 1, 0)
    fwd = lambda off, tt: pl.BlockSpec(blk, lambda b, h, t: (b, off // hb + h, tt(t), 0))
    bwd = lambda off, tt: pl.BlockSpec(blk, lambda b, h, t: (b, off // hb + h, nt - 1 - tt(t), 0))
    ctx = lambda off: pl.BlockSpec(cblk, lambda b, h, t: (0, off // hb + h, b, 0))
    out_shape = jax.ShapeDtypeStruct((bsz, N_HEADS, length, HEAD_DIM), BF16)
    return pl.pallas_call(
        functools.partial(_gdn_body, n_chunks=nc, n_ctx_chunks=ncc, heads=hb, n_blocks=nt),
        grid=(bsz, N_HEADS // hb, nt + 1),
        in_specs=[fwd(0, prep), fwd(N_HEADS, prep), fwd(2 * N_HEADS, scan),
                  pl.BlockSpec(gblk, lambda b, h, t: (b, 0, 0, prep(t))),
                  bwd(0, prep), bwd(N_HEADS, prep), bwd(2 * N_HEADS, scan),
                  pl.BlockSpec(gblk, lambda b, h, t: (b, 0, 0, nt - 1 - prep(t))),
                  ctx(N_HEADS), ctx(2 * N_HEADS),
                  pl.BlockSpec((1, 3, 2 * N_HEADS, lc), lambda b, h, t: (0, 0, 0, b))],
        out_specs=[pl.BlockSpec(blk, lambda b, h, t: (b, h, scan(t), 0)),
                   pl.BlockSpec(blk, lambda b, h, t: (b, h, nt - 1 - scan(t), 0))],
        out_shape=[out_shape, out_shape],
        scratch_shapes=[pltpu.VMEM((2 * hb, HEAD_DIM, HEAD_DIM), F32),
                        pltpu.VMEM((slots, 2 * CHUNK, HEAD_DIM), BF16),
                        pltpu.VMEM((slots, CHUNK, CHUNK), BF16),
                        pltpu.VMEM((slots, HEAD_DIM + CHUNK, CHUNK), BF16),
                        pltpu.VMEM((slots, 1, HEAD_DIM), F32)],
        compiler_params=pltpu.CompilerParams(
            dimension_semantics=("parallel", "parallel", "arbitrary"),
            vmem_limit_bytes=VMEM_LIMIT),
    )(qkv, qkv, qkv, gates, qkv, qkv, qkv, gates, qkv_ctx, qkv_ctx, gates_ctx)


def _out_body(oaf_ref, oab_ref, obf_ref, obb_ref, za_ref, zb_ref, naw_ref, nbw_ref,
              w_ref, x_ref, gate_ref, fw_ref, o_ref):
    parts = []
    for of_ref, ob_ref, z_ref, nw_ref in ((oaf_ref, oab_ref, za_ref, naw_ref),
                                          (obf_ref, obb_ref, zb_ref, nbw_ref)):
        for h in range(N_HEADS):
            o = of_ref[0, h].astype(F32) + ob_ref[0, h].astype(F32)
            o = o * lax.rsqrt(jnp.mean(o * o, axis=-1, keepdims=True) + NORM_EPS) * nw_ref[h:h + 1, :]
            hz = 0.5 * z_ref[0, h].astype(F32)
            parts.append(((hz + hz * jnp.tanh(hz)) * o).astype(BF16))
    y = jnp.concatenate(parts, axis=1)
    xo = x_ref[0] + gate_ref[0] * _dot(y, w_ref[...])
    ms = jnp.mean(xo * xo, axis=-1, keepdims=True)
    o_ref[0] = xo * lax.rsqrt(ms + NORM_EPS) * fw_ref[...]


def _out_stage(oa_f, oa_b, ob_f, ob_b, yh, na_w, nb_w, w_out, x, gate, final_w, tm):
    bsz, length, d = x.shape
    hblk = (1, N_HEADS, tm, HEAD_DIM)
    ospec = pl.BlockSpec(hblk, lambda b, m: (b, 0, m, 0))
    full2 = lambda a: pl.BlockSpec(a.shape, lambda b, m: (0, 0))
    return pl.pallas_call(
        _out_body,
        grid=(bsz, length // tm),
        in_specs=[ospec, ospec, ospec, ospec,
                  pl.BlockSpec(hblk, lambda b, m: (b, OFF_AZ // N_HEADS, m, 0)),
                  pl.BlockSpec(hblk, lambda b, m: (b, OFF_BZ // N_HEADS, m, 0)),
                  full2(na_w), full2(nb_w), full2(w_out),
                  pl.BlockSpec((1, tm, d), lambda b, m: (b, m, 0)),
                  pl.BlockSpec((1, 1, d), lambda b, m: (b, 0, 0)),
                  full2(final_w)],
        out_specs=pl.BlockSpec((1, tm, d), lambda b, m: (b, m, 0)),
        out_shape=jax.ShapeDtypeStruct((bsz, length, d), F32),
        compiler_params=pltpu.CompilerParams(dimension_semantics=("parallel", "parallel"),
                                             vmem_limit_bytes=VMEM_LIMIT),
    )(oa_f, oa_b, ob_f, ob_b, yh, yh, na_w, nb_w, w_out, x, gate, final_w)


def kernel(x, c, ctx, c_ctx, norm_w, ada_w, ada_b, w_in, conv_w, hg_lb_logits, gdn_a_log,
           gdn_dt_bias, ha_norm_w, hb_norm_w, w_out, final_norm_w):
    bsz, length, d = x.shape
    lc = ctx.shape[1]
    assert d == D_MODEL and length % 512 == 0 and length % GRID_W == 0 and lc % CHUNK == 0
    assert w_in.shape[0] == 1, "single-layer block"

    n_cond = bsz + 1
    cond = jnp.concatenate([c, c_ctx[None, :], jnp.zeros((-n_cond % 8, d), F32)], axis=0)
    mod = _adaln(cond, ada_w[0], ada_b[0])
    shift, scale, gate = mod[:, :d], mod[:, d:2 * d], mod[:, 2 * d:]
    lat = lambda m: m[:bsz, None, :]
    of_ctx = lambda m: m[bsz:bsz + 1, None, :]

    w_main = w_in[0].astype(BF16)
    w_gate_t = w_main[:, N_MAIN:].T
    nw = norm_w[0].reshape(1, d)
    yh, gates = _inproj(x, nw, lat(scale), lat(shift), w_main, w_gate_t, tm=min(IN_TM, length))
    yh_c, gates_c = _inproj(ctx.reshape(1, bsz * lc, d), nw, of_ctx(scale), of_ctx(shift), w_main, w_gate_t,
                            tm=bsz * lc)

    lb = jax.nn.softmax(hg_lb_logits.astype(F32), axis=0)[0]
    lb = lb.reshape(2, N_HEADS, HEAD_DIM).transpose(1, 0, 2)
    oa_f, oa_b = _hgrn2_scan(yh, yh_c, lb, min(HG_TB, length))

    cw = conv_w[0].reshape(9, 3 * N_HEADS, HEAD_DIM).transpose(1, 0, 2)
    qkv = _gdn_conv(yh, cw, two_d=True)
    qkv_c = _gdn_conv(yh_c, cw, two_d=False, width=lc)
    params = jnp.stack([gdn_a_log[0].reshape(-1), gdn_dt_bias[0].reshape(-1)], axis=1).astype(F32)
    ob_f, ob_b = _gdn_scan(qkv, _gate_rows(gates, params), qkv_c, _gate_rows(gates_c, params),
                           min(GDN_TB, length))

    return _out_stage(oa_f, oa_b, ob_f, ob_b, yh, ha_norm_w[0], hb_norm_w[0],
                      w_out[0].astype(BF16), x, lat(gate), final_norm_w.reshape(1, d),
                      tm=min(OUT_TM, length))
```

```python
import functools

import jax
import jax.numpy as jnp
import numpy as np
from jax import lax
from jax.experimental import pallas as pl
from jax.experimental.pallas import tpu as pltpu

F32 = jnp.float32
BF16 = jnp.bfloat16

D_MODEL = 1024
N_HEADS = 8
HEAD_DIM = 128
CHUNK = 64
GRID_W = 64
NORM_EPS = 1e-6
N_MAIN = 72 * HEAD_DIM
N_GATE = 4 * N_HEADS
OFF_AQ, OFF_AFF, OFF_AFB, OFF_AI, OFF_AZ, OFF_BQ, OFF_BZ = 0, 8, 16, 24, 32, 40, 64
NEG_BIG = -1e30
F32_TINY = float(np.finfo(np.float32).tiny)
VMEM_LIMIT = 56 * 1024 * 1024
IN_TM, IN_TN = 2048, 1024
HG_TB, GDN_TB = 4096, 512
CONV_RT = 512
OUT_TM = 512
GATE_TG = 2048


def _dot(a, b):
    return jnp.dot(a, b, preferred_element_type=F32)


def _dot_nt(a, b):
    return lax.dot_general(a, b, (((1,), (1,)), ((), ())), preferred_element_type=F32)


def _sigmoid(x):
    return 1.0 / (1.0 + jnp.exp(-x))


def _silu(x):
    return x * _sigmoid(x)


def _adaln_body(c_ref, w_ref, b_ref, o_ref):
    o_ref[...] = _dot(_silu(c_ref[...]), w_ref[...]) + b_ref[...]


def _adaln(cond, ada_w, ada_b):
    rows, d = cond.shape
    n = ada_w.shape[1]
    tn = 1024
    return pl.pallas_call(
        _adaln_body,
        grid=(n // tn,),
        in_specs=[pl.BlockSpec((rows, d), lambda j: (0, 0)),
                  pl.BlockSpec((d, tn), lambda j: (0, j)),
                  pl.BlockSpec((1, tn), lambda j: (0, j))],
        out_specs=pl.BlockSpec((rows, tn), lambda j: (0, j)),
        out_shape=jax.ShapeDtypeStruct((rows, n), F32),
        compiler_params=pltpu.CompilerParams(dimension_semantics=("arbitrary",),
                                             vmem_limit_bytes=VMEM_LIMIT),
    )(cond, ada_w, ada_b.reshape(1, n))


def _inproj_body(x_ref, nw_ref, sc_ref, sh_ref, w_ref, wg_ref, y_ref, yg_ref, h_scr, *, tn):
    @pl.when(pl.program_id(2) == 0)
    def _():
        x = x_ref[0]
        ms = jnp.mean(x * x, axis=-1, keepdims=True)
        h = x * lax.rsqrt(ms + NORM_EPS) * nw_ref[...]
        h = (h * (1.0 + sc_ref[0]) + sh_ref[0]).astype(BF16)
        h_scr[...] = h
        yg_ref[0] = _dot_nt(wg_ref[...], h)

    acc = _dot(h_scr[...], w_ref[...])
    for j in range(tn // HEAD_DIM):
        y_ref[0, j] = acc[:, j * HEAD_DIM:(j + 1) * HEAD_DIM].astype(y_ref.dtype)


def _inproj(x, norm_w, scale, shift, w_main, w_gate_t, tm):
    bsz, length, d = x.shape
    tn = IN_TN
    grid = (bsz, length // tm, N_MAIN // tn)
    return pl.pallas_call(
        functools.partial(_inproj_body, tn=tn),
        grid=grid,
        in_specs=[pl.BlockSpec((1, tm, d), lambda b, m, n: (b, m, 0)),
                  pl.BlockSpec((1, d), lambda b, m, n: (0, 0)),
                  pl.BlockSpec((1, 1, d), lambda b, m, n: (b, 0, 0)),
                  pl.BlockSpec((1, 1, d), lambda b, m, n: (b, 0, 0)),
                  pl.BlockSpec((d, tn), lambda b, m, n: (0, n)),
                  pl.BlockSpec((N_GATE, d), lambda b, m, n: (0, 0))],
        out_specs=[pl.BlockSpec((1, tn // HEAD_DIM, tm, HEAD_DIM), lambda b, m, n: (b, n, m, 0)),
                   pl.BlockSpec((1, N_GATE, tm), lambda b, m, n: (b, 0, m))],
        out_shape=[jax.ShapeDtypeStruct((bsz, N_MAIN // HEAD_DIM, length, HEAD_DIM), BF16),
                   jax.ShapeDtypeStruct((bsz, N_GATE, length), F32)],
        scratch_shapes=[pltpu.VMEM((tm, d), BF16)],
        compiler_params=pltpu.CompilerParams(
            dimension_semantics=("parallel", "parallel", "arbitrary"),
            vmem_limit_bytes=VMEM_LIMIT),
    )(x, norm_w, scale, shift, w_main, w_gate_t)


CONV_PAD = 72
CONV_HALO = 8


def _conv_body(x_ref, w_ref, o_ref, pad_scr, *, length, width, two_d, rt):
    blk = pl.program_id(1)
    zeros = jnp.zeros((CONV_PAD, HEAD_DIM), F32)
    pad_scr[0:CONV_PAD, :] = zeros
    pad_scr[CONV_PAD + length:CONV_PAD + length + CONV_PAD, :] = zeros
    pad_scr[CONV_PAD:CONV_PAD + length, :] = x_ref[0, 0].astype(F32)
    w = 0.5 * w_ref[0]
    win_rows = rt + 2 * CONV_HALO
    is_q = blk < N_HEADS
    is_qk = blk < 2 * N_HEADS
    col = lax.broadcasted_iota(jnp.int32, (rt, HEAD_DIM), 0) & (width - 1)

    def tile(i, carry):
        s = pl.multiple_of(i * rt, rt)
        sums = [None, None, None]
        for dr in ((-1, 0, 1) if two_d else (0,)):
            base = pl.multiple_of(s + (CONV_PAD + dr * width - CONV_HALO), 8)
            win = pad_scr[pl.ds(base, win_rows), :]
            for j in range(3):
                tap = (dr + 1) * 3 + j
                term = win * w[tap:tap + 1, :]
                sums[j] = term if sums[j] is None else sums[j] + term
        inner = slice(CONV_HALO, CONV_HALO + rt)
        left = pltpu.roll(sums[0], 1, 0)[inner]
        right = pltpu.roll(sums[2], win_rows - 1, 0)[inner]
        hs = sums[1][inner] + jnp.where(col >= 1, left, 0.0) + jnp.where(col <= width - 2, right, 0.0)
        a = hs + hs * jnp.tanh(hs)
        nrm = lax.rsqrt(jnp.sum(a * a, axis=-1, keepdims=True) + NORM_EPS)
        f = jnp.where(is_q, nrm * HEAD_DIM ** -0.5, jnp.where(is_qk, nrm, 1.0))
        o_ref[0, 0, pl.ds(s, rt), :] = (a * f).astype(o_ref.dtype)
        return carry

    lax.fori_loop(0, length // rt, tile, 0)


def _gdn_conv(yh, conv_w, two_d, width=GRID_W):
    bsz, _, length, _ = yh.shape
    assert width & (width - 1) == 0, "grid width must be a power of two (column index by bit mask)"
    rt = min(CONV_RT, length)
    assert rt % width == 0
    nblk = 3 * N_HEADS
    return pl.pallas_call(
        functools.partial(_conv_body, length=length, width=width, two_d=two_d, rt=rt),
        grid=(bsz, nblk),
        in_specs=[pl.BlockSpec((1, 1, length, HEAD_DIM), lambda b, j: (b, OFF_BQ + j, 0, 0)),
                  pl.BlockSpec((1, 9, HEAD_DIM), lambda b, j: (j, 0, 0))],
        out_specs=pl.BlockSpec((1, 1, length, HEAD_DIM), lambda b, j: (b, j, 0, 0)),
        out_shape=jax.ShapeDtypeStruct((bsz, nblk, length, HEAD_DIM), BF16),
        scratch_shapes=[pltpu.VMEM((length + 2 * CONV_PAD, HEAD_DIM), F32)],
        compiler_params=pltpu.CompilerParams(dimension_semantics=("parallel", "parallel"),
                                             vmem_limit_bytes=VMEM_LIMIT),
    )(yh, conv_w)


HG_LEVELS = (32, 16, 8, 4, 2, 1)
HG_PAIRS = ((0, 1), (2, 3), (4, 5))
HG_GROUP = 4


def _hg_constants():
    t = np.arange(CHUNK)[:, None]
    col = np.arange(HEAD_DIM)[None, :]
    s = col % CHUNK
    out = [np.broadcast_to(np.where((t & m) != 0, 1.0, -1.0), (CHUNK, HEAD_DIM)) for m in HG_LEVELS]
    for rev in (False, True):
        for pa, pb in HG_PAIRS:
            keep = np.zeros((CHUNK, HEAD_DIM), bool)
            for half, lv in ((col < CHUNK, pa), (col >= CHUNK, pb)):
                m = HG_LEVELS[lv]
                split = ((t ^ s) >> (m.bit_length() - 1)) == 1
                t_hi = (t & m) != 0
                keep |= half & split & (~t_hi if rev else t_hi)
            out.append(keep)
    out.append((col == t) & (col < CHUNK))
    return jnp.asarray(np.stack([np.asarray(o, np.float32) for o in out]))


def _hg_triangles():
    t = np.arange(CHUNK)
    lower = (t[None, :] <= t[:, None]).astype(np.float32)
    return jnp.asarray(np.stack([np.tile(lower, (1, 3)), np.tile(lower.T, (1, 3))]), BF16)


def _cumsum_rows(g, tri3):
    hi = g.astype(BF16)
    r1 = g - hi.astype(F32)
    mid = r1.astype(BF16)
    lo = (r1 - mid.astype(F32)).astype(BF16)
    return _dot(tri3, jnp.concatenate([hi, mid, lo], axis=0))


def _level_operand(b, b_ref, q, k, m, rev):
    parts = []
    for blk in range(CHUNK // m):
        rows = slice(blk * m, (blk + 1) * m)
        r = (blk // 2) * 2 * m + m
        ref = jnp.broadcast_to(b_ref[r:r + 1, :], (m, HEAD_DIM))
        q_side = (blk % 2 == 1) != rev
        e = (b[rows] - ref) if q_side else (ref - b[rows])
        parts.append(((q if q_side else k)[rows] * jnp.exp2(e)).astype(BF16))
    return jnp.concatenate(parts, axis=0)


def _level_ref(b_ref, m, sub):
    bc = lambda r, n: jnp.broadcast_to(b_ref[r:r + 1, :], (n, HEAD_DIM))
    if m >= 4:
        return jnp.concatenate([bc(blk * 2 * m + m, 2 * m) for blk in range(CHUNK // (2 * m))], axis=0)
    lo = jnp.concatenate([bc(v * 8 + 2, 8) for v in range(CHUNK // 8)], axis=0)
    hi = jnp.concatenate([bc(v * 8 + 6, 8) for v in range(CHUNK // 8)], axis=0)
    return jnp.where(sub < 4, lo, hi)


def _hg_prepare(items, readout, cst_ref, tri_ref, b_scr, sc_scr, vv_scr, u_scr, dec_scr):
    row = lax.broadcasted_iota(jnp.int32, (CHUNK, HEAD_DIM), 0)
    sub = row & 7
    even = (row & 1) == 0
    gs, ks, bs = [], [], []
    for i, (_, _, load_f, _, lb, rev) in enumerate(items):
        gate = (1.0 - lb) * (0.5 + 0.5 * jnp.tanh(0.5 * load_f()))
        g = jnp.log2(jnp.maximum(lb + gate, F32_TINY))
        gs.append(g)
        ks.append((1.0 - lb) - gate)
        b = _cumsum_rows(g, tri_ref[1 if rev else 0])
        bs.append(b)
        b_scr[i] = b
    for i, (slot, _, _, load_v, _, rev) in enumerate(items):
        b = bs[i]
        btot = b[0:1, :] if rev else b[CHUNK - 1:CHUNK, :]
        v = load_v()
        vv_t = jnp.concatenate([v, v], axis=0).T.astype(BF16)
        k_dec = (ks[i] * jnp.exp2(btot - b)).astype(BF16)
        u_scr[slot] = _dot(vv_t[:, :CHUNK], k_dec)
        dec_scr[slot] = jnp.exp2(btot)
        vv_scr[slot] = vv_t
    if not readout:
        return

    qs = []
    for _, load_q, _, _, _, _ in items:
        q_raw = load_q()
        hs = (0.5 * HEAD_DIM ** -0.5) * q_raw
        qs.append(hs + hs * jnp.tanh(0.5 * q_raw))
    scores = [jnp.zeros((CHUNK, HEAD_DIM), F32) for _ in items]
    zero_blk = jnp.zeros((CHUNK, HEAD_DIM), BF16)
    for pi, pair in enumerate(HG_PAIRS):
        gps = []
        for i, (_, _, _, _, _, rev) in enumerate(items):
            a_mats = []
            for lv in pair:
                m = HG_LEVELS[lv]
                if m >= 8:
                    a_mats.append(_level_operand(bs[i], b_scr.at[i], qs[i], ks[i], m, rev))
                    continue
                sign = cst_ref[lv]
                if m == 1:
                    x = jnp.exp2(jnp.where(even, gs[i] if rev else pltpu.roll(gs[i], CHUNK - 1, 0), 0.0))
                else:
                    d = bs[i] - _level_ref(b_scr.at[i], m, sub)
                    x = jnp.exp2((-d if rev else d) * sign)
                t_is_q = (sign < 0.0) if rev else (sign > 0.0)
                a_mats.append((jnp.where(t_is_q, qs[i], ks[i]) * x).astype(BF16))
            lhs = jnp.concatenate(a_mats, axis=1)
            rhs = jnp.concatenate([jnp.concatenate([a_mats[0], zero_blk], axis=1),
                                   jnp.concatenate([zero_blk, a_mats[1]], axis=1)], axis=0)
            gps.append(_dot_nt(lhs, rhs))
        for i, (_, _, _, _, _, rev) in enumerate(items):
            scores[i] = scores[i] + gps[i] * cst_ref[6 + (3 if rev else 0) + pi]
    for i, (slot, _, _, _, _, _) in enumerate(items):
        diag = jnp.sum(qs[i] * ks[i], axis=-1, keepdims=True) * cst_ref[12]
        sc_scr[slot, :, :HEAD_DIM] = (scores[i] + diag).astype(BF16)
        sc_scr[slot, :, HEAD_DIM:] = (qs[i] * jnp.exp2(bs[i])).astype(BF16)


def _hg_recur(steps, dirs, readout, sc_scr, vv_scr, u_scr, dec_scr):
    states = [s_scr[...] for s_scr, _, _ in dirs]
    for c in steps:
        for i, (_, slot_of, store_o) in enumerate(dirs):
            slot = slot_of(c)
            if readout:
                rhs = jnp.concatenate([vv_scr[slot], states[i].astype(BF16)], axis=1)
                store_o(c, _dot_nt(sc_scr[slot], rhs))
            states[i] = states[i] * dec_scr[slot] + u_scr[slot]
    for (s_scr, _, _), st in zip(dirs, states):
        s_scr[...] = st


def _hg_body(cst_ref, tri_ref, qf_ref, ff_ref, vf_ref, qb_ref, fb_ref, vb_ref,
             cff_ref, cfb_ref, cv_ref, lb_ref, of_ref, ob_ref,
             sf_scr, sb_scr, b_scr, sc_scr, vv_scr, u_scr, dec_scr, *, n_chunks, n_ctx_chunks):
    lb_f = lb_ref[0, 0:1, :]
    lb_b = lb_ref[0, 1:2, :]
    scr = (sc_scr, vv_scr, u_scr, dec_scr)
    load = lambda ref, rows: (lambda: ref[0, 0, rows, :].astype(F32))

    @pl.when(pl.program_id(2) == 0)
    def _():
        sf_scr[...] = jnp.zeros_like(sf_scr)
        sb_scr[...] = jnp.zeros_like(sb_scr)
        n = n_ctx_chunks
        for c0 in range(0, n, HG_GROUP):
            items = []
            for c in range(c0, min(c0 + HG_GROUP, n)):
                sl = slice(c * CHUNK, (c + 1) * CHUNK)
                items.append((c, None, load(cff_ref, sl), load(cv_ref, sl), lb_f, False))
                items.append((n + c, None, load(cfb_ref, sl), load(cv_ref, sl), lb_b, True))
            _hg_prepare(items, False, cst_ref, tri_ref, b_scr, *scr)
        _hg_recur(range(n), [(sf_scr, lambda c: c, None), (sb_scr, lambda c: 2 * n - 1 - c, None)],
                  False, *scr)

    n = n_chunks
    rows = lambda c: slice(c * CHUNK, (c + 1) * CHUNK)

    def prepare(g):
        items = []
        for p in range(g * HG_GROUP, (g + 1) * HG_GROUP):
            rf, rb = rows(p), rows(n - 1 - p)
            items.append((p, load(qf_ref, rf), load(ff_ref, rf), load(vf_ref, rf), lb_f, False))
            items.append((2 * n - 1 - p, load(qb_ref, rb), load(fb_ref, rb), load(vb_ref, rb), lb_b, True))
        _hg_prepare(items, True, cst_ref, tri_ref, b_scr, *scr)

    def store_f(c, o):
        of_ref[0, 0, rows(c), :] = o.astype(of_ref.dtype)

    def store_b(c, o):
        ob_ref[0, 0, rows(n - 1 - c), :] = o.astype(ob_ref.dtype)

    def recur(g):
        _hg_recur(range(g * HG_GROUP, (g + 1) * HG_GROUP),
                  [(sf_scr, lambda c: c, store_f), (sb_scr, lambda c: 2 * n - 1 - c, store_b)], True, *scr)

    groups = n // HG_GROUP
    prepare(0)
    for g in range(1, groups):
        recur(g - 1)
        prepare(g)
    recur(groups - 1)


def _hgrn2_scan(yh, yh_ctx, lb, tb):
    bsz, _, length, _ = yh.shape
    lc = yh_ctx.shape[2] // bsz
    nt = length // tb
    nc, ncc = tb // CHUNK, lc // CHUNK
    assert nc % HG_GROUP == 0
    slots = 2 * max(nc, ncc)
    blk = (1, 1, tb, HEAD_DIM)
    cblk = (1, 1, lc, HEAD_DIM)
    fwd = lambda off: pl.BlockSpec(blk, lambda b, h, t: (b, off + h, t, 0))
    bwd = lambda off: pl.BlockSpec(blk, lambda b, h, t: (b, off + h, nt - 1 - t, 0))
    ctx = lambda off: pl.BlockSpec(cblk, lambda b, h, t: (0, off + h, b, 0))
    consts = _hg_constants()
    tri3 = _hg_triangles()
    out_shape = jax.ShapeDtypeStruct((bsz, N_HEADS, length, HEAD_DIM), BF16)
    return pl.pallas_call(
        functools.partial(_hg_body, n_chunks=nc, n_ctx_chunks=ncc),
        grid=(bsz, N_HEADS, nt),
        in_specs=[pl.BlockSpec(consts.shape, lambda b, h, t: (0, 0, 0)),
                  pl.BlockSpec(tri3.shape, lambda b, h, t: (0, 0, 0)),
                  fwd(OFF_AQ), fwd(OFF_AFF), fwd(OFF_AI), bwd(OFF_AQ), bwd(OFF_AFB), bwd(OFF_AI),
                  ctx(OFF_AFF), ctx(OFF_AFB), ctx(OFF_AI),
                  pl.BlockSpec((1, 2, HEAD_DIM), lambda b, h, t: (h, 0, 0))],
        out_specs=[pl.BlockSpec(blk, lambda b, h, t: (b, h, t, 0)),
                   pl.BlockSpec(blk, lambda b, h, t: (b, h, nt - 1 - t, 0))],
        out_shape=[out_shape, out_shape],
        scratch_shapes=[pltpu.VMEM((HEAD_DIM, HEAD_DIM), F32), pltpu.VMEM((HEAD_DIM, HEAD_DIM), F32),
                        pltpu.VMEM((2 * HG_GROUP, CHUNK, HEAD_DIM), F32),
                        pltpu.VMEM((slots, CHUNK, 2 * HEAD_DIM), BF16),
                        pltpu.VMEM((slots, HEAD_DIM, HEAD_DIM), BF16),
                        pltpu.VMEM((slots, HEAD_DIM, HEAD_DIM), F32),
                        pltpu.VMEM((slots, 1, HEAD_DIM), F32)],
        compiler_params=pltpu.CompilerParams(
            dimension_semantics=("parallel", "parallel", "arbitrary"),
            vmem_limit_bytes=VMEM_LIMIT),
    )(consts, tri3, yh, yh, yh, yh, yh, yh, yh_ctx, yh_ctx, yh_ctx, lb)


def _seg_cumsum_lanes(x, rev, lane):
    total = x.shape[1]
    seg = lane & (CHUNK - 1)
    for sh in (1, 2, 4, 8, 16, 32):
        if rev:
            x = x + jnp.where(seg < CHUNK - sh, pltpu.roll(x, total - sh, 1), 0.0)
        else:
            x = x + jnp.where(seg >= sh, pltpu.roll(x, sh, 1), 0.0)
    return x


def _gate_rows_body(g_ref, p_ref, o_ref):
    nrow = 2 * N_HEADS
    total = g_ref.shape[2]
    lane = lax.broadcasted_iota(jnp.int32, (nrow, total), 1)
    row = lax.broadcasted_iota(jnp.int32, (nrow, total), 0)
    z = g_ref[0, :nrow, :] + p_ref[:, 1:2]
    softplus = jnp.maximum(z, 0.0) + jnp.log1p(jnp.exp(-jnp.abs(z)))
    g = -jnp.exp(p_ref[:, 0:1]) * softplus
    prefix = _seg_cumsum_lanes(g, False, lane)
    suffix = _seg_cumsum_lanes(g, True, lane)
    o_ref[0, 0] = jnp.where(row < N_HEADS, prefix, suffix)
    o_ref[0, 1] = _sigmoid(g_ref[0, nrow:, :])
    o_ref[0, 2] = prefix + suffix - g


def _gate_rows(gates, params):
    bsz, _, length = gates.shape
    tg = min(GATE_TG, length)
    nrow = 2 * N_HEADS
    return pl.pallas_call(
        _gate_rows_body,
        grid=(bsz, length // tg),
        in_specs=[pl.BlockSpec((1, N_GATE, tg), lambda b, t: (b, 0, t)),
                  pl.BlockSpec(params.shape, lambda b, t: (0, 0))],
        out_specs=pl.BlockSpec((1, 3, nrow, tg), lambda b, t: (b, 0, 0, t)),
        out_shape=jax.ShapeDtypeStruct((bsz, 3, nrow, length), F32),
        compiler_params=pltpu.CompilerParams(dimension_semantics=("parallel", "parallel"),
                                             vmem_limit_bytes=VMEM_LIMIT),
    )(gates, params)


def _gdn_gates(g_ref, d, head):
    r = pl.ds(d * N_HEADS + head, 1)
    b = g_ref[0, 0, r, :]
    rows = jnp.concatenate([b, g_ref[0, 1, r, :], g_ref[0, 2, r, :]], axis=0)
    return rows, jnp.broadcast_to(b, (HEAD_DIM, b.shape[1])).T


GDN_QUAD = 4
GDN_PREP_HEADS = 4


def _gdn_prepare(quads, readout, qkg_scr, tb_scr, kp_scr, dec_scr):
    wide = GDN_QUAD * CHUNK
    t_i = lax.broadcasted_iota(jnp.int32, (CHUNK, wide), 0)
    lane = lax.broadcasted_iota(jnp.int32, (CHUNK, wide), 1)
    s_i = lane & (CHUNK - 1)
    blk = [(lane >= j * CHUNK) & (lane < (j + 1) * CHUNK) for j in range(GDN_QUAD)]
    eye = (t_i == s_i).astype(F32)
    tri = {False: (s_i <= t_i, s_i < t_i), True: (s_i >= t_i, s_i > t_i)}
    rows_of = lambda j: slice(j * CHUNK, (j + 1) * CHUNK)

    def diag_blocks(m, off):
        out = jnp.where(blk[0], m[off:off + CHUNK], 0.0)
        for j in range(1, GDN_QUAD):
            out = jnp.where(blk[j], m[off + j * CHUNK:off + (j + 1) * CHUNK], out)
        return out

    def block_diag(y):
        return jnp.concatenate([jnp.where(blk[j], y, 0.0).astype(BF16) for j in range(GDN_QUAD)], axis=0)

    dmats, qks, xs, ys = [], [], [], []
    for _, load_q, load_k, rows, bcol, rev in quads:
        incl, strict = tri[rev]
        bc = jnp.concatenate([bcol, bcol], axis=1)
        dmat = jnp.exp(jnp.where(incl, diag_blocks(bc, 0) - rows[0:1, :], NEG_BIG))
        kb = load_k()
        qb = load_q() if readout else None
        pair = 2 * CHUNK
        lo_half = lax.broadcasted_iota(jnp.int32, (CHUNK, pair), 1) < CHUNK
        pair_diag = lambda m: jnp.where(lo_half, m[:CHUNK], m[CHUNK:])
        kk_parts, qk_parts = [], []
        for p in range(GDN_QUAD // 2):
            kp = kb[p * pair:(p + 1) * pair]
            if readout:
                qkk = _dot_nt(jnp.concatenate([qb[p * pair:(p + 1) * pair], kp], axis=0), kp)
                qk_parts.append(pair_diag(qkk[:pair]))
                kk_parts.append(pair_diag(qkk[pair:]))
            else:
                kk_parts.append(pair_diag(_dot_nt(kp, kp)))
        kk = jnp.concatenate(kk_parts, axis=1)
        if readout:
            qks.append(jnp.concatenate(qk_parts, axis=1))
        n_mat = jnp.where(strict, kk * dmat, 0.0) * rows[1:2, :]
        dmats.append(dmat)
        xs.append(eye)
        ys.append(n_mat)
    for m in (1, 2, 4, 8, 16, 32):
        opposite = ((t_i ^ s_i) >> (m.bit_length() - 1)) == 1
        later = (t_i & m) != 0
        lvl = {False: opposite & later, True: opposite & jnp.logical_not(later)}
        n_lvl = [jnp.where(lvl[q[5]], y, 0.0) for q, y in zip(quads, ys)]
        if m == 1:
            inner = n_lvl
        else:
            inner = [_dot(nl.astype(BF16), block_diag(x)) for nl, x in zip(n_lvl, xs)]
        xs = [x - _dot(x.astype(BF16), block_diag(a)) for x, a in zip(xs, inner)]

    for i, (slot0, load_q, load_k, rows, bcol, rev) in enumerate(quads):
        k_all = load_k().astype(F32)
        e_col = jnp.exp(bcol)
        qkg_all_k = (k_all * e_col).astype(BF16)
        if readout:
            qkg_all_q = (load_q().astype(F32) * e_col).astype(BF16)
            pm_all = qks[i] * dmats[i] * rows[1:2, :]
        k_t = k_all.T
        kdt_all = (k_t * (jnp.exp(rows[2:3, :] - rows[0:1, :]) * rows[1:2, :])).astype(BF16)
        for j in range(GDN_QUAD):
            slot = slot0 + j
            shift = (wide - j * CHUNK) % wide
            take = lambda m: (m if shift == 0 else pltpu.roll(m, shift, 1))[:, :CHUNK]
            tb_scr[slot] = take(xs[i]).astype(BF16)
            qkg_scr[slot, CHUNK:, :] = qkg_all_k[rows_of(j)]
            if readout:
                qkg_scr[slot, :CHUNK, :] = qkg_all_q[rows_of(j)]
                kp_scr[slot, HEAD_DIM:, :] = take(pm_all).astype(BF16)
            kp_scr[slot, :HEAD_DIM, :] = kdt_all[:, j * CHUNK:(j + 1) * CHUNK]
            tot = rows[2:3, j * CHUNK:(j + 1) * CHUNK]
            dec_scr[slot] = jnp.exp(jnp.concatenate([tot, tot], axis=1))


def _gdn_recur(steps, dirs, readout, qkg_scr, tb_scr, kp_scr, dec_scr):
    def step(c, carry):
        slots = [slot_of(c) for _, slot_of, _, _ in dirs]
        states = [s_ref[...] for s_ref, _, _, _ in dirs]
        if readout:
            qks = [_dot(qkg_scr[slot], s.astype(BF16)) for slot, s in zip(slots, states)]
            kss = [p[CHUNK:] for p in qks]
        else:
            kss = [_dot(qkg_scr[slot, CHUNK:, :], s.astype(BF16)) for slot, s in zip(slots, states)]
        ws = [_dot(tb_scr[slot], (load_v(c) - ks).astype(BF16)).astype(BF16)
              for slot, ks, (_, _, load_v, _) in zip(slots, kss, dirs)]
        if readout:
            res = [_dot(kp_scr[slot], w) for slot, w in zip(slots, ws)]
            for slot, s, r, p, (s_ref, _, _, store_o) in zip(slots, states, res, qks, dirs):
                s_ref[...] = s * dec_scr[slot] + r[:HEAD_DIM]
                store_o(c, p[:CHUNK] + r[HEAD_DIM:])
        else:
            for slot, s, w, (s_ref, _, _, _) in zip(slots, states, ws, dirs):
                s_ref[...] = s * dec_scr[slot] + _dot(kp_scr[slot, :HEAD_DIM, :], w)
        return carry

    if isinstance(steps, int):
        lax.fori_loop(0, steps, step, 0, unroll=4)
    else:
        for c in steps:
            step(c, 0)


def _gdn_body(qf_ref, kf_ref, vf_ref, gf_ref, qb_ref, kb_ref, vb_ref, gb_ref,
              ck_ref, cv_ref, cg_ref, of_ref, ob_ref,
              s_scr, qkg_scr, tb_scr, kp_scr, dec_scr,
              *, n_chunks, n_ctx_chunks, heads, n_blocks):
    t = pl.program_id(2)
    h0 = pl.program_id(1) * heads
    scr = (qkg_scr, tb_scr, kp_scr, dec_scr)
    set_slots = 2 * heads * n_chunks

    def rows(c):
        return slice(c * CHUNK, (c + 1) * CHUNK) if isinstance(c, int) else pl.ds(pl.multiple_of(c * CHUNK, CHUNK), CHUNK)

    def chunk_list(hh, n, base, q_refs, k_refs, g_refs):
        head = h0 + hh
        wide = GDN_QUAD * CHUNK
        out = []
        for d in (0, 1):
            gate_rows, bcol = _gdn_gates(g_refs[d], d, head)
            for c0 in range(0, n, GDN_QUAD):
                sl = slice(c0 * CHUNK, c0 * CHUNK + wide)
                load_q = None if q_refs is None else (lambda r=q_refs[d], sl=sl: r[0, hh, sl, :])
                load_k = lambda r=k_refs[d], sl=sl: r[0, hh, sl, :]
                out.append((base + (2 * hh + d) * n + c0, load_q, load_k, gate_rows[:, sl], bcol[sl, :], d == 1))
        return out

    def prepare(hp, n, base, q_refs, k_refs, g_refs, readout):
        quads = []
        for i in range(GDN_PREP_HEADS):
            quads += chunk_list(hp * GDN_PREP_HEADS + i, n, base, q_refs, k_refs, g_refs)
        _gdn_prepare(quads, readout, *scr)

    def chains(n, base, vf, vb, of, ob):
        out = []
        for hh in range(heads):
            out.append((s_scr.at[2 * hh], lambda c, hh=hh: base + 2 * hh * n + c,
                        lambda c, hh=hh: vf[0, hh, rows(c), :].astype(F32),
                        None if of is None else (lambda c, o, hh=hh: of.__setitem__(
                            (0, hh, rows(c), slice(None)), o.astype(of.dtype)))))
            out.append((s_scr.at[2 * hh + 1], lambda c, hh=hh: base + (2 * hh + 1) * n + (n - 1 - c),
                        lambda c, hh=hh: vb[0, hh, rows(n - 1 - c), :].astype(F32),
                        None if ob is None else (lambda c, o, hh=hh: ob.__setitem__(
                            (0, hh, rows(n - 1 - c), slice(None)), o.astype(ob.dtype)))))
        return out

    main_refs = ((qf_ref, qb_ref), (kf_ref, kb_ref), (gf_ref, gb_ref))
    groups = heads // GDN_PREP_HEADS
    n = n_chunks

    @pl.when(t == 0)
    def _():
        s_scr[...] = jnp.zeros_like(s_scr)
        nc = n_ctx_chunks

        def ctx_head(hp, carry):
            prepare(hp, nc, 0, None, (ck_ref, ck_ref), (cg_ref, cg_ref), False)
            return carry

        lax.fori_loop(0, groups, ctx_head, 0)
        _gdn_recur(nc, chains(nc, 0, cv_ref, cv_ref, None, None), False, *scr)

        def first_head(hp, carry):
            prepare(hp, n, 0, *main_refs, True)
            return carry

        lax.fori_loop(0, groups, first_head, 0)

    @pl.when((t > 0) & (t < n_blocks))
    def _():
        base_p = (t & 1) * set_slots
        base_r = set_slots - base_p
        per = n // groups
        for g in range(groups):
            _gdn_recur(range(g * per, (g + 1) * per), chains(n, base_r, vf_ref, vb_ref, of_ref, ob_ref), True, *scr)
            prepare(g, n, base_p, *main_refs, True)

    @pl.when(t == n_blocks)
    def _():
        base_r = ((n_blocks - 1) & 1) * set_slots
        _gdn_recur(n, chains(n, base_r, vf_ref, vb_ref, of_ref, ob_ref), True, *scr)


GDN_HEADS = 8


def _gdn_scan(qkv, gates, qkv_ctx, gates_ctx, tb):
    bsz, _, length, _ = qkv.shape
    lc = qkv_ctx.shape[2] // bsz
    nt = length // tb
    nc, ncc = tb // CHUNK, lc // CHUNK
    hb = GDN_HEADS
    assert nc % GDN_QUAD == 0 and ncc % GDN_QUAD == 0 and hb % GDN_PREP_HEADS == 0
    assert nc % (hb // GDN_PREP_HEADS) == 0 and ncc <= 2 * nc
    slots = 2 * 2 * hb * nc
    blk = (1, hb, tb, HEAD_DIM)
    cblk = (1, hb, lc, HEAD_DIM)
    gblk = (1, 3, 2 * N_HEADS, tb)
    prep = lambda t: jnp.minimum(t, nt - 1)
    scan = lambda t: jnp.maximum(t - 1, 0)
    fwd = lambda off, tt: pl.BlockSpec(blk, lambda b, h, t: (b, off // hb + h, tt(t), 0))
    bwd = lambda off, tt: pl.BlockSpec(blk, lambda b, h, t: (b, off // hb + h, nt - 1 - tt(t), 0))
    ctx = lambda off: pl.BlockSpec(cblk, lambda b, h, t: (0, off // hb + h, b, 0))
    out_shape = jax.ShapeDtypeStruct((bsz, N_HEADS, length, HEAD_DIM), BF16)
    return pl.pallas_call(
        functools.partial(_gdn_body, n_chunks=nc, n_ctx_chunks=ncc, heads=hb, n_blocks=nt),
        grid=(bsz, N_HEADS // hb, nt + 1),
        in_specs=[fwd(0, prep), fwd(N_HEADS, prep), fwd(2 * N_HEADS, scan),
                  pl.BlockSpec(gblk, lambda b, h, t: (b, 0, 0, prep(t))),
                  bwd(0, prep), bwd(N_HEADS, prep), bwd(2 * N_HEADS, scan),
                  pl.BlockSpec(gblk, lambda b, h, t: (b, 0, 0, nt - 1 - prep(t))),
                  ctx(N_HEADS), ctx(2 * N_HEADS),
                  pl.BlockSpec((1, 3, 2 * N_HEADS, lc), lambda b, h, t: (0, 0, 0, b))],
        out_specs=[pl.BlockSpec(blk, lambda b, h, t: (b, h, scan(t), 0)),
                   pl.BlockSpec(blk, lambda b, h, t: (b, h, nt - 1 - scan(t), 0))],
        out_shape=[out_shape, out_shape],
        scratch_shapes=[pltpu.VMEM((2 * hb, HEAD_DIM, HEAD_DIM), F32),
                        pltpu.VMEM((slots, 2 * CHUNK, HEAD_DIM), BF16),
                        pltpu.VMEM((slots, CHUNK, CHUNK), BF16),
                        pltpu.VMEM((slots, HEAD_DIM + CHUNK, CHUNK), BF16),
                        pltpu.VMEM((slots, 1, HEAD_DIM), F32)],
        compiler_params=pltpu.CompilerParams(
            dimension_semantics=("parallel", "parallel", "arbitrary"),
            vmem_limit_bytes=VMEM_LIMIT),
    )(qkv, qkv, qkv, gates, qkv, qkv, qkv, gates, qkv_ctx, qkv_ctx, gates_ctx)


def _out_body(oaf_ref, oab_ref, obf_ref, obb_ref, za_ref, zb_ref, naw_ref, nbw_ref,
              w_ref, x_ref, gate_ref, fw_ref, o_ref):
    parts = []
    for of_ref, ob_ref, z_ref, nw_ref in ((oaf_ref, oab_ref, za_ref, naw_ref),
                                          (obf_ref, obb_ref, zb_ref, nbw_ref)):
        for h in range(N_HEADS):
            o = of_ref[0, h].astype(F32) + ob_ref[0, h].astype(F32)
            o = o * lax.rsqrt(jnp.mean(o * o, axis=-1, keepdims=True) + NORM_EPS) * nw_ref[h:h + 1, :]
            hz = 0.5 * z_ref[0, h].astype(F32)
            parts.append(((hz + hz * jnp.tanh(hz)) * o).astype(BF16))
    y = jnp.concatenate(parts, axis=1)
    xo = x_ref[0] + gate_ref[0] * _dot(y, w_ref[...])
    ms = jnp.mean(xo * xo, axis=-1, keepdims=True)
    o_ref[0] = xo * lax.rsqrt(ms + NORM_EPS) * fw_ref[...]


def _out_stage(oa_f, oa_b, ob_f, ob_b, yh, na_w, nb_w, w_out, x, gate, final_w, tm):
    bsz, length, d = x.shape
    hblk = (1, N_HEADS, tm, HEAD_DIM)
    ospec = pl.BlockSpec(hblk, lambda b, m: (b, 0, m, 0))
    full2 = lambda a: pl.BlockSpec(a.shape, lambda b, m: (0, 0))
    return pl.pallas_call(
        _out_body,
        grid=(bsz, length // tm),
        in_specs=[ospec, ospec, ospec, ospec,
                  pl.BlockSpec(hblk, lambda b, m: (b, OFF_AZ // N_HEADS, m, 0)),
                  pl.BlockSpec(hblk, lambda b, m: (b, OFF_BZ // N_HEADS, m, 0)),
                  full2(na_w), full2(nb_w), full2(w_out),
                  pl.BlockSpec((1, tm, d), lambda b, m: (b, m, 0)),
                  pl.BlockSpec((1, 1, d), lambda b, m: (b, 0, 0)),
                  full2(final_w)],
        out_specs=pl.BlockSpec((1, tm, d), lambda b, m: (b, m, 0)),
        out_shape=jax.ShapeDtypeStruct((bsz, length, d), F32),
        compiler_params=pltpu.CompilerParams(dimension_semantics=("parallel", "parallel"),
                                             vmem_limit_bytes=VMEM_LIMIT),
    )(oa_f, oa_b, ob_f, ob_b, yh, yh, na_w, nb_w, w_out, x, gate, final_w)


def kernel(x, c, ctx, c_ctx, norm_w, ada_w, ada_b, w_in, conv_w, hg_lb_logits, gdn_a_log,
           gdn_dt_bias, ha_norm_w, hb_norm_w, w_out, final_norm_w):
    bsz, length, d = x.shape
    lc = ctx.shape[1]
    assert d == D_MODEL and length % 512 == 0 and length % GRID_W == 0 and lc % CHUNK == 0
    assert w_in.shape[0] == 1, "single-layer block"

    n_cond = bsz + 1
    cond = jnp.concatenate([c, c_ctx[None, :], jnp.zeros((-n_cond % 8, d), F32)], axis=0)
    mod = _adaln(cond, ada_w[0], ada_b[0])
    shift, scale, gate = mod[:, :d], mod[:, d:2 * d], mod[:, 2 * d:]
    lat = lambda m: m[:bsz, None, :]
    of_ctx = lambda m: m[bsz:bsz + 1, None, :]

    w_main = w_in[0].astype(BF16)
    w_gate_t = w_main[:, N_MAIN:].T
    nw = norm_w[0].reshape(1, d)
    yh, gates = _inproj(x, nw, lat(scale), lat(shift), w_main, w_gate_t, tm=min(IN_TM, length))
    yh_c, gates_c = _inproj(ctx.reshape(1, bsz * lc, d), nw, of_ctx(scale), of_ctx(shift), w_main, w_gate_t,
                            tm=bsz * lc)

    lb = jax.nn.softmax(hg_lb_logits.astype(F32), axis=0)[0]
    lb = lb.reshape(2, N_HEADS, HEAD_DIM).transpose(1, 0, 2)
    oa_f, oa_b = _hgrn2_scan(yh, yh_c, lb, min(HG_TB, length))

    cw = conv_w[0].reshape(9, 3 * N_HEADS, HEAD_DIM).transpose(1, 0, 2)
    qkv = _gdn_conv(yh, cw, two_d=True)
    qkv_c = _gdn_conv(yh_c, cw, two_d=False, width=lc)
    params = jnp.stack([gdn_a_log[0].reshape(-1), gdn_dt_bias[0].reshape(-1)], axis=1).astype(F32)
    ob_f, ob_b = _gdn_scan(qkv, _gate_rows(gates, params), qkv_c, _gate_rows(gates_c, params),
                           min(GDN_TB, length))

    return _out_stage(oa_f, oa_b, ob_f, ob_b, yh, ha_norm_w[0], hb_norm_w[0],
                      w_out[0].astype(BF16), x, lat(gate), final_norm_w.reshape(1, d),
                      tm=min(OUT_TM, length))
```

```python
import functools

import jax
import jax.numpy as jnp
import numpy as np
from jax import lax
from jax.experimental import pallas as pl
from jax.experimental.pallas import tpu as pltpu

F32 = jnp.float32
BF16 = jnp.bfloat16

D_MODEL = 1024
N_HEADS = 8
HEAD_DIM = 128
CHUNK = 64
GRID_W = 64
NORM_EPS = 1e-6
N_MAIN = 72 * HEAD_DIM
N_GATE = 4 * N_HEADS
OFF_AQ, OFF_AFF, OFF_AFB, OFF_AI, OFF_AZ, OFF_BQ, OFF_BZ = 0, 8, 16, 24, 32, 40, 64
NEG_BIG = -1e30
F32_TINY = float(np.finfo(np.float32).tiny)
VMEM_LIMIT = 56 * 1024 * 1024
IN_TM, IN_TN = 2048, 1024
HG_TB, GDN_TB = 4096, 512
CONV_RT = 512
OUT_TM = 512
GATE_TG = 2048


def _dot(a, b):
    return jnp.dot(a, b, preferred_element_type=F32)


def _dot_nt(a, b):
    return lax.dot_general(a, b, (((1,), (1,)), ((), ())), preferred_element_type=F32)


def _sigmoid(x):
    return 1.0 / (1.0 + jnp.exp(-x))


def _silu(x):
    return x * _sigmoid(x)


def _adaln_body(c_ref, w_ref, b_ref, o_ref):
    o_ref[...] = _dot(_silu(c_ref[...]), w_ref[...]) + b_ref[...]


def _adaln(cond, ada_w, ada_b):
    rows, d = cond.shape
    n = ada_w.shape[1]
    tn = 1024
    return pl.pallas_call(
        _adaln_body,
        grid=(n // tn,),
        in_specs=[pl.BlockSpec((rows, d), lambda j: (0, 0)),
                  pl.BlockSpec((d, tn), lambda j: (0, j)),
                  pl.BlockSpec((1, tn), lambda j: (0, j))],
        out_specs=pl.BlockSpec((rows, tn), lambda j: (0, j)),
        out_shape=jax.ShapeDtypeStruct((rows, n), F32),
        compiler_params=pltpu.CompilerParams(dimension_semantics=("arbitrary",),
                                             vmem_limit_bytes=VMEM_LIMIT),
    )(cond, ada_w, ada_b.reshape(1, n))


def _inproj_body(x_ref, nw_ref, sc_ref, sh_ref, w_ref, wg_ref, y_ref, yg_ref, h_scr, *, tn):
    @pl.when(pl.program_id(2) == 0)
    def _():
        x = x_ref[0]
        ms = jnp.mean(x * x, axis=-1, keepdims=True)
        h = x * lax.rsqrt(ms + NORM_EPS) * nw_ref[...]
        h = (h * (1.0 + sc_ref[0]) + sh_ref[0]).astype(BF16)
        h_scr[...] = h
        yg_ref[0] = _dot_nt(wg_ref[...], h)

    acc = _dot(h_scr[...], w_ref[...])
    for j in range(tn // HEAD_DIM):
        y_ref[0, j] = acc[:, j * HEAD_DIM:(j + 1) * HEAD_DIM].astype(y_ref.dtype)


def _inproj(x, norm_w, scale, shift, w_main, w_gate_t, tm):
    bsz, length, d = x.shape
    tn = IN_TN
    grid = (bsz, length // tm, N_MAIN // tn)
    return pl.pallas_call(
        functools.partial(_inproj_body, tn=tn),
        grid=grid,
        in_specs=[pl.BlockSpec((1, tm, d), lambda b, m, n: (b, m, 0)),
                  pl.BlockSpec((1, d), lambda b, m, n: (0, 0)),
                  pl.BlockSpec((1, 1, d), lambda b, m, n: (b, 0, 0)),
                  pl.BlockSpec((1, 1, d), lambda b, m, n: (b, 0, 0)),
                  pl.BlockSpec((d, tn), lambda b, m, n: (0, n)),
                  pl.BlockSpec((N_GATE, d), lambda b, m, n: (0, 0))],
        out_specs=[pl.BlockSpec((1, tn // HEAD_DIM, tm, HEAD_DIM), lambda b, m, n: (b, n, m, 0)),
                   pl.BlockSpec((1, N_GATE, tm), lambda b, m, n: (b, 0, m))],
        out_shape=[jax.ShapeDtypeStruct((bsz, N_MAIN // HEAD_DIM, length, HEAD_DIM), BF16),
                   jax.ShapeDtypeStruct((bsz, N_GATE, length), F32)],
        scratch_shapes=[pltpu.VMEM((tm, d), BF16)],
        compiler_params=pltpu.CompilerParams(
            dimension_semantics=("parallel", "parallel", "arbitrary"),
            vmem_limit_bytes=VMEM_LIMIT),
    )(x, norm_w, scale, shift, w_main, w_gate_t)


CONV_PAD = 72
CONV_HALO = 8


def _conv_body(x_ref, w_ref, o_ref, pad_scr, *, length, width, two_d, rt):
    blk = pl.program_id(1)
    zeros = jnp.zeros((CONV_PAD, HEAD_DIM), F32)
    pad_scr[0:CONV_PAD, :] = zeros
    pad_scr[CONV_PAD + length:CONV_PAD + length + CONV_PAD, :] = zeros
    pad_scr[CONV_PAD:CONV_PAD + length, :] = x_ref[0, 0].astype(F32)
    w = 0.5 * w_ref[0]
    win_rows = rt + 2 * CONV_HALO
    is_q = blk < N_HEADS
    is_qk = blk < 2 * N_HEADS
    col = lax.broadcasted_iota(jnp.int32, (rt, HEAD_DIM), 0) & (width - 1)

    def tile(i, carry):
        s = pl.multiple_of(i * rt, rt)
        sums = [None, None, None]
        for dr in ((-1, 0, 1) if two_d else (0,)):
            base = pl.multiple_of(s + (CONV_PAD + dr * width - CONV_HALO), 8)
            win = pad_scr[pl.ds(base, win_rows), :]
            for j in range(3):
                tap = (dr + 1) * 3 + j
                term = win * w[tap:tap + 1, :]
                sums[j] = term if sums[j] is None else sums[j] + term
        inner = slice(CONV_HALO, CONV_HALO + rt)
        left = pltpu.roll(sums[0], 1, 0)[inner]
        right = pltpu.roll(sums[2], win_rows - 1, 0)[inner]
        hs = sums[1][inner] + jnp.where(col >= 1, left, 0.0) + jnp.where(col <= width - 2, right, 0.0)
        a = hs + hs * jnp.tanh(hs)
        nrm = lax.rsqrt(jnp.sum(a * a, axis=-1, keepdims=True) + NORM_EPS)
        f = jnp.where(is_q, nrm * HEAD_DIM ** -0.5, jnp.where(is_qk, nrm, 1.0))
        o_ref[0, 0, pl.ds(s, rt), :] = (a * f).astype(o_ref.dtype)
        return carry

    lax.fori_loop(0, length // rt, tile, 0)


def _gdn_conv(yh, conv_w, two_d, width=GRID_W):
    bsz, _, length, _ = yh.shape
    assert width & (width - 1) == 0, "grid width must be a power of two (column index by bit mask)"
    rt = min(CONV_RT, length)
    assert rt % width == 0
    nblk = 3 * N_HEADS
    return pl.pallas_call(
        functools.partial(_conv_body, length=length, width=width, two_d=two_d, rt=rt),
        grid=(bsz, nblk),
        in_specs=[pl.BlockSpec((1, 1, length, HEAD_DIM), lambda b, j: (b, OFF_BQ + j, 0, 0)),
                  pl.BlockSpec((1, 9, HEAD_DIM), lambda b, j: (j, 0, 0))],
        out_specs=pl.BlockSpec((1, 1, length, HEAD_DIM), lambda b, j: (b, j, 0, 0)),
        out_shape=jax.ShapeDtypeStruct((bsz, nblk, length, HEAD_DIM), BF16),
        scratch_shapes=[pltpu.VMEM((length + 2 * CONV_PAD, HEAD_DIM), F32)],
        compiler_params=pltpu.CompilerParams(dimension_semantics=("parallel", "parallel"),
                                             vmem_limit_bytes=VMEM_LIMIT),
    )(yh, conv_w)


HG_LEVELS = (32, 16, 8, 4, 2, 1)
HG_PAIRS = ((0, 1), (2, 3), (4, 5))
HG_GROUP = 4


def _hg_constants():
    t = np.arange(CHUNK)[:, None]
    col = np.arange(HEAD_DIM)[None, :]
    s = col % CHUNK
    out = [np.broadcast_to(np.where((t & m) != 0, 1.0, -1.0), (CHUNK, HEAD_DIM)) for m in HG_LEVELS]
    for rev in (False, True):
        for pa, pb in HG_PAIRS:
            keep = np.zeros((CHUNK, HEAD_DIM), bool)
            for half, lv in ((col < CHUNK, pa), (col >= CHUNK, pb)):
                m = HG_LEVELS[lv]
                split = ((t ^ s) >> (m.bit_length() - 1)) == 1
                t_hi = (t & m) != 0
                keep |= half & split & (~t_hi if rev else t_hi)
            out.append(keep)
    out.append((col == t) & (col < CHUNK))
    return jnp.asarray(np.stack([np.asarray(o, np.float32) for o in out]))


def _hg_triangles():
    t = np.arange(CHUNK)
    lower = (t[None, :] <= t[:, None]).astype(np.float32)
    return jnp.asarray(np.stack([np.tile(lower, (1, 3)), np.tile(lower.T, (1, 3))]), BF16)


def _cumsum_rows(g, tri3):
    hi = g.astype(BF16)
    r1 = g - hi.astype(F32)
    mid = r1.astype(BF16)
    lo = (r1 - mid.astype(F32)).astype(BF16)
    return _dot(tri3, jnp.concatenate([hi, mid, lo], axis=0))


def _level_operand(b, b_ref, q, k, m, rev):
    parts = []
    for blk in range(CHUNK // m):
        rows = slice(blk * m, (blk + 1) * m)
        r = (blk // 2) * 2 * m + m
        ref = jnp.broadcast_to(b_ref[r:r + 1, :], (m, HEAD_DIM))
        q_side = (blk % 2 == 1) != rev
        e = (b[rows] - ref) if q_side else (ref - b[rows])
        parts.append(((q if q_side else k)[rows] * jnp.exp2(e)).astype(BF16))
    return jnp.concatenate(parts, axis=0)


def _level_ref(b_ref, m, sub):
    bc = lambda r, n: jnp.broadcast_to(b_ref[r:r + 1, :], (n, HEAD_DIM))
    if m >= 4:
        return jnp.concatenate([bc(blk * 2 * m + m, 2 * m) for blk in range(CHUNK // (2 * m))], axis=0)
    lo = jnp.concatenate([bc(v * 8 + 2, 8) for v in range(CHUNK // 8)], axis=0)
    hi = jnp.concatenate([bc(v * 8 + 6, 8) for v in range(CHUNK // 8)], axis=0)
    return jnp.where(sub < 4, lo, hi)


def _hg_prepare(items, readout, cst_ref, tri_ref, b_scr, sc_scr, vv_scr, u_scr, dec_scr):
    row = lax.broadcasted_iota(jnp.int32, (CHUNK, HEAD_DIM), 0)
    sub = row & 7
    even = (row & 1) == 0
    gs, ks, bs = [], [], []
    for i, (_, _, load_f, _, lb, rev) in enumerate(items):
        gate = (1.0 - lb) * (0.5 + 0.5 * jnp.tanh(0.5 * load_f()))
        g = jnp.log2(jnp.maximum(lb + gate, F32_TINY))
        gs.append(g)
        ks.append((1.0 - lb) - gate)
        b = _cumsum_rows(g, tri_ref[1 if rev else 0])
        bs.append(b)
        b_scr[i] = b
    for i, (slot, _, _, load_v, _, rev) in enumerate(items):
        b = bs[i]
        btot = b[0:1, :] if rev else b[CHUNK - 1:CHUNK, :]
        v = load_v()
        vv_t = jnp.concatenate([v, v], axis=0).T.astype(BF16)
        k_dec = (ks[i] * jnp.exp2(btot - b)).astype(BF16)
        u_scr[slot] = _dot(vv_t[:, :CHUNK], k_dec)
        dec_scr[slot] = jnp.exp2(btot)
        vv_scr[slot] = vv_t
    if not readout:
        return

    qs = []
    for _, load_q, _, _, _, _ in items:
        q_raw = load_q()
        hs = (0.5 * HEAD_DIM ** -0.5) * q_raw
        qs.append(hs + hs * jnp.tanh(0.5 * q_raw))
    scores = [jnp.zeros((CHUNK, HEAD_DIM), F32) for _ in items]
    zero_blk = jnp.zeros((CHUNK, HEAD_DIM), BF16)
    for pi, pair in enumerate(HG_PAIRS):
        gps = []
        for i, (_, _, _, _, _, rev) in enumerate(items):
            a_mats = []
            for lv in pair:
                m = HG_LEVELS[lv]
                if m >= 8:
                    a_mats.append(_level_operand(bs[i], b_scr.at[i], qs[i], ks[i], m, rev))
                    continue
                sign = cst_ref[lv]
                if m == 1:
                    x = jnp.exp2(jnp.where(even, gs[i] if rev else pltpu.roll(gs[i], CHUNK - 1, 0), 0.0))
                else:
                    d = bs[i] - _level_ref(b_scr.at[i], m, sub)
                    x = jnp.exp2((-d if rev else d) * sign)
                t_is_q = (sign < 0.0) if rev else (sign > 0.0)
                a_mats.append((jnp.where(t_is_q, qs[i], ks[i]) * x).astype(BF16))
            lhs = jnp.concatenate(a_mats, axis=1)
            rhs = jnp.concatenate([jnp.concatenate([a_mats[0], zero_blk], axis=1),
                                   jnp.concatenate([zero_blk, a_mats[1]], axis=1)], axis=0)
            gps.append(_dot_nt(lhs, rhs))
        for i, (_, _, _, _, _, rev) in enumerate(items):
            scores[i] = scores[i] + gps[i] * cst_ref[6 + (3 if rev else 0) + pi]
    for i, (slot, _, _, _, _, _) in enumerate(items):
        diag = jnp.sum(qs[i] * ks[i], axis=-1, keepdims=True) * cst_ref[12]
        sc_scr[slot, :, :HEAD_DIM] = (scores[i] + diag).astype(BF16)
        sc_scr[slot, :, HEAD_DIM:] = (qs[i] * jnp.exp2(bs[i])).astype(BF16)


def _hg_recur(steps, dirs, readout, sc_scr, vv_scr, u_scr, dec_scr):
    states = [s_scr[...] for s_scr, _, _ in dirs]
    for c in steps:
        for i, (_, slot_of, store_o) in enumerate(dirs):
            slot = slot_of(c)
            if readout:
                rhs = jnp.concatenate([vv_scr[slot], states[i].astype(BF16)], axis=1)
                store_o(c, _dot_nt(sc_scr[slot], rhs))
            states[i] = states[i] * dec_scr[slot] + u_scr[slot]
    for (s_scr, _, _), st in zip(dirs, states):
        s_scr[...] = st


def _hg_body(cst_ref, tri_ref, qf_ref, ff_ref, vf_ref, qb_ref, fb_ref, vb_ref,
             cff_ref, cfb_ref, cv_ref, lb_ref, of_ref, ob_ref,
             sf_scr, sb_scr, b_scr, sc_scr, vv_scr, u_scr, dec_scr, *, n_chunks, n_ctx_chunks):
    lb_f = lb_ref[0, 0:1, :]
    lb_b = lb_ref[0, 1:2, :]
    scr = (sc_scr, vv_scr, u_scr, dec_scr)
    load = lambda ref, rows: (lambda: ref[0, 0, rows, :].astype(F32))

    @pl.when(pl.program_id(2) == 0)
    def _():
        sf_scr[...] = jnp.zeros_like(sf_scr)
        sb_scr[...] = jnp.zeros_like(sb_scr)
        n = n_ctx_chunks
        for c0 in range(0, n, HG_GROUP):
            items = []
            for c in range(c0, min(c0 + HG_GROUP, n)):
                sl = slice(c * CHUNK, (c + 1) * CHUNK)
                items.append((c, None, load(cff_ref, sl), load(cv_ref, sl), lb_f, False))
                items.append((n + c, None, load(cfb_ref, sl), load(cv_ref, sl), lb_b, True))
            _hg_prepare(items, False, cst_ref, tri_ref, b_scr, *scr)
        _hg_recur(range(n), [(sf_scr, lambda c: c, None), (sb_scr, lambda c: 2 * n - 1 - c, None)],
                  False, *scr)

    n = n_chunks
    rows = lambda c: slice(c * CHUNK, (c + 1) * CHUNK)

    def prepare(g):
        items = []
        for p in range(g * HG_GROUP, (g + 1) * HG_GROUP):
            rf, rb = rows(p), rows(n - 1 - p)
            items.append((p, load(qf_ref, rf), load(ff_ref, rf), load(vf_ref, rf), lb_f, False))
            items.append((2 * n - 1 - p, load(qb_ref, rb), load(fb_ref, rb), load(vb_ref, rb), lb_b, True))
        _hg_prepare(items, True, cst_ref, tri_ref, b_scr, *scr)

    def store_f(c, o):
        of_ref[0, 0, rows(c), :] = o.astype(of_ref.dtype)

    def store_b(c, o):
        ob_ref[0, 0, rows(n - 1 - c), :] = o.astype(ob_ref.dtype)

    def recur(g):
        _hg_recur(range(g * HG_GROUP, (g + 1) * HG_GROUP),
                  [(sf_scr, lambda c: c, store_f), (sb_scr, lambda c: 2 * n - 1 - c, store_b)], True, *scr)

    groups = n // HG_GROUP
    prepare(0)
    for g in range(1, groups):
        recur(g - 1)
        prepare(g)
    recur(groups - 1)


def _hgrn2_scan(yh, yh_ctx, lb, tb):
    bsz, _, length, _ = yh.shape
    lc = yh_ctx.shape[2] // bsz
    nt = length // tb
    nc, ncc = tb // CHUNK, lc // CHUNK
    assert nc % HG_GROUP == 0
    slots = 2 * max(nc, ncc)
    blk = (1, 1, tb, HEAD_DIM)
    cblk = (1, 1, lc, HEAD_DIM)
    fwd = lambda off: pl.BlockSpec(blk, lambda b, h, t: (b, off + h, t, 0))
    bwd = lambda off: pl.BlockSpec(blk, lambda b, h, t: (b, off + h, nt - 1 - t, 0))
    ctx = lambda off: pl.BlockSpec(cblk, lambda b, h, t: (0, off + h, b, 0))
    consts = _hg_constants()
    tri3 = _hg_triangles()
    out_shape = jax.ShapeDtypeStruct((bsz, N_HEADS, length, HEAD_DIM), BF16)
    return pl.pallas_call(
        functools.partial(_hg_body, n_chunks=nc, n_ctx_chunks=ncc),
        grid=(bsz, N_HEADS, nt),
        in_specs=[pl.BlockSpec(consts.shape, lambda b, h, t: (0, 0, 0)),
                  pl.BlockSpec(tri3.shape, lambda b, h, t: (0, 0, 0)),
                  fwd(OFF_AQ), fwd(OFF_AFF), fwd(OFF_AI), bwd(OFF_AQ), bwd(OFF_AFB), bwd(OFF_AI),
                  ctx(OFF_AFF), ctx(OFF_AFB), ctx(OFF_AI),
                  pl.BlockSpec((1, 2, HEAD_DIM), lambda b, h, t: (h, 0, 0))],
        out_specs=[pl.BlockSpec(blk, lambda b, h, t: (b, h, t, 0)),
                   pl.BlockSpec(blk, lambda b, h, t: (b, h, nt - 1 - t, 0))],
        out_shape=[out_shape, out_shape],
        scratch_shapes=[pltpu.VMEM((HEAD_DIM, HEAD_DIM), F32), pltpu.VMEM((HEAD_DIM, HEAD_DIM), F32),
                        pltpu.VMEM((2 * HG_GROUP, CHUNK, HEAD_DIM), F32),
                        pltpu.VMEM((slots, CHUNK, 2 * HEAD_DIM), BF16),
                        pltpu.VMEM((slots, HEAD_DIM, HEAD_DIM), BF16),
                        pltpu.VMEM((slots, HEAD_DIM, HEAD_DIM), F32),
                        pltpu.VMEM((slots, 1, HEAD_DIM), F32)],
        compiler_params=pltpu.CompilerParams(
            dimension_semantics=("parallel", "parallel", "arbitrary"),
            vmem_limit_bytes=VMEM_LIMIT),
    )(consts, tri3, yh, yh, yh, yh, yh, yh, yh_ctx, yh_ctx, yh_ctx, lb)


def _seg_cumsum_lanes(x, rev, lane):
    total = x.shape[1]
    seg = lane & (CHUNK - 1)
    for sh in (1, 2, 4, 8, 16, 32):
        if rev:
            x = x + jnp.where(seg < CHUNK - sh, pltpu.roll(x, total - sh, 1), 0.0)
        else:
            x = x + jnp.where(seg >= sh, pltpu.roll(x, sh, 1), 0.0)
    return x


def _gate_rows_body(g_ref, p_ref, o_ref):
    nrow = 2 * N_HEADS
    total = g_ref.shape[2]
    lane = lax.broadcasted_iota(jnp.int32, (nrow, total), 1)
    row = lax.broadcasted_iota(jnp.int32, (nrow, total), 0)
    z = g_ref[0, :nrow, :] + p_ref[:, 1:2]
    softplus = jnp.maximum(z, 0.0) + jnp.log1p(jnp.exp(-jnp.abs(z)))
    g = -jnp.exp(p_ref[:, 0:1]) * softplus
    prefix = _seg_cumsum_lanes(g, False, lane)
    suffix = _seg_cumsum_lanes(g, True, lane)
    o_ref[0, 0] = jnp.where(row < N_HEADS, prefix, suffix)
    o_ref[0, 1] = _sigmoid(g_ref[0, nrow:, :])
    o_ref[0, 2] = prefix + suffix - g


def _gate_rows(gates, params):
    bsz, _, length = gates.shape
    tg = min(GATE_TG, length)
    nrow = 2 * N_HEADS
    return pl.pallas_call(
        _gate_rows_body,
        grid=(bsz, length // tg),
        in_specs=[pl.BlockSpec((1, N_GATE, tg), lambda b, t: (b, 0, t)),
                  pl.BlockSpec(params.shape, lambda b, t: (0, 0))],
        out_specs=pl.BlockSpec((1, 3, nrow, tg), lambda b, t: (b, 0, 0, t)),
        out_shape=jax.ShapeDtypeStruct((bsz, 3, nrow, length), F32),
        compiler_params=pltpu.CompilerParams(dimension_semantics=("parallel", "parallel"),
                                             vmem_limit_bytes=VMEM_LIMIT),
    )(gates, params)


def _gdn_gates(g_ref, d, head):
    r = pl.ds(d * N_HEADS + head, 1)
    b = g_ref[0, 0, r, :]
    rows = jnp.concatenate([b, g_ref[0, 1, r, :], g_ref[0, 2, r, :]], axis=0)
    return rows, jnp.broadcast_to(b, (HEAD_DIM, b.shape[1])).T


GDN_QUAD = 4
GDN_PREP_HEADS = 4


def _gdn_prepare(quads, readout, qkg_scr, tb_scr, kp_scr, dec_scr):
    wide = GDN_QUAD * CHUNK
    t_i = lax.broadcasted_iota(jnp.int32, (CHUNK, wide), 0)
    lane = lax.broadcasted_iota(jnp.int32, (CHUNK, wide), 1)
    s_i = lane & (CHUNK - 1)
    blk = [(lane >= j * CHUNK) & (lane < (j + 1) * CHUNK) for j in range(GDN_QUAD)]
    eye = (t_i == s_i).astype(F32)
    tri = {False: (s_i <= t_i, s_i < t_i), True: (s_i >= t_i, s_i > t_i)}
    rows_of = lambda j: slice(j * CHUNK, (j + 1) * CHUNK)

    def diag_blocks(m, off):
        out = jnp.where(blk[0], m[off:off + CHUNK], 0.0)
        for j in range(1, GDN_QUAD):
            out = jnp.where(blk[j], m[off + j * CHUNK:off + (j + 1) * CHUNK], out)
        return out

    def block_diag(y):
        return jnp.concatenate([jnp.where(blk[j], y, 0.0).astype(BF16) for j in range(GDN_QUAD)], axis=0)

    dmats, qks, xs, ys = [], [], [], []
    for _, load_q, load_k, rows, bcol, rev in quads:
        incl, strict = tri[rev]
        bc = jnp.concatenate([bcol, bcol], axis=1)
        dmat = jnp.exp(jnp.where(incl, diag_blocks(bc, 0) - rows[0:1, :], NEG_BIG))
        kb = load_k()
        qb = load_q() if readout else None
        pair = 2 * CHUNK
        lo_half = lax.broadcasted_iota(jnp.int32, (CHUNK, pair), 1) < CHUNK
        pair_diag = lambda m: jnp.where(lo_half, m[:CHUNK], m[CHUNK:])
        kk_parts, qk_parts = [], []
        for p in range(GDN_QUAD // 2):
            kp = kb[p * pair:(p + 1) * pair]
            if readout:
                qkk = _dot_nt(jnp.concatenate([qb[p * pair:(p + 1) * pair], kp], axis=0), kp)
                qk_parts.append(pair_diag(qkk[:pair]))
                kk_parts.append(pair_diag(qkk[pair:]))
            else:
                kk_parts.append(pair_diag(_dot_nt(kp, kp)))
        kk = jnp.concatenate(kk_parts, axis=1)
        if readout:
            qks.append(jnp.concatenate(qk_parts, axis=1))
        n_mat = jnp.where(strict, kk * dmat, 0.0) * rows[1:2, :]
        dmats.append(dmat)
        xs.append(eye)
        ys.append(n_mat)
    for m in (1, 2, 4, 8, 16, 32):
        opposite = ((t_i ^ s_i) >> (m.bit_length() - 1)) == 1
        later = (t_i & m) != 0
        lvl = {False: opposite & later, True: opposite & jnp.logical_not(later)}
        n_lvl = [jnp.where(lvl[q[5]], y, 0.0) for q, y in zip(quads, ys)]
        if m == 1:
            xs = [x - nl for x, nl in zip(xs, n_lvl)]
            continue
        inner = [_dot(nl.astype(BF16), block_diag(x)) for nl, x in zip(n_lvl, xs)]
        xs = [x - _dot(x.astype(BF16), block_diag(a)) for x, a in zip(xs, inner)]

    for i, (slot0, load_q, load_k, rows, bcol, rev) in enumerate(quads):
        k_all = load_k().astype(F32)
        e_col = jnp.exp(bcol)
        qkg_all_k = (k_all * e_col).astype(BF16)
        if readout:
            qkg_all_q = (load_q().astype(F32) * e_col).astype(BF16)
            pm_all = qks[i] * dmats[i] * rows[1:2, :]
        k_t = k_all.T
        kdt_all = (k_t * (jnp.exp(rows[2:3, :] - rows[0:1, :]) * rows[1:2, :])).astype(BF16)
        for j in range(GDN_QUAD):
            slot = slot0 + j
            shift = (wide - j * CHUNK) % wide
            take = lambda m: (m if shift == 0 else pltpu.roll(m, shift, 1))[:, :CHUNK]
            tb_scr[slot] = take(xs[i]).astype(BF16)
            qkg_scr[slot, CHUNK:, :] = qkg_all_k[rows_of(j)]
            if readout:
                qkg_scr[slot, :CHUNK, :] = qkg_all_q[rows_of(j)]
                kp_scr[slot, HEAD_DIM:, :] = take(pm_all).astype(BF16)
            kp_scr[slot, :HEAD_DIM, :] = kdt_all[:, j * CHUNK:(j + 1) * CHUNK]
            tot = rows[2:3, j * CHUNK:(j + 1) * CHUNK]
            dec_scr[slot] = jnp.exp(jnp.concatenate([tot, tot], axis=1))


def _gdn_recur(steps, dirs, readout, qkg_scr, tb_scr, kp_scr, dec_scr):
    def step(c, carry):
        slots = [slot_of(c) for _, slot_of, _, _ in dirs]
        states = [s_ref[...] for s_ref, _, _, _ in dirs]
        if readout:
            qks = [_dot(qkg_scr[slot], s.astype(BF16)) for slot, s in zip(slots, states)]
            kss = [p[CHUNK:] for p in qks]
        else:
            kss = [_dot(qkg_scr[slot, CHUNK:, :], s.astype(BF16)) for slot, s in zip(slots, states)]
        ws = [_dot(tb_scr[slot], (load_v(c) - ks).astype(BF16)).astype(BF16)
              for slot, ks, (_, _, load_v, _) in zip(slots, kss, dirs)]
        if readout:
            res = [_dot(kp_scr[slot], w) for slot, w in zip(slots, ws)]
            for slot, s, r, p, (s_ref, _, _, store_o) in zip(slots, states, res, qks, dirs):
                s_ref[...] = s * dec_scr[slot] + r[:HEAD_DIM]
                store_o(c, p[:CHUNK] + r[HEAD_DIM:])
        else:
            for slot, s, w, (s_ref, _, _, _) in zip(slots, states, ws, dirs):
                s_ref[...] = s * dec_scr[slot] + _dot(kp_scr[slot, :HEAD_DIM, :], w)
        return carry

    if isinstance(steps, int):
        lax.fori_loop(0, steps, step, 0, unroll=4)
    else:
        for c in steps:
            step(c, 0)


def _gdn_body(qf_ref, kf_ref, vf_ref, gf_ref, qb_ref, kb_ref, vb_ref, gb_ref,
              ck_ref, cv_ref, cg_ref, of_ref, ob_ref,
              s_scr, qkg_scr, tb_scr, kp_scr, dec_scr,
              *, n_chunks, n_ctx_chunks, heads, n_blocks):
    t = pl.program_id(2)
    h0 = pl.program_id(1) * heads
    scr = (qkg_scr, tb_scr, kp_scr, dec_scr)
    set_slots = 2 * heads * n_chunks

    def rows(c):
        return slice(c * CHUNK, (c + 1) * CHUNK) if isinstance(c, int) else pl.ds(pl.multiple_of(c * CHUNK, CHUNK), CHUNK)

    def chunk_list(hh, n, base, q_refs, k_refs, g_refs):
        head = h0 + hh
        wide = GDN_QUAD * CHUNK
        out = []
        for d in (0, 1):
            gate_rows, bcol = _gdn_gates(g_refs[d], d, head)
            for c0 in range(0, n, GDN_QUAD):
                sl = slice(c0 * CHUNK, c0 * CHUNK + wide)
                load_q = None if q_refs is None else (lambda r=q_refs[d], sl=sl: r[0, hh, sl, :])
                load_k = lambda r=k_refs[d], sl=sl: r[0, hh, sl, :]
                out.append((base + (2 * hh + d) * n + c0, load_q, load_k, gate_rows[:, sl], bcol[sl, :], d == 1))
        return out

    def prepare(hp, n, base, q_refs, k_refs, g_refs, readout):
        quads = []
        for i in range(GDN_PREP_HEADS):
            quads += chunk_list(hp * GDN_PREP_HEADS + i, n, base, q_refs, k_refs, g_refs)
        _gdn_prepare(quads, readout, *scr)

    def chains(n, base, vf, vb, of, ob):
        out = []
        for hh in range(heads):
            out.append((s_scr.at[2 * hh], lambda c, hh=hh: base + 2 * hh * n + c,
                        lambda c, hh=hh: vf[0, hh, rows(c), :].astype(F32),
                        None if of is None else (lambda c, o, hh=hh: of.__setitem__(
                            (0, hh, rows(c), slice(None)), o.astype(of.dtype)))))
            out.append((s_scr.at[2 * hh + 1], lambda c, hh=hh: base + (2 * hh + 1) * n + (n - 1 - c),
                        lambda c, hh=hh: vb[0, hh, rows(n - 1 - c), :].astype(F32),
                        None if ob is None else (lambda c, o, hh=hh: ob.__setitem__(
                            (0, hh, rows(n - 1 - c), slice(None)), o.astype(ob.dtype)))))
        return out

    main_refs = ((qf_ref, qb_ref), (kf_ref, kb_ref), (gf_ref, gb_ref))
    groups = heads // GDN_PREP_HEADS
    n = n_chunks

    @pl.when(t == 0)
    def _():
        s_scr[...] = jnp.zeros_like(s_scr)
        nc = n_ctx_chunks

        def ctx_head(hp, carry):
            prepare(hp, nc, 0, None, (ck_ref, ck_ref), (cg_ref, cg_ref), False)
            return carry

        lax.fori_loop(0, groups, ctx_head, 0)
        _gdn_recur(nc, chains(nc, 0, cv_ref, cv_ref, None, None), False, *scr)

        def first_head(hp, carry):
            prepare(hp, n, 0, *main_refs, True)
            return carry

        lax.fori_loop(0, groups, first_head, 0)

    @pl.when((t > 0) & (t < n_blocks))
    def _():
        base_p = (t & 1) * set_slots
        base_r = set_slots - base_p
        per = n // groups
        for g in range(groups):
            _gdn_recur(range(g * per, (g + 1) * per), chains(n, base_r, vf_ref, vb_ref, of_ref, ob_ref), True, *scr)
            prepare(g, n, base_p, *main_refs, True)

    @pl.when(t == n_blocks)
    def _():
        base_r = ((n_blocks - 1) & 1) * set_slots
        _gdn_recur(n, chains(n, base_r, vf_ref, vb_ref, of_ref, ob_ref), True, *scr)


GDN_HEADS = 8


def _gdn_scan(qkv, gates, qkv_ctx, gates_ctx, tb):
    bsz, _, length, _ = qkv.shape
    lc = qkv_ctx.shape[2] // bsz
    nt = length // tb
    nc, ncc = tb // CHUNK, lc // CHUNK
    hb = GDN_HEADS
    assert nc % GDN_QUAD == 0 and ncc % GDN_QUAD == 0 and hb % GDN_PREP_HEADS == 0
    assert nc % (hb // GDN_PREP_HEADS) == 0 and ncc <= 2 * nc
    slots = 2 * 2 * hb * nc
    blk = (1, hb, tb, HEAD_DIM)
    cblk = (1, hb, lc, HEAD_DIM)
    gblk = (1, 3, 2 * N_HEADS, tb)
    prep = lambda t: jnp.minimum(t, nt - 1)
    scan = lambda t: jnp.maximum(t - 1, 0)
    fwd = lambda off, tt: pl.BlockSpec(blk, lambda b, h, t: (b, off // hb + h, tt(t), 0))
    bwd = lambda off, tt: pl.BlockSpec(blk, lambda b, h, t: (b, off // hb + h, nt - 1 - tt(t), 0))
    ctx = lambda off: pl.BlockSpec(cblk, lambda b, h, t: (0, off // hb + h, b, 0))
    out_shape = jax.ShapeDtypeStruct((bsz, N_HEADS, length, HEAD_DIM), BF16)
    return pl.pallas_call(
        functools.partial(_gdn_body, n_chunks=nc, n_ctx_chunks=ncc, heads=hb, n_blocks=nt),
        grid=(bsz, N_HEADS // hb, nt + 1),
        in_specs=[fwd(0, prep), fwd(N_HEADS, prep), fwd(2 * N_HEADS, scan),
                  pl.BlockSpec(gblk, lambda b, h, t: (b, 0, 0, prep(t))),
                  bwd(0, prep), bwd(N_HEADS, prep), bwd(2 * N_HEADS, scan),
                  pl.BlockSpec(gblk, lambda b, h, t: (b, 0, 0, nt - 1 - prep(t))),
                  ctx(N_HEADS), ctx(2 * N_HEADS),
                  pl.BlockSpec((1, 3, 2 * N_HEADS, lc), lambda b, h, t: (0, 0, 0, b))],
        out_specs=[pl.BlockSpec(blk, lambda b, h, t: (b, h, scan(t), 0)),
                   pl.BlockSpec(blk, lambda b, h, t: (b, h, nt - 1 - scan(t), 0))],
        out_shape=[out_shape, out_shape],
        scratch_shapes=[pltpu.VMEM((2 * hb, HEAD_DIM, HEAD_DIM), F32),
                        pltpu.VMEM((slots, 2 * CHUNK, HEAD_DIM), BF16),
                        pltpu.VMEM((slots, CHUNK, CHUNK), BF16),
                        pltpu.VMEM((slots, HEAD_DIM + CHUNK, CHUNK), BF16),
                        pltpu.VMEM((slots, 1, HEAD_DIM), F32)],
        compiler_params=pltpu.CompilerParams(
            dimension_semantics=("parallel", "parallel", "arbitrary"),
            vmem_limit_bytes=VMEM_LIMIT),
    )(qkv, qkv, qkv, gates, qkv, qkv, qkv, gates, qkv_ctx, qkv_ctx, gates_ctx)


def _out_body(oaf_ref, oab_ref, obf_ref, obb_ref, za_ref, zb_ref, naw_ref, nbw_ref,
              w_ref, x_ref, gate_ref, fw_ref, o_ref):
    parts = []
    for of_ref, ob_ref, z_ref, nw_ref in ((oaf_ref, oab_ref, za_ref, naw_ref),
                                          (obf_ref, obb_ref, zb_ref, nbw_ref)):
        for h in range(N_HEADS):
            o = of_ref[0, h].astype(F32) + ob_ref[0, h].astype(F32)
            o = o * lax.rsqrt(jnp.mean(o * o, axis=-1, keepdims=True) + NORM_EPS) * nw_ref[h:h + 1, :]
            hz = 0.5 * z_ref[0, h].astype(F32)
            parts.append(((hz + hz * jnp.tanh(hz)) * o).astype(BF16))
    y = jnp.concatenate(parts, axis=1)
    xo = x_ref[0] + gate_ref[0] * _dot(y, w_ref[...])
    ms = jnp.mean(xo * xo, axis=-1, keepdims=True)
    o_ref[0] = xo * lax.rsqrt(ms + NORM_EPS) * fw_ref[...]


def _out_stage(oa_f, oa_b, ob_f, ob_b, yh, na_w, nb_w, w_out, x, gate, final_w, tm):
    bsz, length, d = x.shape
    hblk = (1, N_HEADS, tm, HEAD_DIM)
    ospec = pl.BlockSpec(hblk, lambda b, m: (b, 0, m, 0))
    full2 = lambda a: pl.BlockSpec(a.shape, lambda b, m: (0, 0))
    return pl.pallas_call(
        _out_body,
        grid=(bsz, length // tm),
        in_specs=[ospec, ospec, ospec, ospec,
                  pl.BlockSpec(hblk, lambda b, m: (b, OFF_AZ // N_HEADS, m, 0)),
                  pl.BlockSpec(hblk, lambda b, m: (b, OFF_BZ // N_HEADS, m, 0)),
                  full2(na_w), full2(nb_w), full2(w_out),
                  pl.BlockSpec((1, tm, d), lambda b, m: (b, m, 0)),
                  pl.BlockSpec((1, 1, d), lambda b, m: (b, 0, 0)),
                  full2(final_w)],
        out_specs=pl.BlockSpec((1, tm, d), lambda b, m: (b, m, 0)),
        out_shape=jax.ShapeDtypeStruct((bsz, length, d), F32),
        compiler_params=pltpu.CompilerParams(dimension_semantics=("parallel", "parallel"),
                                             vmem_limit_bytes=VMEM_LIMIT),
    )(oa_f, oa_b, ob_f, ob_b, yh, yh, na_w, nb_w, w_out, x, gate, final_w)


def kernel(x, c, ctx, c_ctx, norm_w, ada_w, ada_b, w_in, conv_w, hg_lb_logits, gdn_a_log,
           gdn_dt_bias, ha_norm_w, hb_norm_w, w_out, final_norm_w):
    bsz, length, d = x.shape
    lc = ctx.shape[1]
    assert d == D_MODEL and length % 512 == 0 and length % GRID_W == 0 and lc % CHUNK == 0
    assert w_in.shape[0] == 1, "single-layer block"

    n_cond = bsz + 1
    cond = jnp.concatenate([c, c_ctx[None, :], jnp.zeros((-n_cond % 8, d), F32)], axis=0)
    mod = _adaln(cond, ada_w[0], ada_b[0])
    shift, scale, gate = mod[:, :d], mod[:, d:2 * d], mod[:, 2 * d:]
    lat = lambda m: m[:bsz, None, :]
    of_ctx = lambda m: m[bsz:bsz + 1, None, :]

    w_main = w_in[0].astype(BF16)
    w_gate_t = w_main[:, N_MAIN:].T
    nw = norm_w[0].reshape(1, d)
    yh, gates = _inproj(x, nw, lat(scale), lat(shift), w_main, w_gate_t, tm=min(IN_TM, length))
    yh_c, gates_c = _inproj(ctx.reshape(1, bsz * lc, d), nw, of_ctx(scale), of_ctx(shift), w_main, w_gate_t,
                            tm=bsz * lc)

    lb = jax.nn.softmax(hg_lb_logits.astype(F32), axis=0)[0]
    lb = lb.reshape(2, N_HEADS, HEAD_DIM).transpose(1, 0, 2)
    oa_f, oa_b = _hgrn2_scan(yh, yh_c, lb, min(HG_TB, length))

    cw = conv_w[0].reshape(9, 3 * N_HEADS, HEAD_DIM).transpose(1, 0, 2)
    qkv = _gdn_conv(yh, cw, two_d=True)
    qkv_c = _gdn_conv(yh_c, cw, two_d=False, width=lc)
    params = jnp.stack([gdn_a_log[0].reshape(-1), gdn_dt_bias[0].reshape(-1)], axis=1).astype(F32)
    ob_f, ob_b = _gdn_scan(qkv, _gate_rows(gates, params), qkv_c, _gate_rows(gates_c, params),
                           min(GDN_TB, length))

    return _out_stage(oa_f, oa_b, ob_f, ob_b, yh, ha_norm_w[0], hb_norm_w[0],
                      w_out[0].astype(BF16), x, lat(gate), final_norm_w.reshape(1, d),
                      tm=min(OUT_TM, length))
```

```python
import functools

import jax
import jax.numpy as jnp
import numpy as np
from jax import lax
from jax.experimental import pallas as pl
from jax.experimental.pallas import tpu as pltpu

F32 = jnp.float32
BF16 = jnp.bfloat16

D_MODEL = 1024
N_HEADS = 8
HEAD_DIM = 128
CHUNK = 64
GRID_W = 64
NORM_EPS = 1e-6
N_MAIN = 72 * HEAD_DIM
N_GATE = 4 * N_HEADS
OFF_AQ, OFF_AFF, OFF_AFB, OFF_AI, OFF_AZ, OFF_BQ, OFF_BZ = 0, 8, 16, 24, 32, 40, 64
NEG_BIG = -1e30
F32_TINY = float(np.finfo(np.float32).tiny)
VMEM_LIMIT = 56 * 1024 * 1024
IN_TM, IN_TN = 2048, 1024
HG_TB, GDN_TB = 4096, 512
CONV_RT = 512
OUT_TM = 512
GATE_TG = 2048


def _dot(a, b):
    return jnp.dot(a, b, preferred_element_type=F32)


def _dot_nt(a, b):
    return lax.dot_general(a, b, (((1,), (1,)), ((), ())), preferred_element_type=F32)


def _sigmoid(x):
    return 1.0 / (1.0 + jnp.exp(-x))


def _silu(x):
    return x * _sigmoid(x)


def _adaln_body(c_ref, w_ref, b_ref, o_ref):
    o_ref[...] = _dot(_silu(c_ref[...]), w_ref[...]) + b_ref[...]


def _adaln(cond, ada_w, ada_b):
    rows, d = cond.shape
    n = ada_w.shape[1]
    tn = 1024
    return pl.pallas_call(
        _adaln_body,
        grid=(n // tn,),
        in_specs=[pl.BlockSpec((rows, d), lambda j: (0, 0)),
                  pl.BlockSpec((d, tn), lambda j: (0, j)),
                  pl.BlockSpec((1, tn), lambda j: (0, j))],
        out_specs=pl.BlockSpec((rows, tn), lambda j: (0, j)),
        out_shape=jax.ShapeDtypeStruct((rows, n), F32),
        compiler_params=pltpu.CompilerParams(dimension_semantics=("arbitrary",),
                                             vmem_limit_bytes=VMEM_LIMIT),
    )(cond, ada_w, ada_b.reshape(1, n))


def _inproj_body(x_ref, nw_ref, sc_ref, sh_ref, w_ref, wg_ref, y_ref, yg_ref, h_scr, *, tn):
    @pl.when(pl.program_id(2) == 0)
    def _():
        x = x_ref[0]
        ms = jnp.mean(x * x, axis=-1, keepdims=True)
        h = x * lax.rsqrt(ms + NORM_EPS) * nw_ref[...]
        h = (h * (1.0 + sc_ref[0]) + sh_ref[0]).astype(BF16)
        h_scr[...] = h
        yg_ref[0] = _dot_nt(wg_ref[...], h)

    acc = _dot(h_scr[...], w_ref[...])
    for j in range(tn // HEAD_DIM):
        y_ref[0, j] = acc[:, j * HEAD_DIM:(j + 1) * HEAD_DIM].astype(y_ref.dtype)


def _inproj(x, norm_w, scale, shift, w_main, w_gate_t, tm):
    bsz, length, d = x.shape
    tn = IN_TN
    grid = (bsz, length // tm, N_MAIN // tn)
    return pl.pallas_call(
        functools.partial(_inproj_body, tn=tn),
        grid=grid,
        in_specs=[pl.BlockSpec((1, tm, d), lambda b, m, n: (b, m, 0)),
                  pl.BlockSpec((1, d), lambda b, m, n: (0, 0)),
                  pl.BlockSpec((1, 1, d), lambda b, m, n: (b, 0, 0)),
                  pl.BlockSpec((1, 1, d), lambda b, m, n: (b, 0, 0)),
                  pl.BlockSpec((d, tn), lambda b, m, n: (0, n)),
                  pl.BlockSpec((N_GATE, d), lambda b, m, n: (0, 0))],
        out_specs=[pl.BlockSpec((1, tn // HEAD_DIM, tm, HEAD_DIM), lambda b, m, n: (b, n, m, 0)),
                   pl.BlockSpec((1, N_GATE, tm), lambda b, m, n: (b, 0, m))],
        out_shape=[jax.ShapeDtypeStruct((bsz, N_MAIN // HEAD_DIM, length, HEAD_DIM), BF16),
                   jax.ShapeDtypeStruct((bsz, N_GATE, length), F32)],
        scratch_shapes=[pltpu.VMEM((tm, d), BF16)],
        compiler_params=pltpu.CompilerParams(
            dimension_semantics=("parallel", "parallel", "arbitrary"),
            vmem_limit_bytes=VMEM_LIMIT),
    )(x, norm_w, scale, shift, w_main, w_gate_t)


CONV_PAD = 72
CONV_HALO = 8


def _conv_body(x_ref, w_ref, o_ref, pad_scr, *, length, width, two_d, rt):
    blk = pl.program_id(1)
    zeros = jnp.zeros((CONV_PAD, HEAD_DIM), F32)
    pad_scr[0:CONV_PAD, :] = zeros
    pad_scr[CONV_PAD + length:CONV_PAD + length + CONV_PAD, :] = zeros
    pad_scr[CONV_PAD:CONV_PAD + length, :] = x_ref[0, 0].astype(F32)
    w = 0.5 * w_ref[0]
    win_rows = rt + 2 * CONV_HALO
    is_q = blk < N_HEADS
    is_qk = blk < 2 * N_HEADS
    col = lax.broadcasted_iota(jnp.int32, (rt, HEAD_DIM), 0) & (width - 1)

    def tile(i, carry):
        s = pl.multiple_of(i * rt, rt)
        sums = [None, None, None]
        for dr in ((-1, 0, 1) if two_d else (0,)):
            base = pl.multiple_of(s + (CONV_PAD + dr * width - CONV_HALO), 8)
            win = pad_scr[pl.ds(base, win_rows), :]
            for j in range(3):
                tap = (dr + 1) * 3 + j
                term = win * w[tap:tap + 1, :]
                sums[j] = term if sums[j] is None else sums[j] + term
        inner = slice(CONV_HALO, CONV_HALO + rt)
        left = pltpu.roll(sums[0], 1, 0)[inner]
        right = pltpu.roll(sums[2], win_rows - 1, 0)[inner]
        hs = sums[1][inner] + jnp.where(col >= 1, left, 0.0) + jnp.where(col <= width - 2, right, 0.0)
        a = hs + hs * jnp.tanh(hs)
        nrm = lax.rsqrt(jnp.sum(a * a, axis=-1, keepdims=True) + NORM_EPS)
        f = jnp.where(is_q, nrm * HEAD_DIM ** -0.5, jnp.where(is_qk, nrm, 1.0))
        o_ref[0, 0, pl.ds(s, rt), :] = (a * f).astype(o_ref.dtype)
        return carry

    lax.fori_loop(0, length // rt, tile, 0)


def _gdn_conv(yh, conv_w, two_d, width=GRID_W):
    bsz, _, length, _ = yh.shape
    assert width & (width - 1) == 0, "grid width must be a power of two (column index by bit mask)"
    rt = min(CONV_RT, length)
    assert rt % width == 0
    nblk = 3 * N_HEADS
    return pl.pallas_call(
        functools.partial(_conv_body, length=length, width=width, two_d=two_d, rt=rt),
        grid=(bsz, nblk),
        in_specs=[pl.BlockSpec((1, 1, length, HEAD_DIM), lambda b, j: (b, OFF_BQ + j, 0, 0)),
                  pl.BlockSpec((1, 9, HEAD_DIM), lambda b, j: (j, 0, 0))],
        out_specs=pl.BlockSpec((1, 1, length, HEAD_DIM), lambda b, j: (b, j, 0, 0)),
        out_shape=jax.ShapeDtypeStruct((bsz, nblk, length, HEAD_DIM), BF16),
        scratch_shapes=[pltpu.VMEM((length + 2 * CONV_PAD, HEAD_DIM), F32)],
        compiler_params=pltpu.CompilerParams(dimension_semantics=("parallel", "parallel"),
                                             vmem_limit_bytes=VMEM_LIMIT),
    )(yh, conv_w)


HG_LEVELS = (32, 16, 8, 4, 2, 1)
HG_PAIRS = ((0, 1), (2, 3), (4, 5))
HG_GROUP = 4


def _hg_constants():
    t = np.arange(CHUNK)[:, None]
    col = np.arange(HEAD_DIM)[None, :]
    s = col % CHUNK
    out = [np.broadcast_to(np.where((t & m) != 0, 1.0, -1.0), (CHUNK, HEAD_DIM)) for m in HG_LEVELS]
    for rev in (False, True):
        for pa, pb in HG_PAIRS:
            keep = np.zeros((CHUNK, HEAD_DIM), bool)
            for half, lv in ((col < CHUNK, pa), (col >= CHUNK, pb)):
                m = HG_LEVELS[lv]
                split = ((t ^ s) >> (m.bit_length() - 1)) == 1
                t_hi = (t & m) != 0
                keep |= half & split & (~t_hi if rev else t_hi)
            out.append(keep)
    out.append((col == t) & (col < CHUNK))
    return jnp.asarray(np.stack([np.asarray(o, np.float32) for o in out]))


def _hg_triangles():
    t = np.arange(CHUNK)
    lower = (t[None, :] <= t[:, None]).astype(np.float32)
    return jnp.asarray(np.stack([np.tile(lower, (1, 3)), np.tile(lower.T, (1, 3))]), BF16)


def _cumsum_rows(g, tri3):
    hi = g.astype(BF16)
    r1 = g - hi.astype(F32)
    mid = r1.astype(BF16)
    lo = (r1 - mid.astype(F32)).astype(BF16)
    return _dot(tri3, jnp.concatenate([hi, mid, lo], axis=0))


def _level_operand(b, b_ref, q, k, m, rev):
    parts = []
    for blk in range(CHUNK // m):
        rows = slice(blk * m, (blk + 1) * m)
        r = (blk // 2) * 2 * m + m
        ref = jnp.broadcast_to(b_ref[r:r + 1, :], (m, HEAD_DIM))
        q_side = (blk % 2 == 1) != rev
        e = (b[rows] - ref) if q_side else (ref - b[rows])
        parts.append(((q if q_side else k)[rows] * jnp.exp2(e)).astype(BF16))
    return jnp.concatenate(parts, axis=0)


def _level_ref(b_ref, m, sub):
    bc = lambda r, n: jnp.broadcast_to(b_ref[r:r + 1, :], (n, HEAD_DIM))
    if m >= 4:
        return jnp.concatenate([bc(blk * 2 * m + m, 2 * m) for blk in range(CHUNK // (2 * m))], axis=0)
    lo = jnp.concatenate([bc(v * 8 + 2, 8) for v in range(CHUNK // 8)], axis=0)
    hi = jnp.concatenate([bc(v * 8 + 6, 8) for v in range(CHUNK // 8)], axis=0)
    return jnp.where(sub < 4, lo, hi)


def _hg_prepare(items, readout, cst_ref, tri_ref, b_scr, sc_scr, vv_scr, u_scr, dec_scr):
    row = lax.broadcasted_iota(jnp.int32, (CHUNK, HEAD_DIM), 0)
    sub = row & 7
    even = (row & 1) == 0
    gs, ks, bs = [], [], []
    for i, (_, _, load_f, _, lb, rev) in enumerate(items):
        gate = (1.0 - lb) * (0.5 + 0.5 * jnp.tanh(0.5 * load_f()))
        g = jnp.log2(jnp.maximum(lb + gate, F32_TINY))
        gs.append(g)
        ks.append((1.0 - lb) - gate)
        b = _cumsum_rows(g, tri_ref[1 if rev else 0])
        bs.append(b)
        b_scr[i] = b
    for i, (slot, _, _, load_v, _, rev) in enumerate(items):
        b = bs[i]
        btot = b[0:1, :] if rev else b[CHUNK - 1:CHUNK, :]
        v = load_v()
        vv_t = jnp.concatenate([v, v], axis=0).T.astype(BF16)
        k_dec = (ks[i] * jnp.exp2(btot - b)).astype(BF16)
        u_scr[slot] = _dot(vv_t[:, :CHUNK], k_dec)
        dec_scr[slot] = jnp.exp2(btot)
        vv_scr[slot] = vv_t
    if not readout:
        return

    qs = []
    for _, load_q, _, _, _, _ in items:
        q_raw = load_q()
        hs = (0.5 * HEAD_DIM ** -0.5) * q_raw
        qs.append(hs + hs * jnp.tanh(0.5 * q_raw))
    scores = [jnp.zeros((CHUNK, HEAD_DIM), F32) for _ in items]
    zero_blk = jnp.zeros((CHUNK, HEAD_DIM), BF16)
    for pi, pair in enumerate(HG_PAIRS):
        gps = []
        for i, (_, _, _, _, _, rev) in enumerate(items):
            a_mats = []
            for lv in pair:
                m = HG_LEVELS[lv]
                if m >= 8:
                    a_mats.append(_level_operand(bs[i], b_scr.at[i], qs[i], ks[i], m, rev))
                    continue
                sign = cst_ref[lv]
                if m == 1:
                    x = jnp.exp2(jnp.where(even, gs[i] if rev else pltpu.roll(gs[i], CHUNK - 1, 0), 0.0))
                else:
                    d = bs[i] - _level_ref(b_scr.at[i], m, sub)
                    x = jnp.exp2((-d if rev else d) * sign)
                t_is_q = (sign < 0.0) if rev else (sign > 0.0)
                a_mats.append((jnp.where(t_is_q, qs[i], ks[i]) * x).astype(BF16))
            lhs = jnp.concatenate(a_mats, axis=1)
            rhs = jnp.concatenate([jnp.concatenate([a_mats[0], zero_blk], axis=1),
                                   jnp.concatenate([zero_blk, a_mats[1]], axis=1)], axis=0)
            gps.append(_dot_nt(lhs, rhs))
        for i, (_, _, _, _, _, rev) in enumerate(items):
            scores[i] = scores[i] + gps[i] * cst_ref[6 + (3 if rev else 0) + pi]
    for i, (slot, _, _, _, _, _) in enumerate(items):
        diag = jnp.sum(qs[i] * ks[i], axis=-1, keepdims=True) * cst_ref[12]
        sc_scr[slot, :, :HEAD_DIM] = (scores[i] + diag).astype(BF16)
        sc_scr[slot, :, HEAD_DIM:] = (qs[i] * jnp.exp2(bs[i])).astype(BF16)


def _hg_recur(steps, dirs, readout, sc_scr, vv_scr, u_scr, dec_scr):
    states = [s_scr[...] for s_scr, _, _ in dirs]
    for c in steps:
        for i, (_, slot_of, store_o) in enumerate(dirs):
            slot = slot_of(c)
            if readout:
                rhs = jnp.concatenate([vv_scr[slot], states[i].astype(BF16)], axis=1)
                store_o(c, _dot_nt(sc_scr[slot], rhs))
            states[i] = states[i] * dec_scr[slot] + u_scr[slot]
    for (s_scr, _, _), st in zip(dirs, states):
        s_scr[...] = st


def _hg_body(cst_ref, tri_ref, qf_ref, ff_ref, vf_ref, qb_ref, fb_ref, vb_ref,
             cff_ref, cfb_ref, cv_ref, lb_ref, of_ref, ob_ref,
             sf_scr, sb_scr, b_scr, sc_scr, vv_scr, u_scr, dec_scr, *, n_chunks, n_ctx_chunks):
    lb_f = lb_ref[0, 0:1, :]
    lb_b = lb_ref[0, 1:2, :]
    scr = (sc_scr, vv_scr, u_scr, dec_scr)
    load = lambda ref, rows: (lambda: ref[0, 0, rows, :].astype(F32))

    @pl.when(pl.program_id(2) == 0)
    def _():
        sf_scr[...] = jnp.zeros_like(sf_scr)
        sb_scr[...] = jnp.zeros_like(sb_scr)
        n = n_ctx_chunks
        for c0 in range(0, n, HG_GROUP):
            items = []
            for c in range(c0, min(c0 + HG_GROUP, n)):
                sl = slice(c * CHUNK, (c + 1) * CHUNK)
                items.append((c, None, load(cff_ref, sl), load(cv_ref, sl), lb_f, False))
                items.append((n + c, None, load(cfb_ref, sl), load(cv_ref, sl), lb_b, True))
            _hg_prepare(items, False, cst_ref, tri_ref, b_scr, *scr)
        _hg_recur(range(n), [(sf_scr, lambda c: c, None), (sb_scr, lambda c: 2 * n - 1 - c, None)],
                  False, *scr)

    n = n_chunks
    rows = lambda c: slice(c * CHUNK, (c + 1) * CHUNK)

    def prepare(g):
        items = []
        for p in range(g * HG_GROUP, (g + 1) * HG_GROUP):
            rf, rb = rows(p), rows(n - 1 - p)
            items.append((p, load(qf_ref, rf), load(ff_ref, rf), load(vf_ref, rf), lb_f, False))
            items.append((2 * n - 1 - p, load(qb_ref, rb), load(fb_ref, rb), load(vb_ref, rb), lb_b, True))
        _hg_prepare(items, True, cst_ref, tri_ref, b_scr, *scr)

    def store_f(c, o):
        of_ref[0, 0, rows(c), :] = o.astype(of_ref.dtype)

    def store_b(c, o):
        ob_ref[0, 0, rows(n - 1 - c), :] = o.astype(ob_ref.dtype)

    def recur(g):
        _hg_recur(range(g * HG_GROUP, (g + 1) * HG_GROUP),
                  [(sf_scr, lambda c: c, store_f), (sb_scr, lambda c: 2 * n - 1 - c, store_b)], True, *scr)

    groups = n // HG_GROUP
    prepare(0)
    for g in range(1, groups):
        recur(g - 1)
        prepare(g)
    recur(groups - 1)


def _hgrn2_scan(yh, yh_ctx, lb, tb):
    bsz, _, length, _ = yh.shape
    lc = yh_ctx.shape[2] // bsz
    nt = length // tb
    nc, ncc = tb // CHUNK, lc // CHUNK
    assert nc % HG_GROUP == 0
    slots = 2 * max(nc, ncc)
    blk = (1, 1, tb, HEAD_DIM)
    cblk = (1, 1, lc, HEAD_DIM)
    fwd = lambda off: pl.BlockSpec(blk, lambda b, h, t: (b, off + h, t, 0))
    bwd = lambda off: pl.BlockSpec(blk, lambda b, h, t: (b, off + h, nt - 1 - t, 0))
    ctx = lambda off: pl.BlockSpec(cblk, lambda b, h, t: (0, off + h, b, 0))
    consts = _hg_constants()
    tri3 = _hg_triangles()
    out_shape = jax.ShapeDtypeStruct((bsz, N_HEADS, length, HEAD_DIM), BF16)
    return pl.pallas_call(
        functools.partial(_hg_body, n_chunks=nc, n_ctx_chunks=ncc),
        grid=(bsz, N_HEADS, nt),
        in_specs=[pl.BlockSpec(consts.shape, lambda b, h, t: (0, 0, 0)),
                  pl.BlockSpec(tri3.shape, lambda b, h, t: (0, 0, 0)),
                  fwd(OFF_AQ), fwd(OFF_AFF), fwd(OFF_AI), bwd(OFF_AQ), bwd(OFF_AFB), bwd(OFF_AI),
                  ctx(OFF_AFF), ctx(OFF_AFB), ctx(OFF_AI),
                  pl.BlockSpec((1, 2, HEAD_DIM), lambda b, h, t: (h, 0, 0))],
        out_specs=[pl.BlockSpec(blk, lambda b, h, t: (b, h, t, 0)),
                   pl.BlockSpec(blk, lambda b, h, t: (b, h, nt - 1 - t, 0))],
        out_shape=[out_shape, out_shape],
        scratch_shapes=[pltpu.VMEM((HEAD_DIM, HEAD_DIM), F32), pltpu.VMEM((HEAD_DIM, HEAD_DIM), F32),
                        pltpu.VMEM((2 * HG_GROUP, CHUNK, HEAD_DIM), F32),
                        pltpu.VMEM((slots, CHUNK, 2 * HEAD_DIM), BF16),
                        pltpu.VMEM((slots, HEAD_DIM, HEAD_DIM), BF16),
                        pltpu.VMEM((slots, HEAD_DIM, HEAD_DIM), F32),
                        pltpu.VMEM((slots, 1, HEAD_DIM), F32)],
        compiler_params=pltpu.CompilerParams(
            dimension_semantics=("parallel", "parallel", "arbitrary"),
            vmem_limit_bytes=VMEM_LIMIT),
    )(consts, tri3, yh, yh, yh, yh, yh, yh, yh_ctx, yh_ctx, yh_ctx, lb)


def _seg_cumsum_lanes(x, rev, lane):
    total = x.shape[1]
    seg = lane & (CHUNK - 1)
    for sh in (1, 2, 4, 8, 16, 32):
        if rev:
            x = x + jnp.where(seg < CHUNK - sh, pltpu.roll(x, total - sh, 1), 0.0)
        else:
            x = x + jnp.where(seg >= sh, pltpu.roll(x, sh, 1), 0.0)
    return x


def _gate_rows_body(g_ref, p_ref, o_ref):
    nrow = 2 * N_HEADS
    total = g_ref.shape[2]
    lane = lax.broadcasted_iota(jnp.int32, (nrow, total), 1)
    row = lax.broadcasted_iota(jnp.int32, (nrow, total), 0)
    z = g_ref[0, :nrow, :] + p_ref[:, 1:2]
    softplus = jnp.maximum(z, 0.0) + jnp.log1p(jnp.exp(-jnp.abs(z)))
    g = -jnp.exp(p_ref[:, 0:1]) * softplus
    prefix = _seg_cumsum_lanes(g, False, lane)
    suffix = _seg_cumsum_lanes(g, True, lane)
    o_ref[0, 0] = jnp.where(row < N_HEADS, prefix, suffix)
    o_ref[0, 1] = _sigmoid(g_ref[0, nrow:, :])
    o_ref[0, 2] = prefix + suffix - g


def _gate_rows(gates, params):
    bsz, _, length = gates.shape
    tg = min(GATE_TG, length)
    nrow = 2 * N_HEADS
    return pl.pallas_call(
        _gate_rows_body,
        grid=(bsz, length // tg),
        in_specs=[pl.BlockSpec((1, N_GATE, tg), lambda b, t: (b, 0, t)),
                  pl.BlockSpec(params.shape, lambda b, t: (0, 0))],
        out_specs=pl.BlockSpec((1, 3, nrow, tg), lambda b, t: (b, 0, 0, t)),
        out_shape=jax.ShapeDtypeStruct((bsz, 3, nrow, length), F32),
        compiler_params=pltpu.CompilerParams(dimension_semantics=("parallel", "parallel"),
                                             vmem_limit_bytes=VMEM_LIMIT),
    )(gates, params)


def _gdn_gates(g_ref, d, head):
    r = pl.ds(d * N_HEADS + head, 1)
    b = g_ref[0, 0, r, :]
    rows = jnp.concatenate([b, g_ref[0, 1, r, :], g_ref[0, 2, r, :]], axis=0)
    return rows, jnp.broadcast_to(b, (HEAD_DIM, b.shape[1])).T


GDN_QUAD = 4
GDN_PREP_HEADS = 4


def _gdn_prepare(quads, readout, qkg_scr, tb_scr, kp_scr, dec_scr, between=()):
    wide = GDN_QUAD * CHUNK
    t_i = lax.broadcasted_iota(jnp.int32, (CHUNK, wide), 0)
    lane = lax.broadcasted_iota(jnp.int32, (CHUNK, wide), 1)
    s_i = lane & (CHUNK - 1)
    blk = [(lane >= j * CHUNK) & (lane < (j + 1) * CHUNK) for j in range(GDN_QUAD)]
    eye = (t_i == s_i).astype(F32)
    tri = {False: (s_i <= t_i, s_i < t_i), True: (s_i >= t_i, s_i > t_i)}
    rows_of = lambda j: slice(j * CHUNK, (j + 1) * CHUNK)

    def diag_blocks(m, off):
        out = jnp.where(blk[0], m[off:off + CHUNK], 0.0)
        for j in range(1, GDN_QUAD):
            out = jnp.where(blk[j], m[off + j * CHUNK:off + (j + 1) * CHUNK], out)
        return out

    def block_diag(y):
        return jnp.concatenate([jnp.where(blk[j], y, 0.0).astype(BF16) for j in range(GDN_QUAD)], axis=0)

    dmats, qks, xs, ys = [], [], [], []
    for _, load_q, load_k, rows, bcol, rev in quads:
        incl, strict = tri[rev]
        bc = jnp.concatenate([bcol, bcol], axis=1)
        dmat = jnp.exp(jnp.where(incl, diag_blocks(bc, 0) - rows[0:1, :], NEG_BIG))
        kb = load_k()
        qb = load_q() if readout else None
        pair = 2 * CHUNK
        lo_half = lax.broadcasted_iota(jnp.int32, (CHUNK, pair), 1) < CHUNK
        pair_diag = lambda m: jnp.where(lo_half, m[:CHUNK], m[CHUNK:])
        kk_parts, qk_parts = [], []
        for p in range(GDN_QUAD // 2):
            kp = kb[p * pair:(p + 1) * pair]
            if readout:
                qkk = _dot_nt(jnp.concatenate([qb[p * pair:(p + 1) * pair], kp], axis=0), kp)
                qk_parts.append(pair_diag(qkk[:pair]))
                kk_parts.append(pair_diag(qkk[pair:]))
            else:
                kk_parts.append(pair_diag(_dot_nt(kp, kp)))
        kk = jnp.concatenate(kk_parts, axis=1)
        if readout:
            qks.append(jnp.concatenate(qk_parts, axis=1))
        n_mat = jnp.where(strict, kk * dmat, 0.0) * rows[1:2, :]
        dmats.append(dmat)
        xs.append(eye)
        ys.append(n_mat)
    between = list(between)
    for m in (1, 2, 4, 8, 16, 32):
        if m > 1 and between:
            between.pop(0)()
        opposite = ((t_i ^ s_i) >> (m.bit_length() - 1)) == 1
        later = (t_i & m) != 0
        lvl = {False: opposite & later, True: opposite & jnp.logical_not(later)}
        n_lvl = [jnp.where(lvl[q[5]], y, 0.0) for q, y in zip(quads, ys)]
        if m == 1:
            xs = [x - nl for x, nl in zip(xs, n_lvl)]
            continue
        inner = [_dot(nl.astype(BF16), block_diag(x)) for nl, x in zip(n_lvl, xs)]
        xs = [x - _dot(x.astype(BF16), block_diag(a)) for x, a in zip(xs, inner)]
    for work in between:
        work()

    for i, (slot0, load_q, load_k, rows, bcol, rev) in enumerate(quads):
        k_all = load_k().astype(F32)
        e_col = jnp.exp(bcol)
        qkg_all_k = (k_all * e_col).astype(BF16)
        if readout:
            qkg_all_q = (load_q().astype(F32) * e_col).astype(BF16)
            pm_all = qks[i] * dmats[i] * rows[1:2, :]
        k_t = k_all.T
        kdt_all = (k_t * (jnp.exp(rows[2:3, :] - rows[0:1, :]) * rows[1:2, :])).astype(BF16)
        for j in range(GDN_QUAD):
            slot = slot0 + j
            shift = (wide - j * CHUNK) % wide
            take = lambda m: (m if shift == 0 else pltpu.roll(m, shift, 1))[:, :CHUNK]
            tb_scr[slot] = take(xs[i]).astype(BF16)
            qkg_scr[slot, CHUNK:, :] = qkg_all_k[rows_of(j)]
            if readout:
                qkg_scr[slot, :CHUNK, :] = qkg_all_q[rows_of(j)]
                kp_scr[slot, HEAD_DIM:, :] = take(pm_all).astype(BF16)
            kp_scr[slot, :HEAD_DIM, :] = kdt_all[:, j * CHUNK:(j + 1) * CHUNK]
            tot = rows[2:3, j * CHUNK:(j + 1) * CHUNK]
            dec_scr[slot] = jnp.exp(jnp.concatenate([tot, tot], axis=1))


def _gdn_recur(steps, dirs, readout, qkg_scr, tb_scr, kp_scr, dec_scr):
    def step(c, carry):
        slots = [slot_of(c) for _, slot_of, _, _ in dirs]
        states = [s_ref[...] for s_ref, _, _, _ in dirs]
        if readout:
            qks = [_dot(qkg_scr[slot], s.astype(BF16)) for slot, s in zip(slots, states)]
            kss = [p[CHUNK:] for p in qks]
        else:
            kss = [_dot(qkg_scr[slot, CHUNK:, :], s.astype(BF16)) for slot, s in zip(slots, states)]
        ws = [_dot(tb_scr[slot], (load_v(c) - ks).astype(BF16)).astype(BF16)
              for slot, ks, (_, _, load_v, _) in zip(slots, kss, dirs)]
        if readout:
            res = [_dot(kp_scr[slot], w) for slot, w in zip(slots, ws)]
            for slot, s, r, p, (s_ref, _, _, store_o) in zip(slots, states, res, qks, dirs):
                s_ref[...] = s * dec_scr[slot] + r[:HEAD_DIM]
                store_o(c, p[:CHUNK] + r[HEAD_DIM:])
        else:
            for slot, s, w, (s_ref, _, _, _) in zip(slots, states, ws, dirs):
                s_ref[...] = s * dec_scr[slot] + _dot(kp_scr[slot, :HEAD_DIM, :], w)
        return carry

    if isinstance(steps, int):
        lax.fori_loop(0, steps, step, 0, unroll=4)
    else:
        for c in steps:
            step(c, 0)


def _gdn_body(qf_ref, kf_ref, vf_ref, gf_ref, qb_ref, kb_ref, vb_ref, gb_ref,
              ck_ref, cv_ref, cg_ref, of_ref, ob_ref,
              s_scr, qkg_scr, tb_scr, kp_scr, dec_scr,
              *, n_chunks, n_ctx_chunks, heads, n_blocks):
    t = pl.program_id(2)
    h0 = pl.program_id(1) * heads
    scr = (qkg_scr, tb_scr, kp_scr, dec_scr)
    set_slots = 2 * heads * n_chunks

    def rows(c):
        return slice(c * CHUNK, (c + 1) * CHUNK) if isinstance(c, int) else pl.ds(pl.multiple_of(c * CHUNK, CHUNK), CHUNK)

    def chunk_list(hh, n, base, q_refs, k_refs, g_refs):
        head = h0 + hh
        wide = GDN_QUAD * CHUNK
        out = []
        for d in (0, 1):
            gate_rows, bcol = _gdn_gates(g_refs[d], d, head)
            for c0 in range(0, n, GDN_QUAD):
                sl = slice(c0 * CHUNK, c0 * CHUNK + wide)
                load_q = None if q_refs is None else (lambda r=q_refs[d], sl=sl: r[0, hh, sl, :])
                load_k = lambda r=k_refs[d], sl=sl: r[0, hh, sl, :]
                out.append((base + (2 * hh + d) * n + c0, load_q, load_k, gate_rows[:, sl], bcol[sl, :], d == 1))
        return out

    def prepare(hp, n, base, q_refs, k_refs, g_refs, readout, between=()):
        quads = []
        for i in range(GDN_PREP_HEADS):
            quads += chunk_list(hp * GDN_PREP_HEADS + i, n, base, q_refs, k_refs, g_refs)
        _gdn_prepare(quads, readout, *scr, between=between)

    def chains(n, base, vf, vb, of, ob):
        out = []
        for hh in range(heads):
            out.append((s_scr.at[2 * hh], lambda c, hh=hh: base + 2 * hh * n + c,
                        lambda c, hh=hh: vf[0, hh, rows(c), :].astype(F32),
                        None if of is None else (lambda c, o, hh=hh: of.__setitem__(
                            (0, hh, rows(c), slice(None)), o.astype(of.dtype)))))
            out.append((s_scr.at[2 * hh + 1], lambda c, hh=hh: base + (2 * hh + 1) * n + (n - 1 - c),
                        lambda c, hh=hh: vb[0, hh, rows(n - 1 - c), :].astype(F32),
                        None if ob is None else (lambda c, o, hh=hh: ob.__setitem__(
                            (0, hh, rows(n - 1 - c), slice(None)), o.astype(ob.dtype)))))
        return out

    main_refs = ((qf_ref, qb_ref), (kf_ref, kb_ref), (gf_ref, gb_ref))
    groups = heads // GDN_PREP_HEADS
    n = n_chunks

    @pl.when(t == 0)
    def _():
        s_scr[...] = jnp.zeros_like(s_scr)
        nc = n_ctx_chunks

        def ctx_head(hp, carry):
            prepare(hp, nc, 0, None, (ck_ref, ck_ref), (cg_ref, cg_ref), False)
            return carry

        lax.fori_loop(0, groups, ctx_head, 0)
        _gdn_recur(nc, chains(nc, 0, cv_ref, cv_ref, None, None), False, *scr)

        def first_head(hp, carry):
            prepare(hp, n, 0, *main_refs, True)
            return carry

        lax.fori_loop(0, groups, first_head, 0)

    @pl.when((t > 0) & (t < n_blocks))
    def _():
        base_p = (t & 1) * set_slots
        base_r = set_slots - base_p
        per = n // groups
        scan = chains(n, base_r, vf_ref, vb_ref, of_ref, ob_ref)
        for g in range(groups):
            positions = [functools.partial(_gdn_recur, range(p, p + 1), scan, True, *scr)
                         for p in range(g * per, (g + 1) * per)]
            prepare(g, n, base_p, *main_refs, True, between=positions)

    @pl.when(t == n_blocks)
    def _():
        base_r = ((n_blocks - 1) & 1) * set_slots
        _gdn_recur(n, chains(n, base_r, vf_ref, vb_ref, of_ref, ob_ref), True, *scr)


GDN_HEADS = 8


def _gdn_scan(qkv, gates, qkv_ctx, gates_ctx, tb):
    bsz, _, length, _ = qkv.shape
    lc = qkv_ctx.shape[2] // bsz
    nt = length // tb
    nc, ncc = tb // CHUNK, lc // CHUNK
    hb = GDN_HEADS
    assert nc % GDN_QUAD == 0 and ncc % GDN_QUAD == 0 and hb % GDN_PREP_HEADS == 0
    assert nc % (hb // GDN_PREP_HEADS) == 0 and ncc <= 2 * nc
    slots = 2 * 2 * hb * nc
    blk = (1, hb, tb, HEAD_DIM)
    cblk = (1, hb, lc, HEAD_DIM)
    gblk = (1, 3, 2 * N_HEADS, tb)
    prep = lambda t: jnp.minimum(t, nt - 1)
    scan = lambda t: jnp.maximum(t - 1, 0)
    fwd = lambda off, tt: pl.BlockSpec(blk, lambda b, h, t: (b, off // hb + h, tt(t), 0))
    bwd = lambda off, tt: pl.BlockSpec(blk, lambda b, h, t: (b, off // hb + h, nt - 1 - tt(t), 0))
    ctx = lambda off: pl.BlockSpec(cblk, lambda b, h, t: (0, off // hb + h, b, 0))
    out_shape = jax.ShapeDtypeStruct((bsz, N_HEADS, length, HEAD_DIM), BF16)
    return pl.pallas_call(
        functools.partial(_gdn_body, n_chunks=nc, n_ctx_chunks=ncc, heads=hb, n_blocks=nt),
        grid=(bsz, N_HEADS // hb, nt + 1),
        in_specs=[fwd(0, prep), fwd(N_HEADS, prep), fwd(2 * N_HEADS, scan),
                  pl.BlockSpec(gblk, lambda b, h, t: (b, 0, 0, prep(t))),
                  bwd(0, prep), bwd(N_HEADS, prep), bwd(2 * N_HEADS, scan),
                  pl.BlockSpec(gblk, lambda b, h, t: (b, 0, 0, nt - 1 - prep(t))),
                  ctx(N_HEADS), ctx(2 * N_HEADS),
                  pl.BlockSpec((1, 3, 2 * N_HEADS, lc), lambda b, h, t: (0, 0, 0, b))],
        out_specs=[pl.BlockSpec(blk, lambda b, h, t: (b, h, scan(t), 0)),
                   pl.BlockSpec(blk, lambda b, h, t: (b, h, nt - 1 - scan(t), 0))],
        out_shape=[out_shape, out_shape],
        scratch_shapes=[pltpu.VMEM((2 * hb, HEAD_DIM, HEAD_DIM), F32),
                        pltpu.VMEM((slots, 2 * CHUNK, HEAD_DIM), BF16),
                        pltpu.VMEM((slots, CHUNK, CHUNK), BF16),
                        pltpu.VMEM((slots, HEAD_DIM + CHUNK, CHUNK), BF16),
                        pltpu.VMEM((slots, 1, HEAD_DIM), F32)],
        compiler_params=pltpu.CompilerParams(
            dimension_semantics=("parallel", "parallel", "arbitrary"),
            vmem_limit_bytes=VMEM_LIMIT),
    )(qkv, qkv, qkv, gates, qkv, qkv, qkv, gates, qkv_ctx, qkv_ctx, gates_ctx)


def _out_body(oaf_ref, oab_ref, obf_ref, obb_ref, za_ref, zb_ref, naw_ref, nbw_ref,
              w_ref, x_ref, gate_ref, fw_ref, o_ref):
    parts = []
    for of_ref, ob_ref, z_ref, nw_ref in ((oaf_ref, oab_ref, za_ref, naw_ref),
                                          (obf_ref, obb_ref, zb_ref, nbw_ref)):
        for h in range(N_HEADS):
            o = of_ref[0, h].astype(F32) + ob_ref[0, h].astype(F32)
            o = o * lax.rsqrt(jnp.mean(o * o, axis=-1, keepdims=True) + NORM_EPS) * nw_ref[h:h + 1, :]
            hz = 0.5 * z_ref[0, h].astype(F32)
            parts.append(((hz + hz * jnp.tanh(hz)) * o).astype(BF16))
    y = jnp.concatenate(parts, axis=1)
    xo = x_ref[0] + gate_ref[0] * _dot(y, w_ref[...])
    ms = jnp.mean(xo * xo, axis=-1, keepdims=True)
    o_ref[0] = xo * lax.rsqrt(ms + NORM_EPS) * fw_ref[...]


def _out_stage(oa_f, oa_b, ob_f, ob_b, yh, na_w, nb_w, w_out, x, gate, final_w, tm):
    bsz, length, d = x.shape
    hblk = (1, N_HEADS, tm, HEAD_DIM)
    ospec = pl.BlockSpec(hblk, lambda b, m: (b, 0, m, 0))
    full2 = lambda a: pl.BlockSpec(a.shape, lambda b, m: (0, 0))
    return pl.pallas_call(
        _out_body,
        grid=(bsz, length // tm),
        in_specs=[ospec, ospec, ospec, ospec,
                  pl.BlockSpec(hblk, lambda b, m: (b, OFF_AZ // N_HEADS, m, 0)),
                  pl.BlockSpec(hblk, lambda b, m: (b, OFF_BZ // N_HEADS, m, 0)),
                  full2(na_w), full2(nb_w), full2(w_out),
                  pl.BlockSpec((1, tm, d), lambda b, m: (b, m, 0)),
                  pl.BlockSpec((1, 1, d), lambda b, m: (b, 0, 0)),
                  full2(final_w)],
        out_specs=pl.BlockSpec((1, tm, d), lambda b, m: (b, m, 0)),
        out_shape=jax.ShapeDtypeStruct((bsz, length, d), F32),
        compiler_params=pltpu.CompilerParams(dimension_semantics=("parallel", "parallel"),
                                             vmem_limit_bytes=VMEM_LIMIT),
    )(oa_f, oa_b, ob_f, ob_b, yh, yh, na_w, nb_w, w_out, x, gate, final_w)


def kernel(x, c, ctx, c_ctx, norm_w, ada_w, ada_b, w_in, conv_w, hg_lb_logits, gdn_a_log,
           gdn_dt_bias, ha_norm_w, hb_norm_w, w_out, final_norm_w):
    bsz, length, d = x.shape
    lc = ctx.shape[1]
    assert d == D_MODEL and length % 512 == 0 and length % GRID_W == 0 and lc % CHUNK == 0
    assert w_in.shape[0] == 1, "single-layer block"

    n_cond = bsz + 1
    cond = jnp.concatenate([c, c_ctx[None, :], jnp.zeros((-n_cond % 8, d), F32)], axis=0)
    mod = _adaln(cond, ada_w[0], ada_b[0])
    shift, scale, gate = mod[:, :d], mod[:, d:2 * d], mod[:, 2 * d:]
    lat = lambda m: m[:bsz, None, :]
    of_ctx = lambda m: m[bsz:bsz + 1, None, :]

    w_main = w_in[0].astype(BF16)
    w_gate_t = w_main[:, N_MAIN:].T
    nw = norm_w[0].reshape(1, d)
    yh, gates = _inproj(x, nw, lat(scale), lat(shift), w_main, w_gate_t, tm=min(IN_TM, length))
    yh_c, gates_c = _inproj(ctx.reshape(1, bsz * lc, d), nw, of_ctx(scale), of_ctx(shift), w_main, w_gate_t,
                            tm=bsz * lc)

    lb = jax.nn.softmax(hg_lb_logits.astype(F32), axis=0)[0]
    lb = lb.reshape(2, N_HEADS, HEAD_DIM).transpose(1, 0, 2)
    oa_f, oa_b = _hgrn2_scan(yh, yh_c, lb, min(HG_TB, length))

    cw = conv_w[0].reshape(9, 3 * N_HEADS, HEAD_DIM).transpose(1, 0, 2)
    qkv = _gdn_conv(yh, cw, two_d=True)
    qkv_c = _gdn_conv(yh_c, cw, two_d=False, width=lc)
    params = jnp.stack([gdn_a_log[0].reshape(-1), gdn_dt_bias[0].reshape(-1)], axis=1).astype(F32)
    ob_f, ob_b = _gdn_scan(qkv, _gate_rows(gates, params), qkv_c, _gate_rows(gates_c, params),
                           min(GDN_TB, length))

    return _out_stage(oa_f, oa_b, ob_f, ob_b, yh, ha_norm_w[0], hb_norm_w[0],
                      w_out[0].astype(BF16), x, lat(gate), final_norm_w.reshape(1, d),
                      tm=min(OUT_TM, length))
```
